```python
import math
import jax, jax.numpy as jnp
from jax import lax
import numpy as np

D_MODEL = 1024
BATCH = 8
SEQ = 4096
DEPTH = 4
DEC_BATCH = 1
DEC_SEQ = 16384
PAST_LEN = 128

GRID_W = 64
DIFF_HEADS = 4
DIFF_DH = 64
DIFF_WIDTH = DIFF_HEADS * 2 * DIFF_DH
NA_HEADS = 8
NA_DH = 64
NA_WIDTH = NA_HEADS * NA_DH
NA_KH_MAX = 8
NA_KW = 16
NA_COL_BLOCK = 16
NA_COL_BAND = 32
MIX_WIDTH = DIFF_WIDTH + NA_WIDTH
IN_WIDTH = 3 * MIX_WIDTH
D_FF = 2816
N_EXPERTS = 8
TOP_K = 2
N_DENSE = (DEPTH + 1) // 2
N_MOE = DEPTH // 2
ROPE_THETA = 10000.0
EPS = 1e-6
Q_BLOCK = 128

kernel_name = "hymba_diffattn_natten_moe_encoder"


def rms_norm(x, g):
    xf = x.astype(jnp.float32)
    y = xf * lax.rsqrt(jnp.mean(xf * xf, axis=-1, keepdims=True) + EPS)
    return (y * g.astype(jnp.float32)).astype(x.dtype)


def rope_tables(seq, dim, dtype):
    inv = 1.0 / (ROPE_THETA ** (jnp.arange(0, dim, 2, dtype=jnp.float32) / dim))
    ang = jnp.arange(seq, dtype=jnp.float32)[:, None] * inv[None, :]
    ang = jnp.concatenate([ang, ang], axis=-1)
    return jnp.cos(ang).astype(dtype), jnp.sin(ang).astype(dtype)


def apply_rope(x, cos, sin):
    half = x.shape[-1] // 2
    rot = jnp.concatenate([-x[..., half:], x[..., :half]], axis=-1)
    shape = (1, x.shape[1]) + (1,) * (x.ndim - 3) + (x.shape[-1],)
    return x * cos.reshape(shape) + rot * sin.reshape(shape)


def diff_attention(q, k, v, lam, lam_init, g_sub, cos, sin):
    B, S = q.shape[0], q.shape[1]
    q = apply_rope(q, cos, sin)
    k = apply_rope(k, cos, sin)
    scale = DIFF_DH ** -0.5
    nb = S // Q_BLOCK
    qb = q.reshape(B, nb, Q_BLOCK, DIFF_HEADS, 2, DIFF_DH).transpose(1, 0, 2, 3, 4, 5)

    def block(q_blk):
        s = jnp.einsum('bqhmd,bkhmd->bhmqk', q_blk, k).astype(jnp.float32) * scale
        p = jax.nn.softmax(s, axis=-1)
        w = p[:, :, 0] - lam * p[:, :, 1]
        return jnp.einsum('bhqk,bkhe->bqhe', w.astype(v.dtype), v)

    o = lax.map(block, qb)
    o = o.transpose(1, 0, 2, 3, 4).reshape(B, S, DIFF_HEADS, 2 * DIFF_DH)
    o = rms_norm(o, g_sub) * (1.0 - lam_init)
    return o.reshape(B, S, DIFF_WIDTH)


def neighbourhood_attention(q, k, v, rpb):
    B, S = q.shape[0], q.shape[1]
    rows = S // GRID_W
    kh = min(NA_KH_MAX, rows)
    ncb = GRID_W // NA_COL_BLOCK
    scale = NA_DH ** -0.5
    qg = q.reshape(B, rows, ncb, NA_COL_BLOCK, NA_HEADS, NA_DH).transpose(1, 0, 2, 3, 4, 5)
    kg = k.reshape(B, rows, GRID_W, NA_HEADS, NA_DH)
    vg = v.reshape(B, rows, GRID_W, NA_HEADS, NA_DH)
    qcol = np.arange(GRID_W).reshape(ncb, NA_COL_BLOCK)
    band_start = np.clip(qcol[:, 0] - NA_KW // 2, 0, GRID_W - NA_COL_BAND)
    kcol = band_start[:, None] + np.arange(NA_COL_BAND)[None, :]
    win_start = np.clip(qcol - NA_KW // 2, 0, GRID_W - NA_KW)
    col_in = ((kcol[:, None, :] >= win_start[:, :, None]) &
              (kcol[:, None, :] < win_start[:, :, None] + NA_KW))
    dc_idx = np.clip(kcol[:, None, :] - qcol[:, :, None] + NA_KW - 1, 0, 2 * NA_KW - 2)
    rpb32 = rpb.astype(jnp.float32)
    col_mask = jnp.asarray(col_in)[None, None, :, :, None, :]

    def row_block(inp):
        r, q_r = inp
        r0 = jnp.clip(r - kh // 2, 0, rows - kh)
        k_r = lax.dynamic_slice_in_dim(kg, r0, kh, axis=1)
        v_r = lax.dynamic_slice_in_dim(vg, r0, kh, axis=1)
        k_b = k_r[:, :, kcol]
        v_b = v_r[:, :, kcol]
        s = jnp.einsum('bnchd,bjnkhd->bhncjk', q_r, k_b).astype(jnp.float32) * scale
        dr_idx = r0 + jnp.arange(kh) - r + NA_KH_MAX - 1
        bias = rpb32[:, dr_idx][:, :, dc_idx]
        s = s + bias.transpose(0, 2, 3, 1, 4)[None]
        s = jnp.where(col_mask, s, -jnp.inf)
        p = jax.nn.softmax(s, axis=(-2, -1))
        return jnp.einsum('bhncjk,bjnkhd->bnchd', p.astype(v_b.dtype), v_b)

    o = lax.map(row_block, (jnp.arange(rows), qg))
    return o.transpose(1, 0, 2, 3, 4, 5).reshape(B, S, NA_HEADS, NA_DH)


def swiglu(h, w_gate, w_up, w_down):
    return (jax.nn.silu(h @ w_gate) * (h @ w_up)) @ w_down


def moe_swiglu(h, w_router, w_gate, w_up, w_down):
    logits = (h @ w_router).astype(jnp.float32)
    top_v, top_i = lax.top_k(logits, TOP_K)
    gates = jax.nn.softmax(top_v, axis=-1)
    combine = jnp.sum(jax.nn.one_hot(top_i, N_EXPERTS, dtype=jnp.float32) * gates[..., None], axis=-2)
    combine = combine.astype(h.dtype)
    out = jnp.zeros_like(h)
    for e in range(N_EXPERTS):
        out = out + swiglu(h, w_gate[e], w_up[e], w_down[e]) * combine[..., e:e + 1]
    return out


def run_trunk(x, attn_norm, w_in, w_out, diff_q_norm, diff_k_norm, lambda_q, lambda_k,
              diff_sub_norm, na_q_norm, na_k_norm, na_rpb, na_out_norm, ffn_norm,
              dense_w_gate, dense_w_up, dense_w_down,
              moe_router, moe_w_gate, moe_w_up, moe_w_down):
    B, S, _ = x.shape
    cos, sin = rope_tables(S, DIFF_DH, x.dtype)
    for l in range(DEPTH):
        lam_init = 0.8 - 0.6 * math.exp(-0.3 * l)
        lq = lambda_q[l].astype(jnp.float32)
        lk = lambda_k[l].astype(jnp.float32)
        lam = jnp.exp(jnp.sum(lq[0] * lk[0])) - jnp.exp(jnp.sum(lq[1] * lk[1])) + lam_init
        h = rms_norm(x, attn_norm[l])
        proj = h @ w_in[l]
        dq, dk, dv, nq, nk, nv = jnp.split(proj, 6, axis=-1)
        dq = rms_norm(dq.reshape(B, S, DIFF_HEADS, 2, DIFF_DH), diff_q_norm[l])
        dk = rms_norm(dk.reshape(B, S, DIFF_HEADS, 2, DIFF_DH), diff_k_norm[l])
        dv = dv.reshape(B, S, DIFF_HEADS, 2 * DIFF_DH)
        a_out = diff_attention(dq, dk, dv, lam, lam_init, diff_sub_norm[l], cos, sin)
        nq = rms_norm(nq.reshape(B, S, NA_HEADS, NA_DH), na_q_norm[l])
        nk = rms_norm(nk.reshape(B, S, NA_HEADS, NA_DH), na_k_norm[l])
        nv = nv.reshape(B, S, NA_HEADS, NA_DH)
        b_out = rms_norm(neighbourhood_attention(nq, nk, nv, na_rpb[l]), na_out_norm[l])
        mixed = jnp.concatenate([a_out, b_out.reshape(B, S, NA_WIDTH)], axis=-1)
        x = x + mixed @ w_out[l]
        h = rms_norm(x, ffn_norm[l])
        if l % 2 == 0:
            i = l // 2
            x = x + swiglu(h, dense_w_gate[i], dense_w_up[i], dense_w_down[i])
        else:
            i = l // 2
            x = x + moe_swiglu(h, moe_router[i], moe_w_gate[i], moe_w_up[i], moe_w_down[i])
    return x


def setup_inputs(seed: int = 0) -> dict:
    key = jax.random.key(seed)
    ks = jax.random.split(key, 24)
    f32 = jnp.float32

    def nrm(k, shape, scale):
        return jax.random.normal(k, shape, f32) * scale

    def gain(k, shape):
        return 1.0 + 0.02 * jax.random.normal(k, shape, f32)

    return {
        "x_prompt": jax.random.normal(ks[0], (BATCH, SEQ, D_MODEL), f32),
        "x_sample": jax.random.normal(ks[1], (DEC_BATCH, DEC_SEQ, D_MODEL), f32),
        "attn_norm": gain(ks[2], (DEPTH, D_MODEL)),
        "w_in": nrm(ks[3], (DEPTH, D_MODEL, IN_WIDTH), D_MODEL ** -0.5),
        "w_out": nrm(ks[4], (DEPTH, MIX_WIDTH, D_MODEL), MIX_WIDTH ** -0.5),
        "diff_q_norm": gain(ks[5], (DEPTH, DIFF_DH)),
        "diff_k_norm": gain(ks[6], (DEPTH, DIFF_DH)),
        "lambda_q": nrm(ks[7], (DEPTH, 2, DIFF_DH), 0.1),
        "lambda_k": nrm(ks[8], (DEPTH, 2, DIFF_DH), 0.1),
        "diff_sub_norm": gain(ks[9], (DEPTH, 2 * DIFF_DH)),
        "na_q_norm": gain(ks[10], (DEPTH, NA_DH)),
        "na_k_norm": gain(ks[11], (DEPTH, NA_DH)),
        "na_rpb": nrm(ks[12], (DEPTH, NA_HEADS, 2 * NA_KH_MAX - 1, 2 * NA_KW - 1), 0.1),
        "na_out_norm": gain(ks[13], (DEPTH, NA_DH)),
        "ffn_norm": gain(ks[14], (DEPTH, D_MODEL)),
        "dense_w_gate": nrm(ks[15], (N_DENSE, D_MODEL, D_FF), D_MODEL ** -0.5),
        "dense_w_up": nrm(ks[16], (N_DENSE, D_MODEL, D_FF), D_MODEL ** -0.5),
        "dense_w_down": nrm(ks[17], (N_DENSE, D_FF, D_MODEL), D_FF ** -0.5),
        "moe_router": nrm(ks[18], (N_MOE, D_MODEL, N_EXPERTS), D_MODEL ** -0.5),
        "moe_w_gate": nrm(ks[19], (N_MOE, N_EXPERTS, D_MODEL, D_FF), D_MODEL ** -0.5),
        "moe_w_up": nrm(ks[20], (N_MOE, N_EXPERTS, D_MODEL, D_FF), D_MODEL ** -0.5),
        "moe_w_down": nrm(ks[21], (N_MOE, N_EXPERTS, D_FF, D_MODEL), D_FF ** -0.5),
    }


def reference(x_prompt, x_sample, attn_norm, w_in, w_out, diff_q_norm, diff_k_norm,
              lambda_q, lambda_k, diff_sub_norm, na_q_norm, na_k_norm, na_rpb, na_out_norm,
              ffn_norm, dense_w_gate, dense_w_up, dense_w_down,
              moe_router, moe_w_gate, moe_w_up, moe_w_down):
    y_prompt = run_trunk(x_prompt, attn_norm, w_in, w_out, diff_q_norm, diff_k_norm,
                         lambda_q, lambda_k, diff_sub_norm, na_q_norm, na_k_norm, na_rpb,
                         na_out_norm, ffn_norm, dense_w_gate, dense_w_up, dense_w_down,
                         moe_router, moe_w_gate, moe_w_up, moe_w_down)
    y_sample = run_trunk(x_sample, attn_norm, w_in, w_out, diff_q_norm, diff_k_norm,
                         lambda_q, lambda_k, diff_sub_norm, na_q_norm, na_k_norm, na_rpb,
                         na_out_norm, ffn_norm, dense_w_gate, dense_w_up, dense_w_down,
                         moe_router, moe_w_gate, moe_w_up, moe_w_down)
    return (y_prompt, y_sample)
```

```python
import functools
import math

import jax
import jax.numpy as jnp
import numpy as np
from jax import lax
from jax.experimental import pallas as pl
from jax.experimental.pallas import tpu as pltpu

D_MODEL = 1024
DEPTH = 4
GRID_W = 64
DIFF_HEADS = 4
DIFF_DH = 64
NA_HEADS = 8
NA_DH = 64
NA_KH = 8
NA_KW = 16
HALF_WIDTH = 512
IN_WIDTH = 6 * HALF_WIDTH
D_FF = 2816
N_EXPERTS = 8
ROPE_THETA = 10000.0
EPS = 1e-6
MASK_VALUE = -1e30

LANES = 128
VMEM_LIMIT = 56 * 1024 * 1024

BF16 = jnp.bfloat16
F32 = jnp.float32


def _cparams(sem):
    return pltpu.CompilerParams(dimension_semantics=sem, vmem_limit_bytes=VMEM_LIMIT)


def _dot(a, b):
    return jnp.dot(a, b, preferred_element_type=F32)


def _dot_nt(a, b):
    return lax.dot_general(a, b, (((1,), (1,)), ((), ())), preferred_element_type=F32)


def _rms(x, g):
    return x * lax.rsqrt(jnp.mean(x * x, axis=-1, keepdims=True) + EPS) * g


def _group_norm64(y, gain, bd_ref):
    parts = []
    for c in range(HALF_WIDTH // 256):
        yc = y[:, c * 256:(c + 1) * 256]
        ss = _dot((yc * yc).astype(BF16), bd_ref[...])
        parts.append(yc * lax.rsqrt(ss * (1.0 / 64) + EPS))
    return jnp.concatenate(parts, axis=1) * gain


def _rope64(y, cos, sin_signed):
    n = y.shape[1]
    lane = lax.broadcasted_iota(jnp.int32, y.shape, 1)
    first_half = (lane % 64) < 32
    rot = jnp.where(first_half, pltpu.roll(y, n - 32, 1), pltpu.roll(y, 32, 1))
    return y * cos + rot * sin_signed


def _in_proj_kernel(x_ref, g_ref, w_ref, gains_ref, cos_ref, sin_ref, bd_ref, o_ref):
    x = x_ref[...]
    h = _rms(x, g_ref[...]).astype(BF16)
    cos = jnp.concatenate([cos_ref[...]] * (HALF_WIDTH // LANES), axis=1)
    sin = jnp.concatenate([sin_ref[...]] * (HALF_WIDTH // LANES), axis=1)
    for sec in range(6):
        y = _dot(h, w_ref[:, sec * HALF_WIDTH:(sec + 1) * HALF_WIDTH])
        if sec in (0, 1, 3, 4):
            y = _group_norm64(y, gains_ref[sec:sec + 1, :], bd_ref)
        if sec in (0, 1):
            y = _rope64(y, cos, sin)
        o_ref[:, sec * HALF_WIDTH:(sec + 1) * HALF_WIDTH] = y.astype(BF16)


def _in_proj(x, g, w, gains, cos, sin, bd, seq, tm=512):
    t = x.shape[0]
    tm = min(tm, seq)
    nseq = seq // tm
    return pl.pallas_call(
        _in_proj_kernel,
        grid=(t // tm,),
        in_specs=[
            pl.BlockSpec((tm, D_MODEL), lambda i: (i, 0)),
            pl.BlockSpec((1, D_MODEL), lambda i: (0, 0)),
            pl.BlockSpec((D_MODEL, IN_WIDTH), lambda i: (0, 0)),
            pl.BlockSpec((8, HALF_WIDTH), lambda i: (0, 0)),
            pl.BlockSpec((tm, LANES), lambda i: (i % nseq, 0)),
            pl.BlockSpec((tm, LANES), lambda i: (i % nseq, 0)),
            pl.BlockSpec((256, 256), lambda i: (0, 0)),
        ],
        out_specs=pl.BlockSpec((tm, IN_WIDTH), lambda i: (i, 0)),
        out_shape=jax.ShapeDtypeStruct((t, IN_WIDTH), BF16),
        compiler_params=_cparams(("parallel",)),
        name="in_proj",
    )(x, g, w, gains, cos, sin, bd)


def _diff_attn_kernel(lam_ref, q_ref, k_ref, v_ref, g_ref, o_ref, acc0_ref, acc1_ref, *, tk, out_scale):
    q = q_ref[0]
    tq = q.shape[0]
    lane = lax.broadcasted_iota(jnp.int32, q.shape, 1)
    q0 = jnp.where(lane < DIFF_DH, q, jnp.zeros_like(q))
    q1 = jnp.where(lane >= DIFF_DH, q, jnp.zeros_like(q))
    acc0_ref[...] = jnp.zeros_like(acc0_ref)
    acc1_ref[...] = jnp.zeros_like(acc1_ref)
    nk = k_ref.shape[1] // tk

    def online(s, m, l, acc_ref, v):
        m_new = jnp.maximum(m, jnp.max(s, axis=-1, keepdims=True))
        alpha = jnp.exp(m - m_new)
        p = jnp.exp(s - m_new)
        l_new = alpha * l + jnp.sum(p, axis=-1, keepdims=True)
        acc_ref[...] = alpha * acc_ref[...] + _dot(p.astype(BF16), v)
        return m_new, l_new

    def body(j, carry):
        m0, l0, m1, l1 = carry
        start = pl.multiple_of(j * tk, tk)
        k = k_ref[0, pl.ds(start, tk), :]
        v = v_ref[0, pl.ds(start, tk), :]
        m0, l0 = online(_dot_nt(q0, k), m0, l0, acc0_ref, v)
        m1, l1 = online(_dot_nt(q1, k), m1, l1, acc1_ref, v)
        return m0, l0, m1, l1

    neg = jnp.full((tq, 1), -jnp.inf, F32)
    zero = jnp.zeros((tq, 1), F32)
    _, l0, _, l1 = lax.fori_loop(0, nk, body, (neg, zero, neg, zero))
    lam = lam_ref[0]
    o = acc0_ref[...] / l0 - lam * (acc1_ref[...] / l1)
    o = _rms(o, g_ref[...]) * out_scale
    o_ref[0] = o.astype(BF16)


def _diff_attn(proj, lam, g_sub, out_scale, tq=512, tk=512):
    b, s, _ = proj.shape
    tq, tk = min(tq, s), min(tk, s)
    kern = functools.partial(_diff_attn_kernel, tk=tk, out_scale=out_scale)
    return pl.pallas_call(
        kern,
        grid=(b, DIFF_HEADS, s // tq),
        in_specs=[
            pl.BlockSpec(memory_space=pltpu.SMEM),
            pl.BlockSpec((1, tq, LANES), lambda bi, h, i: (bi, i, h)),
            pl.BlockSpec((1, s, LANES), lambda bi, h, i: (bi, 0, DIFF_HEADS + h)),
            pl.BlockSpec((1, s, LANES), lambda bi, h, i: (bi, 0, 2 * DIFF_HEADS + h)),
            pl.BlockSpec((1, LANES), lambda bi, h, i: (0, 0)),
        ],
        out_specs=pl.BlockSpec((1, tq, LANES), lambda bi, h, i: (bi, i, h)),
        out_shape=jax.ShapeDtypeStruct((b, s, HALF_WIDTH), BF16),
        scratch_shapes=[pltpu.VMEM((tq, LANES), F32), pltpu.VMEM((tq, LANES), F32)],
        compiler_params=_cparams(("parallel", "parallel", "arbitrary")),
        name="diff_attn",
    )(lam, proj, proj, proj, g_sub)


NA_ROWS_PER_STEP = 8
NA_BLOCK = NA_ROWS_PER_STEP * GRID_W


def _na_attn_kernel(q_ref, kp_ref, kc_ref, kn_ref, vp_ref, vc_ref, vn_ref, bias_ref, g_ref, o_ref,
                    kbuf, vbuf, *, rows):
    i = pl.program_id(1)
    for n, (kr, vr) in enumerate(((kp_ref, vp_ref), (kc_ref, vc_ref), (kn_ref, vn_ref))):
        kbuf[n * NA_BLOCK:(n + 1) * NA_BLOCK, :] = kr[0]
        vbuf[n * NA_BLOCK:(n + 1) * NA_BLOCK, :] = vr[0]
    lane = lax.broadcasted_iota(jnp.int32, (GRID_W, LANES), 1)
    halves = (lane < NA_DH, lane >= NA_DH)
    lane_k = lax.broadcasted_iota(jnp.int32, (NA_KH * GRID_W, LANES), 1)
    halves_k = (lane_k < NA_DH, lane_k >= NA_DH)

    def row_body(qr, carry):
        r = i * NA_ROWS_PER_STEP + qr
        r0 = jnp.clip(r - NA_KH // 2, 0, rows - NA_KH)
        start = pl.multiple_of((r0 - (i - 1) * NA_ROWS_PER_STEP) * GRID_W, GRID_W)
        rp = r - r0
        qrow = pl.multiple_of(qr * GRID_W, GRID_W)
        for p in range(NA_HEADS // 2):
            cols = slice(p * LANES, (p + 1) * LANES)
            qp = q_ref[0, pl.ds(qrow, GRID_W), cols]
            kp = kbuf[pl.ds(start, NA_KH * GRID_W), cols]
            vp = vbuf[pl.ds(start, NA_KH * GRID_W), cols]
            o_pair = jnp.zeros((GRID_W, LANES), F32)
            for e in range(2):
                qm = jnp.where(halves[e], qp, jnp.zeros_like(qp))
                s = _dot_nt(qm, kp) + bias_ref[2 * p + e, rp]
                m = jnp.max(s, axis=-1, keepdims=True)
                pe = jnp.exp(s - m)
                l = jnp.sum(pe, axis=-1, keepdims=True)
                vm = jnp.where(halves_k[e], vp, jnp.zeros_like(vp))
                oe = _dot(pe.astype(BF16), vm) / l
                ss = jnp.sum(oe * oe, axis=-1, keepdims=True)
                o_pair = o_pair + oe * lax.rsqrt(ss * (1.0 / NA_DH) + EPS)
            o_ref[0, pl.ds(qrow, GRID_W), cols] = (o_pair * g_ref[...]).astype(BF16)
        return carry

    lax.fori_loop(0, NA_ROWS_PER_STEP, row_body, 0)


def _na_attn(proj, bias, g_out):
    b, s, _ = proj.shape
    rows = s // GRID_W
    nb = rows // NA_ROWS_PER_STEP
    qcol, kcol, vcol = 3, 4, 5
    blk = (1, NA_BLOCK, HALF_WIDTH)

    def prev(bi, i, c):
        return (bi, jnp.maximum(i - 1, 0), c)

    def nxt(bi, i, c):
        return (bi, jnp.minimum(i + 1, nb - 1), c)

    return pl.pallas_call(
        functools.partial(_na_attn_kernel, rows=rows),
        grid=(b, nb),
        in_specs=[
            pl.BlockSpec(blk, lambda bi, i: (bi, i, qcol)),
            pl.BlockSpec(blk, lambda bi, i: prev(bi, i, kcol)),
            pl.BlockSpec(blk, lambda bi, i: (bi, i, kcol)),
            pl.BlockSpec(blk, lambda bi, i: nxt(bi, i, kcol)),
            pl.BlockSpec(blk, lambda bi, i: prev(bi, i, vcol)),
            pl.BlockSpec(blk, lambda bi, i: (bi, i, vcol)),
            pl.BlockSpec(blk, lambda bi, i: nxt(bi, i, vcol)),
            pl.BlockSpec((NA_HEADS, NA_KH, GRID_W, NA_KH * GRID_W), lambda bi, i: (0, 0, 0, 0)),
            pl.BlockSpec((1, LANES), lambda bi, i: (0, 0)),
        ],
        out_specs=pl.BlockSpec(blk, lambda bi, i: (bi, i, 0)),
        out_shape=jax.ShapeDtypeStruct((b, s, HALF_WIDTH), BF16),
        scratch_shapes=[pltpu.VMEM((3 * NA_BLOCK, HALF_WIDTH), BF16),
                        pltpu.VMEM((3 * NA_BLOCK, HALF_WIDTH), BF16)],
        compiler_params=_cparams(("parallel", "arbitrary")),
        name="na_attn",
    )(proj, proj, proj, proj, proj, proj, proj, bias, g_out)


def _na_bias_table(rpb):
    qc = np.arange(GRID_W)
    kc = np.arange(GRID_W)
    win = np.clip(qc - NA_KW // 2, 0, GRID_W - NA_KW)
    valid = (kc[None, :] >= win[:, None]) & (kc[None, :] < win[:, None] + NA_KW)
    dc = np.clip(kc[None, :] - qc[:, None] + NA_KW - 1, 0, 2 * NA_KW - 2)
    rp = np.arange(NA_KH)
    j = np.arange(NA_KH)
    dr = j[None, :] - rp[:, None] + NA_KH - 1
    t = rpb.astype(F32)[:, dr][:, :, :, dc]
    t = jnp.where(jnp.asarray(valid)[None, None, None], t, MASK_VALUE)
    t = t.transpose(0, 1, 3, 2, 4)
    return t.reshape(NA_HEADS, NA_KH, GRID_W, NA_KH * GRID_W)


def _out_proj_kernel(x_ref, a_ref, b_ref, w_ref, g_ref, xo_ref, h_ref):
    x = x_ref[...] + _dot(a_ref[...], w_ref[:HALF_WIDTH, :]) + _dot(b_ref[...], w_ref[HALF_WIDTH:, :])
    xo_ref[...] = x
    h_ref[...] = _rms(x, g_ref[...]).astype(BF16)


def _out_proj_router_kernel(x_ref, a_ref, b_ref, w_ref, g_ref, wr_ref, xo_ref, h_ref, c_ref):
    x = x_ref[...] + _dot(a_ref[...], w_ref[:HALF_WIDTH, :]) + _dot(b_ref[...], w_ref[HALF_WIDTH:, :])
    xo_ref[...] = x
    h = _rms(x, g_ref[...])
    h_hi = h.astype(BF16)
    h_ref[...] = h_hi
    h_lo = (h - h_hi.astype(F32)).astype(BF16)
    wr = wr_ref[...]
    both = _dot(h_hi, wr)
    lane = lax.broadcasted_iota(jnp.int32, both.shape, 1).astype(F32)
    lo_part = pltpu.roll(both, LANES - N_EXPERTS, 1)
    logits = both + lo_part + _dot(h_lo, wr)
    logits = jnp.where(lane < N_EXPERTS, logits, -jnp.inf)
    m1 = jnp.max(logits, axis=-1, keepdims=True)
    i1 = jnp.min(jnp.where(logits == m1, lane, float(LANES)), axis=-1, keepdims=True)
    rest = jnp.where(lane == i1, -jnp.inf, logits)
    m2 = jnp.max(rest, axis=-1, keepdims=True)
    i2 = jnp.min(jnp.where(rest == m2, lane, float(LANES)), axis=-1, keepdims=True)
    e2 = jnp.exp(m2 - m1)
    g1 = 1.0 / (1.0 + e2)
    g2 = e2 * g1
    c_ref[...] = jnp.where(lane == i1, g1, jnp.where(lane == i2, g2, 0.0))


def _out_proj(x, a, b, w, g, wr=None, tm=512):
    t = x.shape[0]
    tm = min(tm, t)
    row = lambda i: (i, 0)
    const = lambda i: (0, 0)
    in_specs = [
        pl.BlockSpec((tm, D_MODEL), row),
        pl.BlockSpec((tm, HALF_WIDTH), row),
        pl.BlockSpec((tm, HALF_WIDTH), row),
        pl.BlockSpec((D_MODEL, D_MODEL), const),
        pl.BlockSpec((1, D_MODEL), const),
    ]
    out_specs = [pl.BlockSpec((tm, D_MODEL), row), pl.BlockSpec((tm, D_MODEL), row)]
    out_shape = [jax.ShapeDtypeStruct((t, D_MODEL), F32), jax.ShapeDtypeStruct((t, D_MODEL), BF16)]
    args = [x, a, b, w, g]
    kern = _out_proj_kernel
    if wr is not None:
        in_specs.append(pl.BlockSpec((D_MODEL, LANES), const))
        out_specs.append(pl.BlockSpec((tm, LANES), row))
        out_shape.append(jax.ShapeDtypeStruct((t, LANES), F32))
        args.append(wr)
        kern = _out_proj_router_kernel
    return pl.pallas_call(
        kern,
        grid=(t // tm,),
        in_specs=in_specs,
        out_specs=out_specs,
        out_shape=out_shape,
        compiler_params=_cparams(("parallel",)),
        name="out_proj",
    )(*args)


def _swiglu_kernel(x_ref, h_ref, wg_ref, wu_ref, wd_ref, o_ref, acc_ref):
    f = pl.program_id(1)

    @pl.when(f == 0)
    def _():
        acc_ref[...] = x_ref[...]

    h = h_ref[...]
    g = _dot(h, wg_ref[...])
    u = _dot(h, wu_ref[...])
    a = (g * jax.nn.sigmoid(g) * u).astype(BF16)
    acc_ref[...] += _dot(a, wd_ref[...])

    @pl.when(f == pl.num_programs(1) - 1)
    def _():
        o_ref[...] = acc_ref[...]


def _swiglu(x, h, wg, wu, wd, tm=1024, tf=1408):
    t = x.shape[0]
    tm = min(tm, t)
    return pl.pallas_call(
        _swiglu_kernel,
        grid=(t // tm, D_FF // tf),
        in_specs=[
            pl.BlockSpec((tm, D_MODEL), lambda i, f: (i, 0)),
            pl.BlockSpec((tm, D_MODEL), lambda i, f: (i, 0)),
            pl.BlockSpec((D_MODEL, tf), lambda i, f: (0, f)),
            pl.BlockSpec((D_MODEL, tf), lambda i, f: (0, f)),
            pl.BlockSpec((tf, D_MODEL), lambda i, f: (f, 0)),
        ],
        out_specs=pl.BlockSpec((tm, D_MODEL), lambda i, f: (i, 0)),
        out_shape=jax.ShapeDtypeStruct((t, D_MODEL), F32),
        scratch_shapes=[pltpu.VMEM((tm, D_MODEL), F32)],
        compiler_params=_cparams(("parallel", "arbitrary")),
        name="swiglu",
    )(x, h, wg, wu, wd)


def _moe_kernel(x_ref, h_ref, c_ref, wg_ref, wu_ref, wd_ref, o_ref, acc_ref):
    e = pl.program_id(1)
    f = pl.program_id(2)

    @pl.when((e == 0) & (f == 0))
    def _():
        acc_ref[...] = x_ref[...]

    c = c_ref[...]
    lane = lax.broadcasted_iota(jnp.int32, c.shape, 1)
    ce = jnp.sum(jnp.where(lane == e, c, 0.0), axis=-1, keepdims=True)
    h = h_ref[...]
    g = _dot(h, wg_ref[0])
    u = _dot(h, wu_ref[0])
    a = (g * jax.nn.sigmoid(g) * u * ce).astype(BF16)
    acc_ref[...] += _dot(a, wd_ref[0])

    @pl.when((e == pl.num_programs(1) - 1) & (f == pl.num_programs(2) - 1))
    def _():
        o_ref[...] = acc_ref[...]


def _moe(x, h, c, wg, wu, wd, tm=1024, tf=1408):
    t = x.shape[0]
    tm = min(tm, t)
    return pl.pallas_call(
        _moe_kernel,
        grid=(t // tm, N_EXPERTS, D_FF // tf),
        in_specs=[
            pl.BlockSpec((tm, D_MODEL), lambda i, e, f: (i, 0)),
            pl.BlockSpec((tm, D_MODEL), lambda i, e, f: (i, 0)),
            pl.BlockSpec((tm, LANES), lambda i, e, f: (i, 0)),
            pl.BlockSpec((1, D_MODEL, tf), lambda i, e, f: (e, 0, f)),
            pl.BlockSpec((1, D_MODEL, tf), lambda i, e, f: (e, 0, f)),
            pl.BlockSpec((1, tf, D_MODEL), lambda i, e, f: (e, f, 0)),
        ],
        out_specs=pl.BlockSpec((tm, D_MODEL), lambda i, e, f: (i, 0)),
        out_shape=jax.ShapeDtypeStruct((t, D_MODEL), F32),
        scratch_shapes=[pltpu.VMEM((tm, D_MODEL), F32)],
        compiler_params=_cparams(("parallel", "arbitrary", "arbitrary")),
        name="moe",
    )(x, h, c, wg, wu, wd)


def _rope_tables(seq):
    inv = 1.0 / (ROPE_THETA ** (jnp.arange(0, DIFF_DH, 2, dtype=F32) / DIFF_DH))
    ang = jnp.arange(seq, dtype=F32)[:, None] * inv[None, :]
    ang = jnp.concatenate([ang, ang], axis=-1)
    cos, sin = jnp.cos(ang), jnp.sin(ang)
    sign = jnp.where(jnp.arange(DIFF_DH) < DIFF_DH // 2, -1.0, 1.0).astype(F32)
    return jnp.tile(cos, (1, 2)), jnp.tile(sin * sign, (1, 2))


def _block_diag_ones():
    idx = np.arange(256) // 64
    return jnp.asarray(idx[:, None] == idx[None, :], dtype=BF16)


def _prepare_layer(l, p):
    lam_init = 0.8 - 0.6 * math.exp(-0.3 * l)
    lq = p["lambda_q"][l].astype(F32)
    lk = p["lambda_k"][l].astype(F32)
    lam = jnp.exp(jnp.sum(lq[0] * lk[0])) - jnp.exp(jnp.sum(lq[1] * lk[1])) + lam_init
    scale = DIFF_DH ** -0.5
    tile8 = lambda g: jnp.tile(g.astype(F32), HALF_WIDTH // g.shape[0])
    zeros = jnp.zeros((HALF_WIDTH,), F32)
    gains = jnp.stack([
        tile8(p["diff_q_norm"][l]) * scale, tile8(p["diff_k_norm"][l]), zeros,
        tile8(p["na_q_norm"][l]) * scale, tile8(p["na_k_norm"][l]), zeros, zeros, zeros])
    lay = dict(
        lam=jnp.reshape(lam, (1,)).astype(F32),
        out_scale=1.0 - lam_init,
        attn_norm=p["attn_norm"][l].reshape(1, D_MODEL).astype(F32),
        w_in=p["w_in"][l].astype(BF16),
        gains=gains,
        g_sub=p["diff_sub_norm"][l].reshape(1, LANES).astype(F32),
        na_bias=_na_bias_table(p["na_rpb"][l]),
        g_na_out=jnp.tile(p["na_out_norm"][l].astype(F32), 2).reshape(1, LANES),
        w_out=p["w_out"][l].astype(BF16),
        ffn_norm=p["ffn_norm"][l].reshape(1, D_MODEL).astype(F32),
    )
    i = l // 2
    if l % 2 == 0:
        lay.update(wg=p["dense_w_gate"][i].astype(BF16), wu=p["dense_w_up"][i].astype(BF16),
                   wd=p["dense_w_down"][i].astype(BF16))
    else:
        wr = p["moe_router"][i].astype(F32)
        wr_hi = wr.astype(BF16)
        wr_lo = (wr - wr_hi.astype(F32)).astype(BF16)
        wr2 = jnp.zeros((D_MODEL, LANES), BF16)
        wr2 = wr2.at[:, :N_EXPERTS].set(wr_hi).at[:, N_EXPERTS:2 * N_EXPERTS].set(wr_lo)
        lay.update(wr=wr2, wg=p["moe_w_gate"][i].astype(BF16), wu=p["moe_w_up"][i].astype(BF16),
                   wd=p["moe_w_down"][i].astype(BF16))
    return lay


def _run_trunk(x, layers, bd):
    b, s, _ = x.shape
    t = b * s
    cos, sin = _rope_tables(s)
    x = x.reshape(t, D_MODEL)
    for l, lay in enumerate(layers):
        proj = _in_proj(x, lay["attn_norm"], lay["w_in"], lay["gains"], cos, sin, bd, s)
        proj3 = proj.reshape(b, s, IN_WIDTH)
        a_out = _diff_attn(proj3, lay["lam"], lay["g_sub"], lay["out_scale"])
        b_out = _na_attn(proj3, lay["na_bias"], lay["g_na_out"])
        a_out = a_out.reshape(t, HALF_WIDTH)
        b_out = b_out.reshape(t, HALF_WIDTH)
        if l % 2 == 0:
            x, h = _out_proj(x, a_out, b_out, lay["w_out"], lay["ffn_norm"])
            x = _swiglu(x, h, lay["wg"], lay["wu"], lay["wd"])
        else:
            x, h, c = _out_proj(x, a_out, b_out, lay["w_out"], lay["ffn_norm"], lay["wr"])
            x = _moe(x, h, c, lay["wg"], lay["wu"], lay["wd"])
    return x.reshape(b, s, D_MODEL)


def kernel(x_prompt, x_sample, attn_norm, w_in, w_out, diff_q_norm, diff_k_norm, lambda_q, lambda_k, diff_sub_norm, na_q_norm, na_k_norm, na_rpb, na_out_norm, ffn_norm, dense_w_gate, dense_w_up, dense_w_down, moe_router, moe_w_gate, moe_w_up, moe_w_down):
    p = dict(attn_norm=attn_norm, w_in=w_in, w_out=w_out, diff_q_norm=diff_q_norm, diff_k_norm=diff_k_norm,
             lambda_q=lambda_q, lambda_k=lambda_k, diff_sub_norm=diff_sub_norm, na_q_norm=na_q_norm,
             na_k_norm=na_k_norm, na_rpb=na_rpb, na_out_norm=na_out_norm, ffn_norm=ffn_norm,
             dense_w_gate=dense_w_gate, dense_w_up=dense_w_up, dense_w_down=dense_w_down,
             moe_router=moe_router, moe_w_gate=moe_w_gate, moe_w_up=moe_w_up, moe_w_down=moe_w_down)
    layers = [_prepare_layer(l, p) for l in range(DEPTH)]
    bd = _block_diag_ones()
    return (_run_trunk(x_prompt, layers, bd), _run_trunk(x_sample, layers, bd))
```

```python
import functools
import math

import jax
import jax.numpy as jnp
import numpy as np
from jax import lax
from jax.experimental import pallas as pl
from jax.experimental.pallas import tpu as pltpu

D_MODEL = 1024
DEPTH = 4
GRID_W = 64
DIFF_HEADS = 4
DIFF_DH = 64
NA_HEADS = 8
NA_DH = 64
NA_KH = 8
NA_KW = 16
HALF_WIDTH = 512
IN_WIDTH = 6 * HALF_WIDTH
D_FF = 2816
N_EXPERTS = 8
ROPE_THETA = 10000.0
EPS = 1e-6
MASK_VALUE = -1e30
SCORE_BOUND_MARGIN = 1.02
MAX_SAFE_SCORE_BOUND = 60.0

LANES = 128
VMEM_LIMIT = 56 * 1024 * 1024

BF16 = jnp.bfloat16
F32 = jnp.float32


def _cparams(sem):
    return pltpu.CompilerParams(dimension_semantics=sem, vmem_limit_bytes=VMEM_LIMIT)


def _dot(a, b):
    return jnp.dot(a, b, preferred_element_type=F32)


def _dot_nt(a, b):
    return lax.dot_general(a, b, (((1,), (1,)), ((), ())), preferred_element_type=F32)


def _rms(x, g):
    return x * lax.rsqrt(jnp.mean(x * x, axis=-1, keepdims=True) + EPS) * g


def _group_norm64(y, gain, bd_ref):
    parts = []
    for c in range(HALF_WIDTH // 256):
        yc = y[:, c * 256:(c + 1) * 256]
        ss = _dot((yc * yc).astype(BF16), bd_ref[...])
        parts.append(yc * lax.rsqrt(ss * (1.0 / 64) + EPS))
    return jnp.concatenate(parts, axis=1) * gain


def _rope64(y, cos, sin_signed):
    n = y.shape[1]
    lane = lax.broadcasted_iota(jnp.int32, y.shape, 1)
    first_half = (lane % 64) < 32
    rot = jnp.where(first_half, pltpu.roll(y, n - 32, 1), pltpu.roll(y, 32, 1))
    return y * cos + rot * sin_signed


def _in_proj_kernel(x_ref, g_ref, w_ref, gains_ref, cos_ref, sin_ref, bd_ref, o_ref):
    x = x_ref[...]
    h = _rms(x, g_ref[...]).astype(BF16)
    cos = jnp.concatenate([cos_ref[...]] * (HALF_WIDTH // LANES), axis=1)
    sin = jnp.concatenate([sin_ref[...]] * (HALF_WIDTH // LANES), axis=1)
    for sec in range(6):
        y = _dot(h, w_ref[:, sec * HALF_WIDTH:(sec + 1) * HALF_WIDTH])
        if sec in (0, 1, 3, 4):
            y = _group_norm64(y, gains_ref[sec:sec + 1, :], bd_ref)
        if sec in (0, 1):
            y = _rope64(y, cos, sin)
        o_ref[:, sec * HALF_WIDTH:(sec + 1) * HALF_WIDTH] = y.astype(BF16)


def _in_proj(x, g, w, gains, cos, sin, bd, seq, tm=512):
    t = x.shape[0]
    tm = min(tm, seq)
    nseq = seq // tm
    return pl.pallas_call(
        _in_proj_kernel,
        grid=(t // tm,),
        in_specs=[
            pl.BlockSpec((tm, D_MODEL), lambda i: (i, 0)),
            pl.BlockSpec((1, D_MODEL), lambda i: (0, 0)),
            pl.BlockSpec((D_MODEL, IN_WIDTH), lambda i: (0, 0)),
            pl.BlockSpec((8, HALF_WIDTH), lambda i: (0, 0)),
            pl.BlockSpec((tm, LANES), lambda i: (i % nseq, 0)),
            pl.BlockSpec((tm, LANES), lambda i: (i % nseq, 0)),
            pl.BlockSpec((256, 256), lambda i: (0, 0)),
        ],
        out_specs=pl.BlockSpec((tm, IN_WIDTH), lambda i: (i, 0)),
        out_shape=jax.ShapeDtypeStruct((t, IN_WIDTH), BF16),
        compiler_params=_cparams(("parallel",)),
        name="in_proj",
    )(x, g, w, gains, cos, sin, bd)


def _diff_attn_kernel(scal_ref, q_ref, k_ref, v_ref, g_ref, o_ref, acc0_ref, acc1_ref, l0_ref, l1_ref, *,
                      tk, out_scale, bounded):
    q = q_ref[0]
    tq = q.shape[0]
    lane = lax.broadcasted_iota(jnp.int32, q.shape, 1)
    qm = (jnp.where(lane < DIFF_DH, q, jnp.zeros_like(q)), jnp.where(lane >= DIFF_DH, q, jnp.zeros_like(q)))
    accs = (acc0_ref, acc1_ref)
    ls = (l0_ref, l1_ref)
    for r in accs + ls:
        r[...] = jnp.zeros_like(r)
    nk = k_ref.shape[1] // tk

    def lane_fold(p):
        out = p[:, :LANES]
        for c in range(1, tk // LANES):
            out = out + p[:, c * LANES:(c + 1) * LANES]
        return out

    def load(j):
        start = pl.multiple_of(j * tk, tk)
        return k_ref[0, pl.ds(start, tk), :], v_ref[0, pl.ds(start, tk), :]

    if bounded:
        bound = scal_ref[1]

        def body(j, carry):
            k, v = load(j)
            for a in range(2):
                p = jnp.exp2(_dot_nt(qm[a], k) - bound)
                ls[a][...] += lane_fold(p)
                accs[a][...] += _dot(p.astype(BF16), v)
            return carry

        lax.fori_loop(0, nk, body, 0)
    else:
        def body(j, ms):
            k, v = load(j)
            new_ms = []
            for a in range(2):
                s = _dot_nt(qm[a], k)
                m_new = jnp.maximum(ms[a], jnp.max(s, axis=-1, keepdims=True))
                alpha = jnp.exp2(ms[a] - m_new)
                p = jnp.exp2(s - m_new)
                ls[a][...] = alpha * ls[a][...] + lane_fold(p)
                accs[a][...] = alpha * accs[a][...] + _dot(p.astype(BF16), v)
                new_ms.append(m_new)
            return tuple(new_ms)

        neg = jnp.full((tq, 1), -jnp.inf, F32)
        lax.fori_loop(0, nk, body, (neg, neg))

    lam = scal_ref[0]
    l0 = jnp.sum(l0_ref[...], axis=-1, keepdims=True)
    l1 = jnp.sum(l1_ref[...], axis=-1, keepdims=True)
    o = acc0_ref[...] / l0 - lam * (acc1_ref[...] / l1)
    o = _rms(o, g_ref[...]) * out_scale
    o_ref[0] = o.astype(BF16)


def _diff_attn(proj, scal, g_sub, out_scale):
    return lax.cond(
        scal[1] <= MAX_SAFE_SCORE_BOUND,
        lambda: _diff_attn_call(proj, scal, g_sub, out_scale, True),
        lambda: _diff_attn_call(proj, scal, g_sub, out_scale, False))


def _diff_attn_call(proj, scal, g_sub, out_scale, bounded, tq=1024, tk=1024):
    b, s, _ = proj.shape
    tq, tk = min(tq, s), min(tk, s)
    kern = functools.partial(_diff_attn_kernel, tk=tk, out_scale=out_scale, bounded=bounded)
    return pl.pallas_call(
        kern,
        grid=(b, DIFF_HEADS, s // tq),
        in_specs=[
            pl.BlockSpec(memory_space=pltpu.SMEM),
            pl.BlockSpec((1, tq, LANES), lambda bi, h, i: (bi, i, h)),
            pl.BlockSpec((1, s, LANES), lambda bi, h, i: (bi, 0, DIFF_HEADS + h)),
            pl.BlockSpec((1, s, LANES), lambda bi, h, i: (bi, 0, 2 * DIFF_HEADS + h)),
            pl.BlockSpec((1, LANES), lambda bi, h, i: (0, 0)),
        ],
        out_specs=pl.BlockSpec((1, tq, LANES), lambda bi, h, i: (bi, i, h)),
        out_shape=jax.ShapeDtypeStruct((b, s, HALF_WIDTH), BF16),
        scratch_shapes=[pltpu.VMEM((tq, LANES), F32)] * 4,
        compiler_params=_cparams(("parallel", "parallel", "arbitrary")),
        name="diff_attn_bounded" if bounded else "diff_attn_online",
    )(scal, proj, proj, proj, g_sub)


NA_ROWS_PER_STEP = 8
NA_BLOCK = NA_ROWS_PER_STEP * GRID_W


def _na_attn_kernel(q_ref, kp_ref, kc_ref, kn_ref, vp_ref, vc_ref, vn_ref, bias_ref, g_ref, o_ref,
                    kbuf, vbuf, *, rows):
    i = pl.program_id(1)
    for n, (kr, vr) in enumerate(((kp_ref, vp_ref), (kc_ref, vc_ref), (kn_ref, vn_ref))):
        kbuf[n * NA_BLOCK:(n + 1) * NA_BLOCK, :] = kr[0]
        vbuf[n * NA_BLOCK:(n + 1) * NA_BLOCK, :] = vr[0]
    lane = lax.broadcasted_iota(jnp.int32, (GRID_W, LANES), 1)
    halves = (lane < NA_DH, lane >= NA_DH)
    lane_k = lax.broadcasted_iota(jnp.int32, (NA_KH * GRID_W, LANES), 1)
    halves_k = (lane_k < NA_DH, lane_k >= NA_DH)

    def row_body(qr, carry):
        r = i * NA_ROWS_PER_STEP + qr
        r0 = jnp.clip(r - NA_KH // 2, 0, rows - NA_KH)
        start = pl.multiple_of((r0 - (i - 1) * NA_ROWS_PER_STEP) * GRID_W, GRID_W)
        rp = r - r0
        qrow = pl.multiple_of(qr * GRID_W, GRID_W)
        npairs = NA_HEADS // 2
        col = [slice(p * LANES, (p + 1) * LANES) for p in range(npairs)]
        scores = []
        for p in range(npairs):
            qp = q_ref[0, pl.ds(qrow, GRID_W), col[p]]
            kp = kbuf[pl.ds(start, NA_KH * GRID_W), col[p]]
            q2 = jnp.concatenate([jnp.where(h, qp, jnp.zeros_like(qp)) for h in halves], axis=0)
            scores.append(_dot_nt(q2, kp) + bias_ref[p, rp])
        probs = []
        for s in scores:
            m = jnp.max(s, axis=-1, keepdims=True)
            pe = jnp.exp2(s - m)
            probs.append((pe.astype(BF16), jnp.sum(pe, axis=-1, keepdims=True)))
        for p in range(npairs):
            vp = vbuf[pl.ds(start, NA_KH * GRID_W), col[p]]
            v2 = jnp.concatenate([jnp.where(h, vp, jnp.zeros_like(vp)) for h in halves_k], axis=0)
            pe, l = probs[p]
            p2 = jnp.concatenate([pe[:GRID_W], pe[GRID_W:]], axis=1)
            o = _dot(p2, v2) * jnp.where(halves[0], 1.0 / l[:GRID_W], 1.0 / l[GRID_W:])
            o2 = o * o
            ss = [jnp.sum(jnp.where(h, o2, 0.0), axis=-1, keepdims=True) for h in halves]
            inv = jnp.where(halves[0], lax.rsqrt(ss[0] * (1.0 / NA_DH) + EPS),
                            lax.rsqrt(ss[1] * (1.0 / NA_DH) + EPS))
            o_ref[0, pl.ds(qrow, GRID_W), col[p]] = (o * inv * g_ref[...]).astype(BF16)
        return carry

    lax.fori_loop(0, NA_ROWS_PER_STEP, row_body, 0)


def _na_attn(proj, bias, g_out):
    b, s, _ = proj.shape
    rows = s // GRID_W
    nb = rows // NA_ROWS_PER_STEP
    qcol, kcol, vcol = 3, 4, 5
    blk = (1, NA_BLOCK, HALF_WIDTH)

    def prev(bi, i, c):
        return (bi, jnp.maximum(i - 1, 0), c)

    def nxt(bi, i, c):
        return (bi, jnp.minimum(i + 1, nb - 1), c)

    return pl.pallas_call(
        functools.partial(_na_attn_kernel, rows=rows),
        grid=(b, nb),
        in_specs=[
            pl.BlockSpec(blk, lambda bi, i: (bi, i, qcol)),
            pl.BlockSpec(blk, lambda bi, i: prev(bi, i, kcol)),
            pl.BlockSpec(blk, lambda bi, i: (bi, i, kcol)),
            pl.BlockSpec(blk, lambda bi, i: nxt(bi, i, kcol)),
            pl.BlockSpec(blk, lambda bi, i: prev(bi, i, vcol)),
            pl.BlockSpec(blk, lambda bi, i: (bi, i, vcol)),
            pl.BlockSpec(blk, lambda bi, i: nxt(bi, i, vcol)),
            pl.BlockSpec((NA_HEADS // 2, NA_KH, 2 * GRID_W, NA_KH * GRID_W), lambda bi, i: (0, 0, 0, 0)),
            pl.BlockSpec((1, LANES), lambda bi, i: (0, 0)),
        ],
        out_specs=pl.BlockSpec(blk, lambda bi, i: (bi, i, 0)),
        out_shape=jax.ShapeDtypeStruct((b, s, HALF_WIDTH), BF16),
        scratch_shapes=[pltpu.VMEM((3 * NA_BLOCK, HALF_WIDTH), BF16),
                        pltpu.VMEM((3 * NA_BLOCK, HALF_WIDTH), BF16)],
        compiler_params=_cparams(("parallel", "arbitrary")),
        name="na_attn",
    )(proj, proj, proj, proj, proj, proj, proj, bias, g_out)


def _na_bias_table(rpb):
    qc = np.arange(GRID_W)
    kc = np.arange(GRID_W)
    win = np.clip(qc - NA_KW // 2, 0, GRID_W - NA_KW)
    valid = (kc[None, :] >= win[:, None]) & (kc[None, :] < win[:, None] + NA_KW)
    dc = np.clip(kc[None, :] - qc[:, None] + NA_KW - 1, 0, 2 * NA_KW - 2)
    rp = np.arange(NA_KH)
    j = np.arange(NA_KH)
    dr = j[None, :] - rp[:, None] + NA_KH - 1
    t = (rpb.astype(F32) * math.log2(math.e))[:, dr][:, :, :, dc]
    t = jnp.where(jnp.asarray(valid)[None, None, None], t, MASK_VALUE)
    t = t.reshape(NA_HEADS // 2, 2, NA_KH, NA_KH, GRID_W, GRID_W)
    t = t.transpose(0, 2, 1, 4, 3, 5)
    return t.reshape(NA_HEADS // 2, NA_KH, 2 * GRID_W, NA_KH * GRID_W)


def _out_proj_kernel(x_ref, a_ref, b_ref, w_ref, g_ref, xo_ref, h_ref):
    x = x_ref[...] + _dot(a_ref[...], w_ref[:HALF_WIDTH, :]) + _dot(b_ref[...], w_ref[HALF_WIDTH:, :])
    xo_ref[...] = x
    h_ref[...] = _rms(x, g_ref[...]).astype(BF16)


def _out_proj_router_kernel(x_ref, a_ref, b_ref, w_ref, g_ref, wr_ref, xo_ref, h_ref, c_ref):
    x = x_ref[...] + _dot(a_ref[...], w_ref[:HALF_WIDTH, :]) + _dot(b_ref[...], w_ref[HALF_WIDTH:, :])
    xo_ref[...] = x
    h = _rms(x, g_ref[...])
    h_hi = h.astype(BF16)
    h_ref[...] = h_hi
    h_lo = (h - h_hi.astype(F32)).astype(BF16)
    wr = wr_ref[...]
    both = _dot(h_hi, wr)
    lane = lax.broadcasted_iota(jnp.int32, both.shape, 1).astype(F32)
    lo_part = pltpu.roll(both, LANES - N_EXPERTS, 1)
    logits = both + lo_part + _dot(h_lo, wr)
    logits = jnp.where(lane < N_EXPERTS, logits, -jnp.inf)
    m1 = jnp.max(logits, axis=-1, keepdims=True)
    i1 = jnp.min(jnp.where(logits == m1, lane, float(LANES)), axis=-1, keepdims=True)
    rest = jnp.where(lane == i1, -jnp.inf, logits)
    m2 = jnp.max(rest, axis=-1, keepdims=True)
    i2 = jnp.min(jnp.where(rest == m2, lane, float(LANES)), axis=-1, keepdims=True)
    e2 = jnp.exp(m2 - m1)
    g1 = 1.0 / (1.0 + e2)
    g2 = e2 * g1
    c_ref[...] = jnp.where(lane == i1, g1, jnp.where(lane == i2, g2, 0.0))


def _out_proj(x, a, b, w, g, wr=None, tm=512):
    t = x.shape[0]
    tm = min(tm, t)
    row = lambda i: (i, 0)
    const = lambda i: (0, 0)
    in_specs = [
        pl.BlockSpec((tm, D_MODEL), row),
        pl.BlockSpec((tm, HALF_WIDTH), row),
        pl.BlockSpec((tm, HALF_WIDTH), row),
        pl.BlockSpec((D_MODEL, D_MODEL), const),
        pl.BlockSpec((1, D_MODEL), const),
    ]
    out_specs = [pl.BlockSpec((tm, D_MODEL), row), pl.BlockSpec((tm, D_MODEL), row)]
    out_shape = [jax.ShapeDtypeStruct((t, D_MODEL), F32), jax.ShapeDtypeStruct((t, D_MODEL), BF16)]
    args = [x, a, b, w, g]
    kern = _out_proj_kernel
    if wr is not None:
        in_specs.append(pl.BlockSpec((D_MODEL, LANES), const))
        out_specs.append(pl.BlockSpec((tm, LANES), row))
        out_shape.append(jax.ShapeDtypeStruct((t, LANES), F32))
        args.append(wr)
        kern = _out_proj_router_kernel
    return pl.pallas_call(
        kern,
        grid=(t // tm,),
        in_specs=in_specs,
        out_specs=out_specs,
        out_shape=out_shape,
        compiler_params=_cparams(("parallel",)),
        name="out_proj",
    )(*args)


def _swiglu_kernel(x_ref, h_ref, wg_ref, wu_ref, wd_ref, o_ref, acc_ref):
    f = pl.program_id(1)

    @pl.when(f == 0)
    def _():
        acc_ref[...] = x_ref[...]

    h = h_ref[...]
    g = _dot(h, wg_ref[...])
    u = _dot(h, wu_ref[...])
    a = (g * jax.nn.sigmoid(g) * u).astype(BF16)
    acc_ref[...] += _dot(a, wd_ref[...])

    @pl.when(f == pl.num_programs(1) - 1)
    def _():
        o_ref[...] = acc_ref[...]


def _swiglu(x, h, wg, wu, wd, tm=1024, tf=1408):
    t = x.shape[0]
    tm = min(tm, t)
    return pl.pallas_call(
        _swiglu_kernel,
        grid=(t // tm, D_FF // tf),
        in_specs=[
            pl.BlockSpec((tm, D_MODEL), lambda i, f: (i, 0)),
            pl.BlockSpec((tm, D_MODEL), lambda i, f: (i, 0)),
            pl.BlockSpec((D_MODEL, tf), lambda i, f: (0, f)),
            pl.BlockSpec((D_MODEL, tf), lambda i, f: (0, f)),
            pl.BlockSpec((tf, D_MODEL), lambda i, f: (f, 0)),
        ],
        out_specs=pl.BlockSpec((tm, D_MODEL), lambda i, f: (i, 0)),
        out_shape=jax.ShapeDtypeStruct((t, D_MODEL), F32),
        scratch_shapes=[pltpu.VMEM((tm, D_MODEL), F32)],
        compiler_params=_cparams(("parallel", "arbitrary")),
        name="swiglu",
    )(x, h, wg, wu, wd)


def _moe_kernel(x_ref, h_ref, c_ref, wg_ref, wu_ref, wd_ref, o_ref, acc_ref):
    e = pl.program_id(1)
    f = pl.program_id(2)

    @pl.when((e == 0) & (f == 0))
    def _():
        acc_ref[...] = x_ref[...]

    c = c_ref[...]
    lane = lax.broadcasted_iota(jnp.int32, c.shape, 1)
    ce = jnp.sum(jnp.where(lane == e, c, 0.0), axis=-1, keepdims=True)
    h = h_ref[...]
    g = _dot(h, wg_ref[0])
    u = _dot(h, wu_ref[0])
    a = (g * jax.nn.sigmoid(g) * u * ce).astype(BF16)
    acc_ref[...] += _dot(a, wd_ref[0])

    @pl.when((e == pl.num_programs(1) - 1) & (f == pl.num_programs(2) - 1))
    def _():
        o_ref[...] = acc_ref[...]


def _moe(x, h, c, wg, wu, wd, tm=1024, tf=1408):
    t = x.shape[0]
    tm = min(tm, t)
    return pl.pallas_call(
        _moe_kernel,
        grid=(t // tm, N_EXPERTS, D_FF // tf),
        in_specs=[
            pl.BlockSpec((tm, D_MODEL), lambda i, e, f: (i, 0)),
            pl.BlockSpec((tm, D_MODEL), lambda i, e, f: (i, 0)),
            pl.BlockSpec((tm, LANES), lambda i, e, f: (i, 0)),
            pl.BlockSpec((1, D_MODEL, tf), lambda i, e, f: (e, 0, f)),
            pl.BlockSpec((1, D_MODEL, tf), lambda i, e, f: (e, 0, f)),
            pl.BlockSpec((1, tf, D_MODEL), lambda i, e, f: (e, f, 0)),
        ],
        out_specs=pl.BlockSpec((tm, D_MODEL), lambda i, e, f: (i, 0)),
        out_shape=jax.ShapeDtypeStruct((t, D_MODEL), F32),
        scratch_shapes=[pltpu.VMEM((tm, D_MODEL), F32)],
        compiler_params=_cparams(("parallel", "arbitrary", "arbitrary")),
        name="moe",
    )(x, h, c, wg, wu, wd)


MOE_TILE = 1024
MOE_CAPACITY = 320
TRI_BLOCK = 512


def _route_kernel(c_ref, ltri_ref, utri_ref, posm_ref, posmt_ref):
    tm = c_ref.shape[0]
    carry = jnp.zeros((1, LANES), F32)
    carry_t = jnp.zeros((LANES, 1), F32)
    for b in range(tm // TRI_BLOCK):
        rows = slice(b * TRI_BLOCK, (b + 1) * TRI_BLOCK)
        sel = c_ref[rows, :] > 0.0
        self = jnp.where(sel, 1.0, 0.0)
        pos = _dot(ltri_ref[...], self.astype(BF16)) + carry
        posm_ref[rows, :] = jnp.where(sel, pos, -1.0)
        sel_t = self.T
        pos_t = _dot(sel_t.astype(BF16), utri_ref[...]) + carry_t
        posmt_ref[0, :, rows] = jnp.where(sel_t > 0.0, pos_t, -1.0)
        carry = carry + jnp.sum(self, axis=0, keepdims=True)
        carry_t = carry_t + jnp.sum(sel_t, axis=1, keepdims=True)


def _route(c, tm):
    t = c.shape[0]
    idx = np.arange(TRI_BLOCK)
    ltri = jnp.asarray(idx[None, :] < idx[:, None], dtype=BF16)
    utri = jnp.asarray(idx[:, None] < idx[None, :], dtype=BF16)
    return pl.pallas_call(
        _route_kernel,
        grid=(t // tm,),
        in_specs=[
            pl.BlockSpec((tm, LANES), lambda i: (i, 0)),
            pl.BlockSpec((TRI_BLOCK, TRI_BLOCK), lambda i: (0, 0)),
            pl.BlockSpec((TRI_BLOCK, TRI_BLOCK), lambda i: (0, 0)),
        ],
        out_specs=[pl.BlockSpec((tm, LANES), lambda i: (i, 0)),
                   pl.BlockSpec((1, LANES, tm), lambda i: (i, 0, 0))],
        out_shape=[jax.ShapeDtypeStruct((t, LANES), F32),
                   jax.ShapeDtypeStruct((t // tm, LANES, tm), F32)],
        compiler_params=_cparams(("parallel",)),
        name="moe_route",
    )(c, ltri, utri)


def _moe_sparse_kernel(x_ref, h_ref, c_ref, posm_ref, posmt_ref, wg_ref, wu_ref, wd_ref, o_ref, hc_ref, y_ref):
    e = pl.program_id(1)
    f = pl.program_id(2)
    last_f = pl.num_programs(2) - 1
    tm = x_ref.shape[0]
    cap = hc_ref.shape[0]

    @pl.when((e == 0) & (f == 0))
    def _():
        o_ref[...] = x_ref[...]

    @pl.when(f == 0)
    def _():
        slot = lax.broadcasted_iota(jnp.int32, (cap, tm), 0).astype(F32)
        gather = jnp.where(posmt_ref[0, pl.ds(e, 1), :] == slot, 1.0, 0.0).astype(BF16)
        hc_ref[...] = _dot(gather, h_ref[...]).astype(BF16)

    hk = hc_ref[...]
    g = _dot(hk, wg_ref[0])
    u = _dot(hk, wu_ref[0])
    y = _dot((g * jax.nn.sigmoid(g) * u).astype(BF16), wd_ref[0])

    @pl.when(f == 0)
    def _():
        y_ref[...] = y

    @pl.when(f > 0)
    def _():
        y_ref[...] += y

    @pl.when(f == last_f)
    def _():
        lane = lax.broadcasted_iota(jnp.int32, (tm, LANES), 1)
        pos = jnp.sum(jnp.where(lane == e, posm_ref[...], 0.0), axis=-1, keepdims=True)
        gate = jnp.sum(jnp.where(lane == e, c_ref[...], 0.0), axis=-1, keepdims=True)
        slot = lax.broadcasted_iota(jnp.int32, (tm, cap), 1).astype(F32)
        scatter = jnp.where(pos == slot, 1.0, 0.0).astype(BF16)
        yb = y_ref[...].astype(BF16)
        for n in range(D_MODEL // 256):
            cols = slice(n * 256, (n + 1) * 256)
            o_ref[:, cols] += gate * _dot(scatter, yb[:, cols])


def _moe_sparse(x, h, c, wg, wu, wd, tm=MOE_TILE, tf=1408):
    t = x.shape[0]
    tm = min(tm, t)
    posm, posmt = _route(c, tm)
    out = pl.pallas_call(
        _moe_sparse_kernel,
        grid=(t // tm, N_EXPERTS, D_FF // tf),
        in_specs=[
            pl.BlockSpec((tm, D_MODEL), lambda i, e, f: (i, 0)),
            pl.BlockSpec((tm, D_MODEL), lambda i, e, f: (i, 0)),
            pl.BlockSpec((tm, LANES), lambda i, e, f: (i, 0)),
            pl.BlockSpec((tm, LANES), lambda i, e, f: (i, 0)),
            pl.BlockSpec((1, LANES, tm), lambda i, e, f: (i, 0, 0)),
            pl.BlockSpec((1, D_MODEL, tf), lambda i, e, f: (e, 0, f)),
            pl.BlockSpec((1, D_MODEL, tf), lambda i, e, f: (e, 0, f)),
            pl.BlockSpec((1, tf, D_MODEL), lambda i, e, f: (e, f, 0)),
        ],
        out_specs=pl.BlockSpec((tm, D_MODEL), lambda i, e, f: (i, 0)),
        out_shape=jax.ShapeDtypeStruct((t, D_MODEL), F32),
        scratch_shapes=[pltpu.VMEM((MOE_CAPACITY, D_MODEL), BF16), pltpu.VMEM((MOE_CAPACITY, D_MODEL), F32)],
        compiler_params=_cparams(("parallel", "arbitrary", "arbitrary")),
        name="moe_sparse",
    )(x, h, c, posm, posmt, wg, wu, wd)
    overflow = posm >= MOE_CAPACITY
    return lax.cond(
        jnp.any(overflow),
        lambda: _moe(out, h, jnp.where(overflow, c, 0.0), wg, wu, wd),
        lambda: out)


def _rope_tables(seq):
    inv = 1.0 / (ROPE_THETA ** (jnp.arange(0, DIFF_DH, 2, dtype=F32) / DIFF_DH))
    ang = jnp.arange(seq, dtype=F32)[:, None] * inv[None, :]
    ang = jnp.concatenate([ang, ang], axis=-1)
    cos, sin = jnp.cos(ang), jnp.sin(ang)
    sign = jnp.where(jnp.arange(DIFF_DH) < DIFF_DH // 2, -1.0, 1.0).astype(F32)
    return jnp.tile(cos, (1, 2)), jnp.tile(sin * sign, (1, 2))


def _block_diag_ones():
    idx = np.arange(256) // 64
    return jnp.asarray(idx[:, None] == idx[None, :], dtype=BF16)


def _prepare_layer(l, p):
    lam_init = 0.8 - 0.6 * math.exp(-0.3 * l)
    lq = p["lambda_q"][l].astype(F32)
    lk = p["lambda_k"][l].astype(F32)
    lam = jnp.exp(jnp.sum(lq[0] * lk[0])) - jnp.exp(jnp.sum(lq[1] * lk[1])) + lam_init
    scale = DIFF_DH ** -0.5 * math.log2(math.e)
    tile8 = lambda g: jnp.tile(g.astype(F32), HALF_WIDTH // g.shape[0])
    zeros = jnp.zeros((HALF_WIDTH,), F32)
    dq_gain = p["diff_q_norm"][l].astype(F32) * scale
    dk_gain = p["diff_k_norm"][l].astype(F32)
    gains = jnp.stack([
        tile8(dq_gain), tile8(dk_gain), zeros,
        tile8(p["na_q_norm"][l]) * scale, tile8(p["na_k_norm"][l]), zeros, zeros, zeros])
    score_bound = SCORE_BOUND_MARGIN * DIFF_DH * jnp.max(jnp.abs(dq_gain)) * jnp.max(jnp.abs(dk_gain))
    lay = dict(
        scal=jnp.stack([lam, score_bound]).astype(F32),
        out_scale=1.0 - lam_init,
        attn_norm=p["attn_norm"][l].reshape(1, D_MODEL).astype(F32),
        w_in=p["w_in"][l].astype(BF16),
        gains=gains,
        g_sub=p["diff_sub_norm"][l].reshape(1, LANES).astype(F32),
        na_bias=_na_bias_table(p["na_rpb"][l]),
        g_na_out=jnp.tile(p["na_out_norm"][l].astype(F32), 2).reshape(1, LANES),
        w_out=p["w_out"][l].astype(BF16),
        ffn_norm=p["ffn_norm"][l].reshape(1, D_MODEL).astype(F32),
    )
    i = l // 2
    if l % 2 == 0:
        lay.update(wg=p["dense_w_gate"][i].astype(BF16), wu=p["dense_w_up"][i].astype(BF16),
                   wd=p["dense_w_down"][i].astype(BF16))
    else:
        wr = p["moe_router"][i].astype(F32)
        wr_hi = wr.astype(BF16)
        wr_lo = (wr - wr_hi.astype(F32)).astype(BF16)
        wr2 = jnp.zeros((D_MODEL, LANES), BF16)
        wr2 = wr2.at[:, :N_EXPERTS].set(wr_hi).at[:, N_EXPERTS:2 * N_EXPERTS].set(wr_lo)
        lay.update(wr=wr2, wg=p["moe_w_gate"][i].astype(BF16), wu=p["moe_w_up"][i].astype(BF16),
                   wd=p["moe_w_down"][i].astype(BF16))
    return lay


def _run_trunk(x, layers, bd):
    b, s, _ = x.shape
    t = b * s
    cos, sin = _rope_tables(s)
    x = x.reshape(t, D_MODEL)
    for l, lay in enumerate(layers):
        proj = _in_proj(x, lay["attn_norm"], lay["w_in"], lay["gains"], cos, sin, bd, s)
        proj3 = proj.reshape(b, s, IN_WIDTH)
        a_out = _diff_attn(proj3, lay["scal"], lay["g_sub"], lay["out_scale"])
        b_out = _na_attn(proj3, lay["na_bias"], lay["g_na_out"])
        a_out = a_out.reshape(t, HALF_WIDTH)
        b_out = b_out.reshape(t, HALF_WIDTH)
        if l % 2 == 0:
            x, h = _out_proj(x, a_out, b_out, lay["w_out"], lay["ffn_norm"])
            x = _swiglu(x, h, lay["wg"], lay["wu"], lay["wd"])
        else:
            x, h, c = _out_proj(x, a_out, b_out, lay["w_out"], lay["ffn_norm"], lay["wr"])
            x = _moe_sparse(x, h, c, lay["wg"], lay["wu"], lay["wd"])
    return x.reshape(b, s, D_MODEL)


def kernel(x_prompt, x_sample, attn_norm, w_in, w_out, diff_q_norm, diff_k_norm, lambda_q, lambda_k, diff_sub_norm, na_q_norm, na_k_norm, na_rpb, na_out_norm, ffn_norm, dense_w_gate, dense_w_up, dense_w_down, moe_router, moe_w_gate, moe_w_up, moe_w_down):
    p = dict(attn_norm=attn_norm, w_in=w_in, w_out=w_out, diff_q_norm=diff_q_norm, diff_k_norm=diff_k_norm,
             lambda_q=lambda_q, lambda_k=lambda_k, diff_sub_norm=diff_sub_norm, na_q_norm=na_q_norm,
             na_k_norm=na_k_norm, na_rpb=na_rpb, na_out_norm=na_out_norm, ffn_norm=ffn_norm,
             dense_w_gate=dense_w_gate, dense_w_up=dense_w_up, dense_w_down=dense_w_down,
             moe_router=moe_router, moe_w_gate=moe_w_gate, moe_w_up=moe_w_up, moe_w_down=moe_w_down)
    layers = [_prepare_layer(l, p) for l in range(DEPTH)]
    bd = _block_diag_ones()
    return (_run_trunk(x_prompt, layers, bd), _run_trunk(x_sample, layers, bd))
```

```python
import functools
import math

import jax
import jax.numpy as jnp
import numpy as np
from jax import lax
from jax.experimental import pallas as pl
from jax.experimental.pallas import tpu as pltpu

D_MODEL = 1024
DEPTH = 4
GRID_W = 64
DIFF_HEADS = 4
DIFF_DH = 64
NA_HEADS = 8
NA_DH = 64
NA_KH = 8
NA_KW = 16
HALF_WIDTH = 512
IN_WIDTH = 6 * HALF_WIDTH
D_FF = 2816
N_EXPERTS = 8
ROPE_THETA = 10000.0
EPS = 1e-6
MASK_VALUE = -1e30
SCORE_BOUND_MARGIN = 1.02
MAX_SAFE_SCORE_BOUND = 60.0

LANES = 128
VMEM_LIMIT = 56 * 1024 * 1024

BF16 = jnp.bfloat16
F32 = jnp.float32


def _cparams(sem):
    return pltpu.CompilerParams(dimension_semantics=sem, vmem_limit_bytes=VMEM_LIMIT)


def _dot(a, b):
    return jnp.dot(a, b, preferred_element_type=F32)


def _dot_nt(a, b):
    return lax.dot_general(a, b, (((1,), (1,)), ((), ())), preferred_element_type=F32)


def _rms(x, g):
    return x * lax.rsqrt(jnp.mean(x * x, axis=-1, keepdims=True) + EPS) * g


def _group_norm64(y, gain, bd_ref):
    parts = []
    for c in range(HALF_WIDTH // 256):
        yc = y[:, c * 256:(c + 1) * 256]
        ss = _dot((yc * yc).astype(BF16), bd_ref[...])
        parts.append(yc * lax.rsqrt(ss * (1.0 / 64) + EPS))
    return jnp.concatenate(parts, axis=1) * gain


def _rope64(y, cos, sin_signed):
    n = y.shape[1]
    lane = lax.broadcasted_iota(jnp.int32, y.shape, 1)
    first_half = (lane % 64) < 32
    rot = jnp.where(first_half, pltpu.roll(y, n - 32, 1), pltpu.roll(y, 32, 1))
    return y * cos + rot * sin_signed


def _in_proj_kernel(x_ref, g_ref, w_ref, gains_ref, cos_ref, sin_ref, bd_ref, o_ref):
    x = x_ref[...]
    h = _rms(x, g_ref[...]).astype(BF16)
    cos = jnp.concatenate([cos_ref[...]] * (HALF_WIDTH // LANES), axis=1)
    sin = jnp.concatenate([sin_ref[...]] * (HALF_WIDTH // LANES), axis=1)
    for sec in range(6):
        y = _dot(h, w_ref[:, sec * HALF_WIDTH:(sec + 1) * HALF_WIDTH])
        if sec in (0, 1, 3, 4):
            y = _group_norm64(y, gains_ref[sec:sec + 1, :], bd_ref)
        if sec in (0, 1):
            y = _rope64(y, cos, sin)
        o_ref[:, sec * HALF_WIDTH:(sec + 1) * HALF_WIDTH] = y.astype(BF16)


def _in_proj(x, g, w, gains, cos, sin, bd, seq, tm=512):
    t = x.shape[0]
    tm = min(tm, seq)
    nseq = seq // tm
    return pl.pallas_call(
        _in_proj_kernel,
        grid=(t // tm,),
        in_specs=[
            pl.BlockSpec((tm, D_MODEL), lambda i: (i, 0)),
            pl.BlockSpec((1, D_MODEL), lambda i: (0, 0)),
            pl.BlockSpec((D_MODEL, IN_WIDTH), lambda i: (0, 0)),
            pl.BlockSpec((8, HALF_WIDTH), lambda i: (0, 0)),
            pl.BlockSpec((tm, LANES), lambda i: (i % nseq, 0)),
            pl.BlockSpec((tm, LANES), lambda i: (i % nseq, 0)),
            pl.BlockSpec((256, 256), lambda i: (0, 0)),
        ],
        out_specs=pl.BlockSpec((tm, IN_WIDTH), lambda i: (i, 0)),
        out_shape=jax.ShapeDtypeStruct((t, IN_WIDTH), BF16),
        compiler_params=_cparams(("parallel",)),
        name="in_proj",
    )(x, g, w, gains, cos, sin, bd)


def _diff_attn_kernel(scal_ref, q_ref, k_ref, v_ref, g_ref, o_ref, acc0_ref, acc1_ref, l0_ref, l1_ref, *,
                      tk, out_scale, bounded):
    q = q_ref[0]
    tq = q.shape[0]
    lane = lax.broadcasted_iota(jnp.int32, q.shape, 1)
    qm = (jnp.where(lane < DIFF_DH, q, jnp.zeros_like(q)), jnp.where(lane >= DIFF_DH, q, jnp.zeros_like(q)))
    accs = (acc0_ref, acc1_ref)
    ls = (l0_ref, l1_ref)
    for r in accs + ls:
        r[...] = jnp.zeros_like(r)
    nk = k_ref.shape[1] // tk

    def lane_fold(p):
        out = p[:, :LANES]
        for c in range(1, tk // LANES):
            out = out + p[:, c * LANES:(c + 1) * LANES]
        return out

    def load(j):
        start = pl.multiple_of(j * tk, tk)
        return k_ref[0, pl.ds(start, tk), :], v_ref[0, pl.ds(start, tk), :]

    if bounded:
        bound = scal_ref[1]

        def body(j, carry):
            k, v = load(j)
            for a in range(2):
                p = jnp.exp2(_dot_nt(qm[a], k) - bound)
                ls[a][...] += lane_fold(p)
                accs[a][...] += _dot(p.astype(BF16), v)
            return carry

        lax.fori_loop(0, nk, body, 0)
    else:
        def body(j, ms):
            k, v = load(j)
            new_ms = []
            for a in range(2):
                s = _dot_nt(qm[a], k)
                m_new = jnp.maximum(ms[a], jnp.max(s, axis=-1, keepdims=True))
                alpha = jnp.exp2(ms[a] - m_new)
                p = jnp.exp2(s - m_new)
                ls[a][...] = alpha * ls[a][...] + lane_fold(p)
                accs[a][...] = alpha * accs[a][...] + _dot(p.astype(BF16), v)
                new_ms.append(m_new)
            return tuple(new_ms)

        neg = jnp.full((tq, 1), -jnp.inf, F32)
        lax.fori_loop(0, nk, body, (neg, neg))

    lam = scal_ref[0]
    l0 = jnp.sum(l0_ref[...], axis=-1, keepdims=True)
    l1 = jnp.sum(l1_ref[...], axis=-1, keepdims=True)
    o = acc0_ref[...] / l0 - lam * (acc1_ref[...] / l1)
    o = _rms(o, g_ref[...]) * out_scale
    o_ref[0] = o.astype(BF16)


def _diff_attn_t_kernel(scal_ref, q_ref, k_ref, vt_ref, g_ref, o_ref, acc0_ref, acc1_ref, l0_ref, l1_ref, *,
                        tk, out_scale):
    q = q_ref[0]
    lane = lax.broadcasted_iota(jnp.int32, q.shape, 1)
    qm = (jnp.where(lane < DIFF_DH, q, jnp.zeros_like(q)), jnp.where(lane >= DIFF_DH, q, jnp.zeros_like(q)))
    accs = (acc0_ref, acc1_ref)
    ls = (l0_ref, l1_ref)
    for r in accs + ls:
        r[...] = jnp.zeros_like(r)
    nk = k_ref.shape[1] // tk
    tq = q.shape[0]
    bound = scal_ref[1]

    def body(j, carry):
        start = pl.multiple_of(j * tk, tk)
        k = k_ref[0, pl.ds(start, tk), :]
        vt = vt_ref[0, :, pl.ds(start, tk)]
        for a in range(2):
            p = jnp.exp2(_dot_nt(k, qm[a]) - bound)
            ls[a][...] += jnp.sum(p.reshape(tk // 8, 8, tq), axis=0)
            accs[a][...] += _dot(vt, p.astype(BF16))
        return carry

    lax.fori_loop(0, nk, body, 0)
    lam = scal_ref[0]
    l0 = jnp.sum(l0_ref[...], axis=0, keepdims=True)
    l1 = jnp.sum(l1_ref[...], axis=0, keepdims=True)
    ot = acc0_ref[...] / l0 - lam * (acc1_ref[...] / l1)
    ot = ot * lax.rsqrt(jnp.mean(ot * ot, axis=0, keepdims=True) + EPS)
    o_ref[0] = (ot.T * g_ref[...] * out_scale).astype(BF16)


def _diff_attn_t_call(proj, scal, g_sub, out_scale, tq=1024, tk=1024):
    b, s, _ = proj.shape
    tq, tk = min(tq, s), min(tk, s)
    vt = jnp.swapaxes(proj[:, :, 2 * HALF_WIDTH:3 * HALF_WIDTH], 1, 2)
    kern = functools.partial(_diff_attn_t_kernel, tk=tk, out_scale=out_scale)
    return pl.pallas_call(
        kern,
        grid=(b, DIFF_HEADS, s // tq),
        in_specs=[
            pl.BlockSpec(memory_space=pltpu.SMEM),
            pl.BlockSpec((1, tq, LANES), lambda bi, h, i: (bi, i, h)),
            pl.BlockSpec((1, s, LANES), lambda bi, h, i: (bi, 0, DIFF_HEADS + h)),
            pl.BlockSpec((1, LANES, s), lambda bi, h, i: (bi, h, 0)),
            pl.BlockSpec((1, LANES), lambda bi, h, i: (0, 0)),
        ],
        out_specs=pl.BlockSpec((1, tq, LANES), lambda bi, h, i: (bi, i, h)),
        out_shape=jax.ShapeDtypeStruct((b, s, HALF_WIDTH), BF16),
        scratch_shapes=[pltpu.VMEM((LANES, tq), F32)] * 2 + [pltpu.VMEM((8, tq), F32)] * 2,
        compiler_params=_cparams(("parallel", "parallel", "arbitrary")),
        name="diff_attn_t",
    )(scal, proj, proj, vt, g_sub)


def _diff_attn(proj, scal, g_sub, out_scale):
    return lax.cond(
        scal[1] <= MAX_SAFE_SCORE_BOUND,
        lambda: _diff_attn_t_call(proj, scal, g_sub, out_scale),
        lambda: _diff_attn_call(proj, scal, g_sub, out_scale, False))


def _diff_attn_call(proj, scal, g_sub, out_scale, bounded, tq=1024, tk=1024):
    b, s, _ = proj.shape
    tq, tk = min(tq, s), min(tk, s)
    kern = functools.partial(_diff_attn_kernel, tk=tk, out_scale=out_scale, bounded=bounded)
    return pl.pallas_call(
        kern,
        grid=(b, DIFF_HEADS, s // tq),
        in_specs=[
            pl.BlockSpec(memory_space=pltpu.SMEM),
            pl.BlockSpec((1, tq, LANES), lambda bi, h, i: (bi, i, h)),
            pl.BlockSpec((1, s, LANES), lambda bi, h, i: (bi, 0, DIFF_HEADS + h)),
            pl.BlockSpec((1, s, LANES), lambda bi, h, i: (bi, 0, 2 * DIFF_HEADS + h)),
            pl.BlockSpec((1, LANES), lambda bi, h, i: (0, 0)),
        ],
        out_specs=pl.BlockSpec((1, tq, LANES), lambda bi, h, i: (bi, i, h)),
        out_shape=jax.ShapeDtypeStruct((b, s, HALF_WIDTH), BF16),
        scratch_shapes=[pltpu.VMEM((tq, LANES), F32)] * 4,
        compiler_params=_cparams(("parallel", "parallel", "arbitrary")),
        name="diff_attn_bounded" if bounded else "diff_attn_online",
    )(scal, proj, proj, proj, g_sub)


NA_ROWS_PER_STEP = 8
NA_BLOCK = NA_ROWS_PER_STEP * GRID_W


def _na_attn_kernel(q_ref, kp_ref, kc_ref, kn_ref, vp_ref, vc_ref, vn_ref, bias_ref, g_ref, o_ref,
                    kbuf, vbuf, *, rows):
    i = pl.program_id(1)
    for n, (kr, vr) in enumerate(((kp_ref, vp_ref), (kc_ref, vc_ref), (kn_ref, vn_ref))):
        kbuf[n * NA_BLOCK:(n + 1) * NA_BLOCK, :] = kr[0]
        vbuf[n * NA_BLOCK:(n + 1) * NA_BLOCK, :] = vr[0]
    lane = lax.broadcasted_iota(jnp.int32, (GRID_W, LANES), 1)
    halves = (lane < NA_DH, lane >= NA_DH)
    lane_k = lax.broadcasted_iota(jnp.int32, (NA_KH * GRID_W, LANES), 1)
    halves_k = (lane_k < NA_DH, lane_k >= NA_DH)

    def row_body(qr, carry):
        r = i * NA_ROWS_PER_STEP + qr
        r0 = jnp.clip(r - NA_KH // 2, 0, rows - NA_KH)
        start = pl.multiple_of((r0 - (i - 1) * NA_ROWS_PER_STEP) * GRID_W, GRID_W)
        rp = r - r0
        qrow = pl.multiple_of(qr * GRID_W, GRID_W)
        npairs = NA_HEADS // 2
        col = [slice(p * LANES, (p + 1) * LANES) for p in range(npairs)]
        scores = []
        for p in range(npairs):
            qp = q_ref[0, pl.ds(qrow, GRID_W), col[p]]
            kp = kbuf[pl.ds(start, NA_KH * GRID_W), col[p]]
            q2 = jnp.concatenate([jnp.where(h, qp, jnp.zeros_like(qp)) for h in halves], axis=0)
            scores.append(_dot_nt(q2, kp) + bias_ref[p, rp])
        probs = []
        for s in scores:
            m = jnp.max(s, axis=-1, keepdims=True)
            pe = jnp.exp2(s - m)
            probs.append((pe.astype(BF16), jnp.sum(pe, axis=-1, keepdims=True)))
        for p in range(npairs):
            vp = vbuf[pl.ds(start, NA_KH * GRID_W), col[p]]
            v2 = jnp.concatenate([jnp.where(h, vp, jnp.zeros_like(vp)) for h in halves_k], axis=0)
            pe, l = probs[p]
            p2 = jnp.concatenate([pe[:GRID_W], pe[GRID_W:]], axis=1)
            o = _dot(p2, v2) * jnp.where(halves[0], 1.0 / l[:GRID_W], 1.0 / l[GRID_W:])
            o2 = o * o
            ss = [jnp.sum(jnp.where(h, o2, 0.0), axis=-1, keepdims=True) for h in halves]
            inv = jnp.where(halves[0], lax.rsqrt(ss[0] * (1.0 / NA_DH) + EPS),
                            lax.rsqrt(ss[1] * (1.0 / NA_DH) + EPS))
            o_ref[0, pl.ds(qrow, GRID_W), col[p]] = (o * inv * g_ref[...]).astype(BF16)
        return carry

    lax.fori_loop(0, NA_ROWS_PER_STEP, row_body, 0)


def _na_attn(proj, bias, g_out):
    b, s, _ = proj.shape
    rows = s // GRID_W
    nb = rows // NA_ROWS_PER_STEP
    qcol, kcol, vcol = 3, 4, 5
    blk = (1, NA_BLOCK, HALF_WIDTH)

    def prev(bi, i, c):
        return (bi, jnp.maximum(i - 1, 0), c)

    def nxt(bi, i, c):
        return (bi, jnp.minimum(i + 1, nb - 1), c)

    return pl.pallas_call(
        functools.partial(_na_attn_kernel, rows=rows),
        grid=(b, nb),
        in_specs=[
            pl.BlockSpec(blk, lambda bi, i: (bi, i, qcol)),
            pl.BlockSpec(blk, lambda bi, i: prev(bi, i, kcol)),
            pl.BlockSpec(blk, lambda bi, i: (bi, i, kcol)),
            pl.BlockSpec(blk, lambda bi, i: nxt(bi, i, kcol)),
            pl.BlockSpec(blk, lambda bi, i: prev(bi, i, vcol)),
            pl.BlockSpec(blk, lambda bi, i: (bi, i, vcol)),
            pl.BlockSpec(blk, lambda bi, i: nxt(bi, i, vcol)),
            pl.BlockSpec((NA_HEADS // 2, NA_KH, 2 * GRID_W, NA_KH * GRID_W), lambda bi, i: (0, 0, 0, 0)),
            pl.BlockSpec((1, LANES), lambda bi, i: (0, 0)),
        ],
        out_specs=pl.BlockSpec(blk, lambda bi, i: (bi, i, 0)),
        out_shape=jax.ShapeDtypeStruct((b, s, HALF_WIDTH), BF16),
        scratch_shapes=[pltpu.VMEM((3 * NA_BLOCK, HALF_WIDTH), BF16),
                        pltpu.VMEM((3 * NA_BLOCK, HALF_WIDTH), BF16)],
        compiler_params=_cparams(("parallel", "arbitrary")),
        name="na_attn",
    )(proj, proj, proj, proj, proj, proj, proj, bias, g_out)


def _na_bias_table(rpb):
    qc = np.arange(GRID_W)
    kc = np.arange(GRID_W)
    win = np.clip(qc - NA_KW // 2, 0, GRID_W - NA_KW)
    valid = (kc[None, :] >= win[:, None]) & (kc[None, :] < win[:, None] + NA_KW)
    dc = np.clip(kc[None, :] - qc[:, None] + NA_KW - 1, 0, 2 * NA_KW - 2)
    rp = np.arange(NA_KH)
    j = np.arange(NA_KH)
    dr = j[None, :] - rp[:, None] + NA_KH - 1
    t = (rpb.astype(F32) * math.log2(math.e))[:, dr][:, :, :, dc]
    t = jnp.where(jnp.asarray(valid)[None, None, None], t, MASK_VALUE)
    t = t.reshape(NA_HEADS // 2, 2, NA_KH, NA_KH, GRID_W, GRID_W)
    t = t.transpose(0, 2, 1, 4, 3, 5)
    return t.reshape(NA_HEADS // 2, NA_KH, 2 * GRID_W, NA_KH * GRID_W)


def _out_proj_kernel(x_ref, a_ref, b_ref, w_ref, g_ref, xo_ref, h_ref):
    x = x_ref[...] + _dot(a_ref[...], w_ref[:HALF_WIDTH, :]) + _dot(b_ref[...], w_ref[HALF_WIDTH:, :])
    xo_ref[...] = x
    h_ref[...] = _rms(x, g_ref[...]).astype(BF16)


def _out_proj_router_kernel(x_ref, a_ref, b_ref, w_ref, g_ref, wr_ref, xo_ref, h_ref, c_ref):
    x = x_ref[...] + _dot(a_ref[...], w_ref[:HALF_WIDTH, :]) + _dot(b_ref[...], w_ref[HALF_WIDTH:, :])
    xo_ref[...] = x
    h = _rms(x, g_ref[...])
    h_hi = h.astype(BF16)
    h_ref[...] = h_hi
    h_lo = (h - h_hi.astype(F32)).astype(BF16)
    wr = wr_ref[...]
    both = _dot(h_hi, wr)
    lane = lax.broadcasted_iota(jnp.int32, both.shape, 1).astype(F32)
    lo_part = pltpu.roll(both, LANES - N_EXPERTS, 1)
    logits = both + lo_part + _dot(h_lo, wr)
    logits = jnp.where(lane < N_EXPERTS, logits, -jnp.inf)
    m1 = jnp.max(logits, axis=-1, keepdims=True)
    i1 = jnp.min(jnp.where(logits == m1, lane, float(LANES)), axis=-1, keepdims=True)
    rest = jnp.where(lane == i1, -jnp.inf, logits)
    m2 = jnp.max(rest, axis=-1, keepdims=True)
    i2 = jnp.min(jnp.where(rest == m2, lane, float(LANES)), axis=-1, keepdims=True)
    e2 = jnp.exp(m2 - m1)
    g1 = 1.0 / (1.0 + e2)
    g2 = e2 * g1
    c_ref[...] = jnp.where(lane == i1, g1, jnp.where(lane == i2, g2, 0.0))


def _out_proj(x, a, b, w, g, wr=None, tm=512):
    t = x.shape[0]
    tm = min(tm, t)
    row = lambda i: (i, 0)
    const = lambda i: (0, 0)
    in_specs = [
        pl.BlockSpec((tm, D_MODEL), row),
        pl.BlockSpec((tm, HALF_WIDTH), row),
        pl.BlockSpec((tm, HALF_WIDTH), row),
        pl.BlockSpec((D_MODEL, D_MODEL), const),
        pl.BlockSpec((1, D_MODEL), const),
    ]
    out_specs = [pl.BlockSpec((tm, D_MODEL), row), pl.BlockSpec((tm, D_MODEL), row)]
    out_shape = [jax.ShapeDtypeStruct((t, D_MODEL), F32), jax.ShapeDtypeStruct((t, D_MODEL), BF16)]
    args = [x, a, b, w, g]
    kern = _out_proj_kernel
    if wr is not None:
        in_specs.append(pl.BlockSpec((D_MODEL, LANES), const))
        out_specs.append(pl.BlockSpec((tm, LANES), row))
        out_shape.append(jax.ShapeDtypeStruct((t, LANES), F32))
        args.append(wr)
        kern = _out_proj_router_kernel
    return pl.pallas_call(
        kern,
        grid=(t // tm,),
        in_specs=in_specs,
        out_specs=out_specs,
        out_shape=out_shape,
        compiler_params=_cparams(("parallel",)),
        name="out_proj",
    )(*args)


def _swiglu_kernel(x_ref, h_ref, wg_ref, wu_ref, wd_ref, o_ref, acc_ref):
    f = pl.program_id(1)

    @pl.when(f == 0)
    def _():
        acc_ref[...] = x_ref[...]

    h = h_ref[...]
    g = _dot(h, wg_ref[...])
    u = _dot(h, wu_ref[...])
    a = (g * jax.nn.sigmoid(g) * u).astype(BF16)
    acc_ref[...] += _dot(a, wd_ref[...])

    @pl.when(f == pl.num_programs(1) - 1)
    def _():
        o_ref[...] = acc_ref[...]


def _swiglu(x, h, wg, wu, wd, tm=1024, tf=1408):
    t = x.shape[0]
    tm = min(tm, t)
    return pl.pallas_call(
        _swiglu_kernel,
        grid=(t // tm, D_FF // tf),
        in_specs=[
            pl.BlockSpec((tm, D_MODEL), lambda i, f: (i, 0)),
            pl.BlockSpec((tm, D_MODEL), lambda i, f: (i, 0)),
            pl.BlockSpec((D_MODEL, tf), lambda i, f: (0, f)),
            pl.BlockSpec((D_MODEL, tf), lambda i, f: (0, f)),
            pl.BlockSpec((tf, D_MODEL), lambda i, f: (f, 0)),
        ],
        out_specs=pl.BlockSpec((tm, D_MODEL), lambda i, f: (i, 0)),
        out_shape=jax.ShapeDtypeStruct((t, D_MODEL), F32),
        scratch_shapes=[pltpu.VMEM((tm, D_MODEL), F32)],
        compiler_params=_cparams(("parallel", "arbitrary")),
        name="swiglu",
    )(x, h, wg, wu, wd)


MOE_TILE = 1024
MOE_CHUNK = 256
TRI_BLOCK = 512


def _route_kernel(c_ref, ltri_ref, utri_ref, posm_ref, posmt_ref):
    tm = c_ref.shape[0]
    carry = jnp.zeros((1, LANES), F32)
    carry_t = jnp.zeros((LANES, 1), F32)
    for b in range(tm // TRI_BLOCK):
        rows = slice(b * TRI_BLOCK, (b + 1) * TRI_BLOCK)
        sel = c_ref[rows, :] > 0.0
        self = jnp.where(sel, 1.0, 0.0)
        pos = _dot(ltri_ref[...], self.astype(BF16)) + carry
        posm_ref[rows, :] = jnp.where(sel, pos, -1.0)
        sel_t = self.T
        pos_t = _dot(sel_t.astype(BF16), utri_ref[...]) + carry_t
        posmt_ref[0, :, rows] = jnp.where(sel_t > 0.0, pos_t, -1.0)
        carry = carry + jnp.sum(self, axis=0, keepdims=True)
        carry_t = carry_t + jnp.sum(sel_t, axis=1, keepdims=True)


def _route(c, tm):
    t = c.shape[0]
    idx = np.arange(TRI_BLOCK)
    ltri = jnp.asarray(idx[None, :] < idx[:, None], dtype=BF16)
    utri = jnp.asarray(idx[:, None] < idx[None, :], dtype=BF16)
    return pl.pallas_call(
        _route_kernel,
        grid=(t // tm,),
        in_specs=[
            pl.BlockSpec((tm, LANES), lambda i: (i, 0)),
            pl.BlockSpec((TRI_BLOCK, TRI_BLOCK), lambda i: (0, 0)),
            pl.BlockSpec((TRI_BLOCK, TRI_BLOCK), lambda i: (0, 0)),
        ],
        out_specs=[pl.BlockSpec((tm, LANES), lambda i: (i, 0)),
                   pl.BlockSpec((1, LANES, tm), lambda i: (i, 0, 0))],
        out_shape=[jax.ShapeDtypeStruct((t, LANES), F32),
                   jax.ShapeDtypeStruct((t // tm, LANES, tm), F32)],
        compiler_params=_cparams(("parallel",)),
        name="moe_route",
    )(c, ltri, utri)


def _moe_sparse_kernel(x_ref, h_ref, c_ref, posm_ref, posmt_ref, wg_ref, wu_ref, wd_ref, o_ref, hc_ref, y_ref):
    e = pl.program_id(1)
    f = pl.program_id(2)
    last_f = pl.num_programs(2) - 1
    tm = x_ref.shape[0]

    @pl.when((e == 0) & (f == 0))
    def _():
        o_ref[...] = x_ref[...]

    pos_row = posmt_ref[0, pl.ds(e, 1), :]
    load = (jnp.max(pos_row) + 1.0).astype(jnp.int32)
    nchunks = (load + MOE_CHUNK - 1) // MOE_CHUNK

    def rows_of(k):
        return pl.ds(pl.multiple_of(k * MOE_CHUNK, MOE_CHUNK), MOE_CHUNK)

    @pl.when(f == 0)
    def _():
        def gather_chunk(k, carry):
            slot = (lax.broadcasted_iota(jnp.int32, (MOE_CHUNK, tm), 0) + k * MOE_CHUNK).astype(F32)
            onehot = jnp.where(pos_row == slot, 1.0, 0.0).astype(BF16)
            hc_ref[rows_of(k), :] = _dot(onehot, h_ref[...]).astype(BF16)
            return carry

        lax.fori_loop(0, nchunks, gather_chunk, 0)

    def expert_chunk(k, carry):
        hk = hc_ref[rows_of(k), :]
        g = _dot(hk, wg_ref[0])
        u = _dot(hk, wu_ref[0])
        y = _dot((g * jax.nn.sigmoid(g) * u).astype(BF16), wd_ref[0])

        @pl.when(f == 0)
        def _():
            y_ref[rows_of(k), :] = y

        @pl.when(f > 0)
        def _():
            y_ref[rows_of(k), :] += y

        return carry

    lax.fori_loop(0, nchunks, expert_chunk, 0)

    @pl.when(f == last_f)
    def _():
        lane = lax.broadcasted_iota(jnp.int32, (tm, LANES), 1)
        pos = jnp.sum(jnp.where(lane == e, posm_ref[...], 0.0), axis=-1, keepdims=True)
        gate = jnp.sum(jnp.where(lane == e, c_ref[...], 0.0), axis=-1, keepdims=True)

        def scatter_chunk(k, carry):
            slot = (lax.broadcasted_iota(jnp.int32, (tm, MOE_CHUNK), 1) + k * MOE_CHUNK).astype(F32)
            onehot = jnp.where(pos == slot, 1.0, 0.0).astype(BF16)
            o_ref[...] += gate * _dot(onehot, y_ref[rows_of(k), :].astype(BF16))
            return carry

        lax.fori_loop(0, nchunks, scatter_chunk, 0)


def _moe_sparse(x, h, c, wg, wu, wd, tm=MOE_TILE, tf=1408):
    t = x.shape[0]
    tm = min(tm, t)
    posm, posmt = _route(c, tm)
    return pl.pallas_call(
        _moe_sparse_kernel,
        grid=(t // tm, N_EXPERTS, D_FF // tf),
        in_specs=[
            pl.BlockSpec((tm, D_MODEL), lambda i, e, f: (i, 0)),
            pl.BlockSpec((tm, D_MODEL), lambda i, e, f: (i, 0)),
            pl.BlockSpec((tm, LANES), lambda i, e, f: (i, 0)),
            pl.BlockSpec((tm, LANES), lambda i, e, f: (i, 0)),
            pl.BlockSpec((1, LANES, tm), lambda i, e, f: (i, 0, 0)),
            pl.BlockSpec((1, D_MODEL, tf), lambda i, e, f: (e, 0, f)),
            pl.BlockSpec((1, D_MODEL, tf), lambda i, e, f: (e, 0, f)),
            pl.BlockSpec((1, tf, D_MODEL), lambda i, e, f: (e, f, 0)),
        ],
        out_specs=pl.BlockSpec((tm, D_MODEL), lambda i, e, f: (i, 0)),
        out_shape=jax.ShapeDtypeStruct((t, D_MODEL), F32),
        scratch_shapes=[pltpu.VMEM((tm, D_MODEL), BF16), pltpu.VMEM((tm, D_MODEL), F32)],
        compiler_params=_cparams(("parallel", "arbitrary", "arbitrary")),
        name="moe_sparse",
    )(x, h, c, posm, posmt, wg, wu, wd)


def _rope_tables(seq):
    inv = 1.0 / (ROPE_THETA ** (jnp.arange(0, DIFF_DH, 2, dtype=F32) / DIFF_DH))
    ang = jnp.arange(seq, dtype=F32)[:, None] * inv[None, :]
    ang = jnp.concatenate([ang, ang], axis=-1)
    cos, sin = jnp.cos(ang), jnp.sin(ang)
    sign = jnp.where(jnp.arange(DIFF_DH) < DIFF_DH // 2, -1.0, 1.0).astype(F32)
    return jnp.tile(cos, (1, 2)), jnp.tile(sin * sign, (1, 2))


def _block_diag_ones():
    idx = np.arange(256) // 64
    return jnp.asarray(idx[:, None] == idx[None, :], dtype=BF16)


def _prepare_layer(l, p):
    lam_init = 0.8 - 0.6 * math.exp(-0.3 * l)
    lq = p["lambda_q"][l].astype(F32)
    lk = p["lambda_k"][l].astype(F32)
    lam = jnp.exp(jnp.sum(lq[0] * lk[0])) - jnp.exp(jnp.sum(lq[1] * lk[1])) + lam_init
    scale = DIFF_DH ** -0.5 * math.log2(math.e)
    tile8 = lambda g: jnp.tile(g.astype(F32), HALF_WIDTH // g.shape[0])
    zeros = jnp.zeros((HALF_WIDTH,), F32)
    dq_gain = p["diff_q_norm"][l].astype(F32) * scale
    dk_gain = p["diff_k_norm"][l].astype(F32)
    gains = jnp.stack([
        tile8(dq_gain), tile8(dk_gain), zeros,
        tile8(p["na_q_norm"][l]) * scale, tile8(p["na_k_norm"][l]), zeros, zeros, zeros])
    score_bound = SCORE_BOUND_MARGIN * DIFF_DH * jnp.max(jnp.abs(dq_gain)) * jnp.max(jnp.abs(dk_gain))
    lay = dict(
        scal=jnp.stack([lam, score_bound]).astype(F32),
        out_scale=1.0 - lam_init,
        attn_norm=p["attn_norm"][l].reshape(1, D_MODEL).astype(F32),
        w_in=p["w_in"][l].astype(BF16),
        gains=gains,
        g_sub=p["diff_sub_norm"][l].reshape(1, LANES).astype(F32),
        na_bias=_na_bias_table(p["na_rpb"][l]),
        g_na_out=jnp.tile(p["na_out_norm"][l].astype(F32), 2).reshape(1, LANES),
        w_out=p["w_out"][l].astype(BF16),
        ffn_norm=p["ffn_norm"][l].reshape(1, D_MODEL).astype(F32),
    )
    i = l // 2
    if l % 2 == 0:
        lay.update(wg=p["dense_w_gate"][i].astype(BF16), wu=p["dense_w_up"][i].astype(BF16),
                   wd=p["dense_w_down"][i].astype(BF16))
    else:
        wr = p["moe_router"][i].astype(F32)
        wr_hi = wr.astype(BF16)
        wr_lo = (wr - wr_hi.astype(F32)).astype(BF16)
        wr2 = jnp.zeros((D_MODEL, LANES), BF16)
        wr2 = wr2.at[:, :N_EXPERTS].set(wr_hi).at[:, N_EXPERTS:2 * N_EXPERTS].set(wr_lo)
        lay.update(wr=wr2, wg=p["moe_w_gate"][i].astype(BF16), wu=p["moe_w_up"][i].astype(BF16),
                   wd=p["moe_w_down"][i].astype(BF16))
    return lay


def _run_trunk(x, layers, bd):
    b, s, _ = x.shape
    t = b * s
    cos, sin = _rope_tables(s)
    x = x.reshape(t, D_MODEL)
    for l, lay in enumerate(layers):
        proj = _in_proj(x, lay["attn_norm"], lay["w_in"], lay["gains"], cos, sin, bd, s)
        proj3 = proj.reshape(b, s, IN_WIDTH)
        a_out = _diff_attn(proj3, lay["scal"], lay["g_sub"], lay["out_scale"])
        b_out = _na_attn(proj3, lay["na_bias"], lay["g_na_out"])
        a_out = a_out.reshape(t, HALF_WIDTH)
        b_out = b_out.reshape(t, HALF_WIDTH)
        if l % 2 == 0:
            x, h = _out_proj(x, a_out, b_out, lay["w_out"], lay["ffn_norm"])
            x = _swiglu(x, h, lay["wg"], lay["wu"], lay["wd"])
        else:
            x, h, c = _out_proj(x, a_out, b_out, lay["w_out"], lay["ffn_norm"], lay["wr"])
            x = _moe_sparse(x, h, c, lay["wg"], lay["wu"], lay["wd"])
    return x.reshape(b, s, D_MODEL)


def kernel(x_prompt, x_sample, attn_norm, w_in, w_out, diff_q_norm, diff_k_norm, lambda_q, lambda_k, diff_sub_norm, na_q_norm, na_k_norm, na_rpb, na_out_norm, ffn_norm, dense_w_gate, dense_w_up, dense_w_down, moe_router, moe_w_gate, moe_w_up, moe_w_down):
    p = dict(attn_norm=attn_norm, w_in=w_in, w_out=w_out, diff_q_norm=diff_q_norm, diff_k_norm=diff_k_norm,
             lambda_q=lambda_q, lambda_k=lambda_k, diff_sub_norm=diff_sub_norm, na_q_norm=na_q_norm,
             na_k_norm=na_k_norm, na_rpb=na_rpb, na_out_norm=na_out_norm, ffn_norm=ffn_norm,
             dense_w_gate=dense_w_gate, dense_w_up=dense_w_up, dense_w_down=dense_w_down,
             moe_router=moe_router, moe_w_gate=moe_w_gate, moe_w_up=moe_w_up, moe_w_down=moe_w_down)
    layers = [_prepare_layer(l, p) for l in range(DEPTH)]
    bd = _block_diag_ones()
    return (_run_trunk(x_prompt, layers, bd), _run_trunk(x_sample, layers, bd))
```

```python
import functools
import math

import jax
import jax.numpy as jnp
import numpy as np
from jax import lax
from jax.experimental import pallas as pl
from jax.experimental.pallas import tpu as pltpu

D_MODEL = 1024
DEPTH = 4
GRID_W = 64
DIFF_HEADS = 4
DIFF_DH = 64
NA_HEADS = 8
NA_DH = 64
NA_KH = 8
NA_KW = 16
HALF_WIDTH = 512
IN_WIDTH = 6 * HALF_WIDTH
D_FF = 2816
N_EXPERTS = 8
ROPE_THETA = 10000.0
EPS = 1e-6
MASK_VALUE = -1e30
SCORE_BOUND_MARGIN = 1.02
MAX_SAFE_SCORE_BOUND = 60.0

LANES = 128
VMEM_LIMIT = 56 * 1024 * 1024

BF16 = jnp.bfloat16
F32 = jnp.float32


def _cparams(sem):
    return pltpu.CompilerParams(dimension_semantics=sem, vmem_limit_bytes=VMEM_LIMIT)


def _dot(a, b):
    return jnp.dot(a, b, preferred_element_type=F32)


def _dot_nt(a, b):
    return lax.dot_general(a, b, (((1,), (1,)), ((), ())), preferred_element_type=F32)


def _rms(x, g):
    return x * lax.rsqrt(jnp.mean(x * x, axis=-1, keepdims=True) + EPS) * g


def _group_norm64(y, gain, bd_ref):
    parts = []
    for c in range(HALF_WIDTH // 256):
        yc = y[:, c * 256:(c + 1) * 256]
        ss = _dot((yc * yc).astype(BF16), bd_ref[...])
        parts.append(yc * lax.rsqrt(ss * (1.0 / 64) + EPS))
    return jnp.concatenate(parts, axis=1) * gain


def _rope64(y, cos, sin_signed):
    n = y.shape[1]
    lane = lax.broadcasted_iota(jnp.int32, y.shape, 1)
    first_half = (lane % 64) < 32
    rot = jnp.where(first_half, pltpu.roll(y, n - 32, 1), pltpu.roll(y, 32, 1))
    return y * cos + rot * sin_signed


DV_SECTION = 2


def _in_proj_kernel(x_ref, g_ref, w_ref, gains_ref, cos_ref, sin_ref, bd_ref, o_ref, vt_ref):
    x = x_ref[...]
    h = _rms(x, g_ref[...]).astype(BF16)
    cos = jnp.concatenate([cos_ref[...]] * (HALF_WIDTH // LANES), axis=1)
    sin = jnp.concatenate([sin_ref[...]] * (HALF_WIDTH // LANES), axis=1)
    for sec in range(6):
        y = _dot(h, w_ref[:, sec * HALF_WIDTH:(sec + 1) * HALF_WIDTH])
        if sec in (0, 1, 3, 4):
            y = _group_norm64(y, gains_ref[sec:sec + 1, :], bd_ref)
        if sec in (0, 1):
            y = _rope64(y, cos, sin)
        o_ref[:, sec * HALF_WIDTH:(sec + 1) * HALF_WIDTH] = y.astype(BF16)
        if sec == DV_SECTION:
            vt_ref[0] = y.T.astype(BF16)


def _in_proj(x, g, w, gains, cos, sin, bd, seq, tm=512):
    t = x.shape[0]
    tm = min(tm, seq)
    nseq = seq // tm
    return pl.pallas_call(
        _in_proj_kernel,
        grid=(t // tm,),
        in_specs=[
            pl.BlockSpec((tm, D_MODEL), lambda i: (i, 0)),
            pl.BlockSpec((1, D_MODEL), lambda i: (0, 0)),
            pl.BlockSpec((D_MODEL, IN_WIDTH), lambda i: (0, 0)),
            pl.BlockSpec((8, HALF_WIDTH), lambda i: (0, 0)),
            pl.BlockSpec((tm, LANES), lambda i: (i % nseq, 0)),
            pl.BlockSpec((tm, LANES), lambda i: (i % nseq, 0)),
            pl.BlockSpec((256, 256), lambda i: (0, 0)),
        ],
        out_specs=[pl.BlockSpec((tm, IN_WIDTH), lambda i: (i, 0)),
                   pl.BlockSpec((1, HALF_WIDTH, tm), lambda i: (i // nseq, 0, i % nseq))],
        out_shape=[jax.ShapeDtypeStruct((t, IN_WIDTH), BF16),
                   jax.ShapeDtypeStruct((t // seq, HALF_WIDTH, seq), BF16)],
        compiler_params=_cparams(("parallel",)),
        name="in_proj",
    )(x, g, w, gains, cos, sin, bd)


def _diff_attn_kernel(scal_ref, q_ref, k_ref, v_ref, g_ref, o_ref, acc0_ref, acc1_ref, l0_ref, l1_ref, *,
                      tk, out_scale, bounded):
    q = q_ref[0]
    tq = q.shape[0]
    lane = lax.broadcasted_iota(jnp.int32, q.shape, 1)
    qm = (jnp.where(lane < DIFF_DH, q, jnp.zeros_like(q)), jnp.where(lane >= DIFF_DH, q, jnp.zeros_like(q)))
    accs = (acc0_ref, acc1_ref)
    ls = (l0_ref, l1_ref)
    for r in accs + ls:
        r[...] = jnp.zeros_like(r)
    nk = k_ref.shape[1] // tk

    def lane_fold(p):
        out = p[:, :LANES]
        for c in range(1, tk // LANES):
            out = out + p[:, c * LANES:(c + 1) * LANES]
        return out

    def load(j):
        start = pl.multiple_of(j * tk, tk)
        return k_ref[0, pl.ds(start, tk), :], v_ref[0, pl.ds(start, tk), :]

    if bounded:
        bound = scal_ref[1]

        def body(j, carry):
            k, v = load(j)
            for a in range(2):
                p = jnp.exp2(_dot_nt(qm[a], k) - bound)
                ls[a][...] += lane_fold(p)
                accs[a][...] += _dot(p.astype(BF16), v)
            return carry

        lax.fori_loop(0, nk, body, 0)
    else:
        def body(j, ms):
            k, v = load(j)
            new_ms = []
            for a in range(2):
                s = _dot_nt(qm[a], k)
                m_new = jnp.maximum(ms[a], jnp.max(s, axis=-1, keepdims=True))
                alpha = jnp.exp2(ms[a] - m_new)
                p = jnp.exp2(s - m_new)
                ls[a][...] = alpha * ls[a][...] + lane_fold(p)
                accs[a][...] = alpha * accs[a][...] + _dot(p.astype(BF16), v)
                new_ms.append(m_new)
            return tuple(new_ms)

        neg = jnp.full((tq, 1), -jnp.inf, F32)
        lax.fori_loop(0, nk, body, (neg, neg))

    lam = scal_ref[0]
    l0 = jnp.sum(l0_ref[...], axis=-1, keepdims=True)
    l1 = jnp.sum(l1_ref[...], axis=-1, keepdims=True)
    o = acc0_ref[...] / l0 - lam * (acc1_ref[...] / l1)
    o = _rms(o, g_ref[...]) * out_scale
    o_ref[0] = o.astype(BF16)


def _diff_attn_t_kernel(scal_ref, q_ref, k_ref, vt_ref, g_ref, o_ref, acc0_ref, acc1_ref, l0_ref, l1_ref, *,
                        tk, out_scale):
    q = q_ref[0]
    lane = lax.broadcasted_iota(jnp.int32, q.shape, 1)
    qm = (jnp.where(lane < DIFF_DH, q, jnp.zeros_like(q)), jnp.where(lane >= DIFF_DH, q, jnp.zeros_like(q)))
    accs = (acc0_ref, acc1_ref)
    ls = (l0_ref, l1_ref)
    for r in accs + ls:
        r[...] = jnp.zeros_like(r)
    nk = k_ref.shape[1] // tk
    tq = q.shape[0]
    bound = scal_ref[1]

    def body(j, carry):
        start = pl.multiple_of(j * tk, tk)
        k = k_ref[0, pl.ds(start, tk), :]
        vt = vt_ref[0, :, pl.ds(start, tk)]
        for a in range(2):
            p = jnp.exp2(_dot_nt(k, qm[a]) - bound)
            ls[a][...] += jnp.sum(p.reshape(tk // 8, 8, tq), axis=0)
            accs[a][...] += _dot(vt, p.astype(BF16))
        return carry

    lax.fori_loop(0, nk, body, 0)
    lam = scal_ref[0]
    l0 = jnp.sum(l0_ref[...], axis=0, keepdims=True)
    l1 = jnp.sum(l1_ref[...], axis=0, keepdims=True)
    ot = acc0_ref[...] / l0 - lam * (acc1_ref[...] / l1)
    ot = ot * lax.rsqrt(jnp.mean(ot * ot, axis=0, keepdims=True) + EPS)
    o_ref[0] = (ot.T * g_ref[...] * out_scale).astype(BF16)


def _diff_attn_t_call(proj, vt, scal, g_sub, out_scale, tq=1024, tk=1024):
    b, s, _ = proj.shape
    tq, tk = min(tq, s), min(tk, s)
    kern = functools.partial(_diff_attn_t_kernel, tk=tk, out_scale=out_scale)
    return pl.pallas_call(
        kern,
        grid=(b, DIFF_HEADS, s // tq),
        in_specs=[
            pl.BlockSpec(memory_space=pltpu.SMEM),
            pl.BlockSpec((1, tq, LANES), lambda bi, h, i: (bi, i, h)),
            pl.BlockSpec((1, s, LANES), lambda bi, h, i: (bi, 0, DIFF_HEADS + h)),
            pl.BlockSpec((1, LANES, s), lambda bi, h, i: (bi, h, 0)),
            pl.BlockSpec((1, LANES), lambda bi, h, i: (0, 0)),
        ],
        out_specs=pl.BlockSpec((1, tq, LANES), lambda bi, h, i: (bi, i, h)),
        out_shape=jax.ShapeDtypeStruct((b, s, HALF_WIDTH), BF16),
        scratch_shapes=[pltpu.VMEM((LANES, tq), F32)] * 2 + [pltpu.VMEM((8, tq), F32)] * 2,
        compiler_params=_cparams(("parallel", "parallel", "arbitrary")),
        name="diff_attn_t",
    )(scal, proj, proj, vt, g_sub)


def _diff_attn(proj, vt, scal, g_sub, out_scale):
    return lax.cond(
        scal[1] <= MAX_SAFE_SCORE_BOUND,
        lambda: _diff_attn_t_call(proj, vt, scal, g_sub, out_scale),
        lambda: _diff_attn_call(proj, scal, g_sub, out_scale, False))


def _diff_attn_call(proj, scal, g_sub, out_scale, bounded, tq=1024, tk=1024):
    b, s, _ = proj.shape
    tq, tk = min(tq, s), min(tk, s)
    kern = functools.partial(_diff_attn_kernel, tk=tk, out_scale=out_scale, bounded=bounded)
    return pl.pallas_call(
        kern,
        grid=(b, DIFF_HEADS, s // tq),
        in_specs=[
            pl.BlockSpec(memory_space=pltpu.SMEM),
            pl.BlockSpec((1, tq, LANES), lambda bi, h, i: (bi, i, h)),
            pl.BlockSpec((1, s, LANES), lambda bi, h, i: (bi, 0, DIFF_HEADS + h)),
            pl.BlockSpec((1, s, LANES), lambda bi, h, i: (bi, 0, 2 * DIFF_HEADS + h)),
            pl.BlockSpec((1, LANES), lambda bi, h, i: (0, 0)),
        ],
        out_specs=pl.BlockSpec((1, tq, LANES), lambda bi, h, i: (bi, i, h)),
        out_shape=jax.ShapeDtypeStruct((b, s, HALF_WIDTH), BF16),
        scratch_shapes=[pltpu.VMEM((tq, LANES), F32)] * 4,
        compiler_params=_cparams(("parallel", "parallel", "arbitrary")),
        name="diff_attn_bounded" if bounded else "diff_attn_online",
    )(scal, proj, proj, proj, g_sub)


NA_ROWS_PER_STEP = 8
NA_BLOCK = NA_ROWS_PER_STEP * GRID_W


def _na_attn_kernel(q_ref, kp_ref, kc_ref, kn_ref, vp_ref, vc_ref, vn_ref, bias_ref, g_ref, o_ref,
                    kbuf, vbuf, *, rows):
    i = pl.program_id(1)
    for n, (kr, vr) in enumerate(((kp_ref, vp_ref), (kc_ref, vc_ref), (kn_ref, vn_ref))):
        kbuf[n * NA_BLOCK:(n + 1) * NA_BLOCK, :] = kr[0]
        vbuf[n * NA_BLOCK:(n + 1) * NA_BLOCK, :] = vr[0]
    lane = lax.broadcasted_iota(jnp.int32, (GRID_W, LANES), 1)
    halves = (lane < NA_DH, lane >= NA_DH)

    def row_body(qr, carry):
        r = i * NA_ROWS_PER_STEP + qr
        r0 = jnp.clip(r - NA_KH // 2, 0, rows - NA_KH)
        start = pl.multiple_of((r0 - (i - 1) * NA_ROWS_PER_STEP) * GRID_W, GRID_W)
        rp = r - r0
        qrow = pl.multiple_of(qr * GRID_W, GRID_W)
        npairs = NA_HEADS // 2
        col = [slice(p * LANES, (p + 1) * LANES) for p in range(npairs)]
        scores = []
        for p in range(npairs):
            qp = q_ref[0, pl.ds(qrow, GRID_W), col[p]]
            kp = kbuf[pl.ds(start, NA_KH * GRID_W), col[p]]
            q2 = jnp.concatenate([jnp.where(h, qp, jnp.zeros_like(qp)) for h in halves], axis=0)
            scores.append(_dot_nt(q2, kp) + bias_ref[p, rp])
        probs = []
        for s in scores:
            m = jnp.max(s, axis=-1, keepdims=True)
            pe = jnp.exp2(s - m)
            probs.append((pe.astype(BF16), jnp.sum(pe, axis=-1, keepdims=True)))
        for p in range(npairs):
            vp = vbuf[pl.ds(start, NA_KH * GRID_W), col[p]]
            pe, l = probs[p]
            o_both = _dot(pe, vp) / l
            o = jnp.where(halves[0], o_both[:GRID_W], o_both[GRID_W:])
            o2 = o * o
            ss = [jnp.sum(jnp.where(h, o2, 0.0), axis=-1, keepdims=True) for h in halves]
            inv = jnp.where(halves[0], lax.rsqrt(ss[0] * (1.0 / NA_DH) + EPS),
                            lax.rsqrt(ss[1] * (1.0 / NA_DH) + EPS))
            o_ref[0, pl.ds(qrow, GRID_W), col[p]] = (o * inv * g_ref[...]).astype(BF16)
        return carry

    lax.fori_loop(0, NA_ROWS_PER_STEP, row_body, 0, unroll=4)


def _na_attn(proj, bias, g_out):
    b, s, _ = proj.shape
    rows = s // GRID_W
    nb = rows // NA_ROWS_PER_STEP
    qcol, kcol, vcol = 3, 4, 5
    blk = (1, NA_BLOCK, HALF_WIDTH)

    def prev(bi, i, c):
        return (bi, jnp.maximum(i - 1, 0), c)

    def nxt(bi, i, c):
        return (bi, jnp.minimum(i + 1, nb - 1), c)

    return pl.pallas_call(
        functools.partial(_na_attn_kernel, rows=rows),
        grid=(b, nb),
        in_specs=[
            pl.BlockSpec(blk, lambda bi, i: (bi, i, qcol)),
            pl.BlockSpec(blk, lambda bi, i: prev(bi, i, kcol)),
            pl.BlockSpec(blk, lambda bi, i: (bi, i, kcol)),
            pl.BlockSpec(blk, lambda bi, i: nxt(bi, i, kcol)),
            pl.BlockSpec(blk, lambda bi, i: prev(bi, i, vcol)),
            pl.BlockSpec(blk, lambda bi, i: (bi, i, vcol)),
            pl.BlockSpec(blk, lambda bi, i: nxt(bi, i, vcol)),
            pl.BlockSpec((NA_HEADS // 2, NA_KH, 2 * GRID_W, NA_KH * GRID_W), lambda bi, i: (0, 0, 0, 0)),
            pl.BlockSpec((1, LANES), lambda bi, i: (0, 0)),
        ],
        out_specs=pl.BlockSpec(blk, lambda bi, i: (bi, i, 0)),
        out_shape=jax.ShapeDtypeStruct((b, s, HALF_WIDTH), BF16),
        scratch_shapes=[pltpu.VMEM((3 * NA_BLOCK, HALF_WIDTH), BF16),
                        pltpu.VMEM((3 * NA_BLOCK, HALF_WIDTH), BF16)],
        compiler_params=_cparams(("parallel", "arbitrary")),
        name="na_attn",
    )(proj, proj, proj, proj, proj, proj, proj, bias, g_out)


def _na_bias_table(rpb):
    qc = np.arange(GRID_W)
    kc = np.arange(GRID_W)
    win = np.clip(qc - NA_KW // 2, 0, GRID_W - NA_KW)
    valid = (kc[None, :] >= win[:, None]) & (kc[None, :] < win[:, None] + NA_KW)
    dc = np.clip(kc[None, :] - qc[:, None] + NA_KW - 1, 0, 2 * NA_KW - 2)
    rp = np.arange(NA_KH)
    j = np.arange(NA_KH)
    dr = j[None, :] - rp[:, None] + NA_KH - 1
    t = (rpb.astype(F32) * math.log2(math.e))[:, dr][:, :, :, dc]
    t = jnp.where(jnp.asarray(valid)[None, None, None], t, MASK_VALUE)
    t = t.reshape(NA_HEADS // 2, 2, NA_KH, NA_KH, GRID_W, GRID_W)
    t = t.transpose(0, 2, 1, 4, 3, 5)
    return t.reshape(NA_HEADS // 2, NA_KH, 2 * GRID_W, NA_KH * GRID_W)


def _out_proj_kernel(x_ref, a_ref, b_ref, w_ref, g_ref, xo_ref, h_ref):
    x = x_ref[...] + _dot(a_ref[...], w_ref[:HALF_WIDTH, :]) + _dot(b_ref[...], w_ref[HALF_WIDTH:, :])
    xo_ref[...] = x
    h_ref[...] = _rms(x, g_ref[...]).astype(BF16)


def _out_proj_router_kernel(x_ref, a_ref, b_ref, w_ref, g_ref, wr_ref, xo_ref, h_ref, c_ref):
    x = x_ref[...] + _dot(a_ref[...], w_ref[:HALF_WIDTH, :]) + _dot(b_ref[...], w_ref[HALF_WIDTH:, :])
    xo_ref[...] = x
    h = _rms(x, g_ref[...])
    h_hi = h.astype(BF16)
    h_ref[...] = h_hi
    h_lo = (h - h_hi.astype(F32)).astype(BF16)
    wr = wr_ref[...]
    both = _dot(h_hi, wr)
    lane = lax.broadcasted_iota(jnp.int32, both.shape, 1).astype(F32)
    lo_part = pltpu.roll(both, LANES - N_EXPERTS, 1)
    logits = both + lo_part + _dot(h_lo, wr)
    logits = jnp.where(lane < N_EXPERTS, logits, -jnp.inf)
    m1 = jnp.max(logits, axis=-1, keepdims=True)
    i1 = jnp.min(jnp.where(logits == m1, lane, float(LANES)), axis=-1, keepdims=True)
    rest = jnp.where(lane == i1, -jnp.inf, logits)
    m2 = jnp.max(rest, axis=-1, keepdims=True)
    i2 = jnp.min(jnp.where(rest == m2, lane, float(LANES)), axis=-1, keepdims=True)
    e2 = jnp.exp(m2 - m1)
    g1 = 1.0 / (1.0 + e2)
    g2 = e2 * g1
    c_ref[...] = jnp.where(lane == i1, g1, jnp.where(lane == i2, g2, 0.0))


def _out_proj(x, a, b, w, g, wr=None, tm=512):
    t = x.shape[0]
    tm = min(tm, t)
    row = lambda i: (i, 0)
    const = lambda i: (0, 0)
    in_specs = [
        pl.BlockSpec((tm, D_MODEL), row),
        pl.BlockSpec((tm, HALF_WIDTH), row),
        pl.BlockSpec((tm, HALF_WIDTH), row),
        pl.BlockSpec((D_MODEL, D_MODEL), const),
        pl.BlockSpec((1, D_MODEL), const),
    ]
    out_specs = [pl.BlockSpec((tm, D_MODEL), row), pl.BlockSpec((tm, D_MODEL), row)]
    out_shape = [jax.ShapeDtypeStruct((t, D_MODEL), F32), jax.ShapeDtypeStruct((t, D_MODEL), BF16)]
    args = [x, a, b, w, g]
    kern = _out_proj_kernel
    if wr is not None:
        in_specs.append(pl.BlockSpec((D_MODEL, LANES), const))
        out_specs.append(pl.BlockSpec((tm, LANES), row))
        out_shape.append(jax.ShapeDtypeStruct((t, LANES), F32))
        args.append(wr)
        kern = _out_proj_router_kernel
    return pl.pallas_call(
        kern,
        grid=(t // tm,),
        in_specs=in_specs,
        out_specs=out_specs,
        out_shape=out_shape,
        compiler_params=_cparams(("parallel",)),
        name="out_proj",
    )(*args)


def _swiglu_kernel(x_ref, h_ref, wg_ref, wu_ref, wd_ref, o_ref, acc_ref):
    f = pl.program_id(1)

    @pl.when(f == 0)
    def _():
        acc_ref[...] = x_ref[...]

    h = h_ref[...]
    g = _dot(h, wg_ref[...])
    u = _dot(h, wu_ref[...])
    a = (g * jax.nn.sigmoid(g) * u).astype(BF16)
    acc_ref[...] += _dot(a, wd_ref[...])

    @pl.when(f == pl.num_programs(1) - 1)
    def _():
        o_ref[...] = acc_ref[...]


def _swiglu(x, h, wg, wu, wd, tm=1024, tf=1408):
    t = x.shape[0]
    tm = min(tm, t)
    return pl.pallas_call(
        _swiglu_kernel,
        grid=(t // tm, D_FF // tf),
        in_specs=[
            pl.BlockSpec((tm, D_MODEL), lambda i, f: (i, 0)),
            pl.BlockSpec((tm, D_MODEL), lambda i, f: (i, 0)),
            pl.BlockSpec((D_MODEL, tf), lambda i, f: (0, f)),
            pl.BlockSpec((D_MODEL, tf), lambda i, f: (0, f)),
            pl.BlockSpec((tf, D_MODEL), lambda i, f: (f, 0)),
        ],
        out_specs=pl.BlockSpec((tm, D_MODEL), lambda i, f: (i, 0)),
        out_shape=jax.ShapeDtypeStruct((t, D_MODEL), F32),
        scratch_shapes=[pltpu.VMEM((tm, D_MODEL), F32)],
        compiler_params=_cparams(("parallel", "arbitrary")),
        name="swiglu",
    )(x, h, wg, wu, wd)


MOE_TILE = 1024
MOE_CHUNK = 256
TRI_BLOCK = 512


def _route_kernel(c_ref, ltri_ref, utri_ref, posm_ref, posmt_ref):
    tm = c_ref.shape[0]
    carry = jnp.zeros((1, LANES), F32)
    carry_t = jnp.zeros((LANES, 1), F32)
    for b in range(tm // TRI_BLOCK):
        rows = slice(b * TRI_BLOCK, (b + 1) * TRI_BLOCK)
        sel = c_ref[rows, :] > 0.0
        self = jnp.where(sel, 1.0, 0.0)
        pos = _dot(ltri_ref[...], self.astype(BF16)) + carry
        posm_ref[rows, :] = jnp.where(sel, pos, -1.0)
        sel_t = self.T
        pos_t = _dot(sel_t.astype(BF16), utri_ref[...]) + carry_t
        posmt_ref[0, :, rows] = jnp.where(sel_t > 0.0, pos_t, -1.0)
        carry = carry + jnp.sum(self, axis=0, keepdims=True)
        carry_t = carry_t + jnp.sum(sel_t, axis=1, keepdims=True)


def _route(c, tm):
    t = c.shape[0]
    idx = np.arange(TRI_BLOCK)
    ltri = jnp.asarray(idx[None, :] < idx[:, None], dtype=BF16)
    utri = jnp.asarray(idx[:, None] < idx[None, :], dtype=BF16)
    return pl.pallas_call(
        _route_kernel,
        grid=(t // tm,),
        in_specs=[
            pl.BlockSpec((tm, LANES), lambda i: (i, 0)),
            pl.BlockSpec((TRI_BLOCK, TRI_BLOCK), lambda i: (0, 0)),
            pl.BlockSpec((TRI_BLOCK, TRI_BLOCK), lambda i: (0, 0)),
        ],
        out_specs=[pl.BlockSpec((tm, LANES), lambda i: (i, 0)),
                   pl.BlockSpec((1, LANES, tm), lambda i: (i, 0, 0))],
        out_shape=[jax.ShapeDtypeStruct((t, LANES), F32),
                   jax.ShapeDtypeStruct((t // tm, LANES, tm), F32)],
        compiler_params=_cparams(("parallel",)),
        name="moe_route",
    )(c, ltri, utri)


def _moe_sparse_kernel(x_ref, h_ref, c_ref, posm_ref, posmt_ref, wg_ref, wu_ref, wd_ref, o_ref, hc_ref, y_ref):
    e = pl.program_id(1)
    f = pl.program_id(2)
    last_f = pl.num_programs(2) - 1
    tm = x_ref.shape[0]

    @pl.when((e == 0) & (f == 0))
    def _():
        o_ref[...] = x_ref[...]

    pos_row = posmt_ref[0, pl.ds(e, 1), :]
    load = (jnp.max(pos_row) + 1.0).astype(jnp.int32)
    nchunks = (load + MOE_CHUNK - 1) // MOE_CHUNK

    def rows_of(k):
        return pl.ds(pl.multiple_of(k * MOE_CHUNK, MOE_CHUNK), MOE_CHUNK)

    @pl.when(f == 0)
    def _():
        def gather_chunk(k, carry):
            slot = (lax.broadcasted_iota(jnp.int32, (MOE_CHUNK, tm), 0) + k * MOE_CHUNK).astype(F32)
            onehot = jnp.where(pos_row == slot, 1.0, 0.0).astype(BF16)
            hc_ref[rows_of(k), :] = _dot(onehot, h_ref[...]).astype(BF16)
            return carry

        lax.fori_loop(0, nchunks, gather_chunk, 0)

    def expert_chunk(k, carry):
        hk = hc_ref[rows_of(k), :]
        g = _dot(hk, wg_ref[0])
        u = _dot(hk, wu_ref[0])
        y = _dot((g * jax.nn.sigmoid(g) * u).astype(BF16), wd_ref[0])

        @pl.when(f == 0)
        def _():
            y_ref[rows_of(k), :] = y

        @pl.when(f > 0)
        def _():
            y_ref[rows_of(k), :] += y

        return carry

    lax.fori_loop(0, nchunks, expert_chunk, 0)

    @pl.when(f == last_f)
    def _():
        lane = lax.broadcasted_iota(jnp.int32, (tm, LANES), 1)
        pos = jnp.sum(jnp.where(lane == e, posm_ref[...], 0.0), axis=-1, keepdims=True)
        gate = jnp.sum(jnp.where(lane == e, c_ref[...], 0.0), axis=-1, keepdims=True)

        def scatter_chunk(k, carry):
            slot = (lax.broadcasted_iota(jnp.int32, (tm, MOE_CHUNK), 1) + k * MOE_CHUNK).astype(F32)
            onehot = jnp.where(pos == slot, 1.0, 0.0).astype(BF16)
            o_ref[...] += gate * _dot(onehot, y_ref[rows_of(k), :].astype(BF16))
            return carry

        lax.fori_loop(0, nchunks, scatter_chunk, 0)


def _moe_sparse(x, h, c, wg, wu, wd, tm=MOE_TILE, tf=1408):
    t = x.shape[0]
    tm = min(tm, t)
    posm, posmt = _route(c, tm)
    return pl.pallas_call(
        _moe_sparse_kernel,
        grid=(t // tm, N_EXPERTS, D_FF // tf),
        in_specs=[
            pl.BlockSpec((tm, D_MODEL), lambda i, e, f: (i, 0)),
            pl.BlockSpec((tm, D_MODEL), lambda i, e, f: (i, 0)),
            pl.BlockSpec((tm, LANES), lambda i, e, f: (i, 0)),
            pl.BlockSpec((tm, LANES), lambda i, e, f: (i, 0)),
            pl.BlockSpec((1, LANES, tm), lambda i, e, f: (i, 0, 0)),
            pl.BlockSpec((1, D_MODEL, tf), lambda i, e, f: (e, 0, f)),
            pl.BlockSpec((1, D_MODEL, tf), lambda i, e, f: (e, 0, f)),
            pl.BlockSpec((1, tf, D_MODEL), lambda i, e, f: (e, f, 0)),
        ],
        out_specs=pl.BlockSpec((tm, D_MODEL), lambda i, e, f: (i, 0)),
        out_shape=jax.ShapeDtypeStruct((t, D_MODEL), F32),
        scratch_shapes=[pltpu.VMEM((tm, D_MODEL), BF16), pltpu.VMEM((tm, D_MODEL), F32)],
        compiler_params=_cparams(("parallel", "arbitrary", "arbitrary")),
        name="moe_sparse",
    )(x, h, c, posm, posmt, wg, wu, wd)


def _rope_tables(seq):
    inv = 1.0 / (ROPE_THETA ** (jnp.arange(0, DIFF_DH, 2, dtype=F32) / DIFF_DH))
    ang = jnp.arange(seq, dtype=F32)[:, None] * inv[None, :]
    ang = jnp.concatenate([ang, ang], axis=-1)
    cos, sin = jnp.cos(ang), jnp.sin(ang)
    sign = jnp.where(jnp.arange(DIFF_DH) < DIFF_DH // 2, -1.0, 1.0).astype(F32)
    return jnp.tile(cos, (1, 2)), jnp.tile(sin * sign, (1, 2))


def _block_diag_ones():
    idx = np.arange(256) // 64
    return jnp.asarray(idx[:, None] == idx[None, :], dtype=BF16)


def _prepare_layer(l, p):
    lam_init = 0.8 - 0.6 * math.exp(-0.3 * l)
    lq = p["lambda_q"][l].astype(F32)
    lk = p["lambda_k"][l].astype(F32)
    lam = jnp.exp(jnp.sum(lq[0] * lk[0])) - jnp.exp(jnp.sum(lq[1] * lk[1])) + lam_init
    scale = DIFF_DH ** -0.5 * math.log2(math.e)
    tile8 = lambda g: jnp.tile(g.astype(F32), HALF_WIDTH // g.shape[0])
    zeros = jnp.zeros((HALF_WIDTH,), F32)
    dq_gain = p["diff_q_norm"][l].astype(F32) * scale
    dk_gain = p["diff_k_norm"][l].astype(F32)
    gains = jnp.stack([
        tile8(dq_gain), tile8(dk_gain), zeros,
        tile8(p["na_q_norm"][l]) * scale, tile8(p["na_k_norm"][l]), zeros, zeros, zeros])
    score_bound = SCORE_BOUND_MARGIN * DIFF_DH * jnp.max(jnp.abs(dq_gain)) * jnp.max(jnp.abs(dk_gain))
    lay = dict(
        scal=jnp.stack([lam, score_bound]).astype(F32),
        out_scale=1.0 - lam_init,
        attn_norm=p["attn_norm"][l].reshape(1, D_MODEL).astype(F32),
        w_in=p["w_in"][l].astype(BF16),
        gains=gains,
        g_sub=p["diff_sub_norm"][l].reshape(1, LANES).astype(F32),
        na_bias=_na_bias_table(p["na_rpb"][l]),
        g_na_out=jnp.tile(p["na_out_norm"][l].astype(F32), 2).reshape(1, LANES),
        w_out=p["w_out"][l].astype(BF16),
        ffn_norm=p["ffn_norm"][l].reshape(1, D_MODEL).astype(F32),
    )
    i = l // 2
    if l % 2 == 0:
        lay.update(wg=p["dense_w_gate"][i].astype(BF16), wu=p["dense_w_up"][i].astype(BF16),
                   wd=p["dense_w_down"][i].astype(BF16))
    else:
        wr = p["moe_router"][i].astype(F32)
        wr_hi = wr.astype(BF16)
        wr_lo = (wr - wr_hi.astype(F32)).astype(BF16)
        wr2 = jnp.zeros((D_MODEL, LANES), BF16)
        wr2 = wr2.at[:, :N_EXPERTS].set(wr_hi).at[:, N_EXPERTS:2 * N_EXPERTS].set(wr_lo)
        lay.update(wr=wr2, wg=p["moe_w_gate"][i].astype(BF16), wu=p["moe_w_up"][i].astype(BF16),
                   wd=p["moe_w_down"][i].astype(BF16))
    return lay


def _run_trunk(x, layers, bd):
    b, s, _ = x.shape
    t = b * s
    cos, sin = _rope_tables(s)
    x = x.reshape(t, D_MODEL)
    for l, lay in enumerate(layers):
        proj, vt = _in_proj(x, lay["attn_norm"], lay["w_in"], lay["gains"], cos, sin, bd, s)
        proj3 = proj.reshape(b, s, IN_WIDTH)
        a_out = _diff_attn(proj3, vt, lay["scal"], lay["g_sub"], lay["out_scale"])
        b_out = _na_attn(proj3, lay["na_bias"], lay["g_na_out"])
        a_out = a_out.reshape(t, HALF_WIDTH)
        b_out = b_out.reshape(t, HALF_WIDTH)
        if l % 2 == 0:
            x, h = _out_proj(x, a_out, b_out, lay["w_out"], lay["ffn_norm"])
            x = _swiglu(x, h, lay["wg"], lay["wu"], lay["wd"])
        else:
            x, h, c = _out_proj(x, a_out, b_out, lay["w_out"], lay["ffn_norm"], lay["wr"])
            x = _moe_sparse(x, h, c, lay["wg"], lay["wu"], lay["wd"])
    return x.reshape(b, s, D_MODEL)


def kernel(x_prompt, x_sample, attn_norm, w_in, w_out, diff_q_norm, diff_k_norm, lambda_q, lambda_k, diff_sub_norm, na_q_norm, na_k_norm, na_rpb, na_out_norm, ffn_norm, dense_w_gate, dense_w_up, dense_w_down, moe_router, moe_w_gate, moe_w_up, moe_w_down):
    p = dict(attn_norm=attn_norm, w_in=w_in, w_out=w_out, diff_q_norm=diff_q_norm, diff_k_norm=diff_k_norm,
             lambda_q=lambda_q, lambda_k=lambda_k, diff_sub_norm=diff_sub_norm, na_q_norm=na_q_norm,
             na_k_norm=na_k_norm, na_rpb=na_rpb, na_out_norm=na_out_norm, ffn_norm=ffn_norm,
             dense_w_gate=dense_w_gate, dense_w_up=dense_w_up, dense_w_down=dense_w_down,
             moe_router=moe_router, moe_w_gate=moe_w_gate, moe_w_up=moe_w_up, moe_w_down=moe_w_down)
    layers = [_prepare_layer(l, p) for l in range(DEPTH)]
    bd = _block_diag_ones()
    return (_run_trunk(x_prompt, layers, bd), _run_trunk(x_sample, layers, bd))
```

```python
import functools
import math

import jax
import jax.numpy as jnp
import numpy as np
from jax import lax
from jax.experimental import pallas as pl
from jax.experimental.pallas import tpu as pltpu

D_MODEL = 1024
DEPTH = 4
GRID_W = 64
DIFF_HEADS = 4
DIFF_DH = 64
NA_HEADS = 8
NA_DH = 64
NA_KH = 8
NA_KW = 16
HALF_WIDTH = 512
IN_WIDTH = 6 * HALF_WIDTH
D_FF = 2816
N_EXPERTS = 8
ROPE_THETA = 10000.0
EPS = 1e-6
MASK_VALUE = -1e30
SCORE_BOUND_MARGIN = 1.02
MAX_SAFE_SCORE_BOUND = 60.0

LANES = 128
VMEM_LIMIT = 56 * 1024 * 1024

BF16 = jnp.bfloat16
F32 = jnp.float32


def _cparams(sem):
    return pltpu.CompilerParams(dimension_semantics=sem, vmem_limit_bytes=VMEM_LIMIT)


def _dot(a, b):
    return jnp.dot(a, b, preferred_element_type=F32)


def _dot_nt(a, b):
    return lax.dot_general(a, b, (((1,), (1,)), ((), ())), preferred_element_type=F32)


def _rms(x, g):
    return x * lax.rsqrt(jnp.mean(x * x, axis=-1, keepdims=True) + EPS) * g


def _group_norm64(y, gain, bd_ref):
    parts = []
    for c in range(HALF_WIDTH // 256):
        yc = y[:, c * 256:(c + 1) * 256]
        ss = _dot((yc * yc).astype(BF16), bd_ref[...])
        parts.append(yc * lax.rsqrt(ss * (1.0 / 64) + EPS))
    return jnp.concatenate(parts, axis=1) * gain


def _rope64(y, cos, sin_signed):
    n = y.shape[1]
    lane = lax.broadcasted_iota(jnp.int32, y.shape, 1)
    first_half = (lane % 64) < 32
    rot = jnp.where(first_half, pltpu.roll(y, n - 32, 1), pltpu.roll(y, 32, 1))
    return y * cos + rot * sin_signed


DV_SECTION = 2


def _in_proj_kernel(x_ref, g_ref, w_ref, gains_ref, cos_ref, sin_ref, bd_ref, o_ref, vt_ref):
    x = x_ref[...]
    h = _rms(x, g_ref[...]).astype(BF16)
    cos = jnp.concatenate([cos_ref[...]] * (HALF_WIDTH // LANES), axis=1)
    sin = jnp.concatenate([sin_ref[...]] * (HALF_WIDTH // LANES), axis=1)
    for sec in range(6):
        y = _dot(h, w_ref[:, sec * HALF_WIDTH:(sec + 1) * HALF_WIDTH])
        if sec in (0, 1, 3, 4):
            y = _group_norm64(y, gains_ref[sec:sec + 1, :], bd_ref)
        if sec in (0, 1):
            y = _rope64(y, cos, sin)
        o_ref[:, sec * HALF_WIDTH:(sec + 1) * HALF_WIDTH] = y.astype(BF16)
        if sec == DV_SECTION:
            vt_ref[0] = y.T.astype(BF16)


def _in_proj(x, g, w, gains, cos, sin, bd, seq, tm=512):
    t = x.shape[0]
    tm = min(tm, seq)
    nseq = seq // tm
    return pl.pallas_call(
        _in_proj_kernel,
        grid=(t // tm,),
        in_specs=[
            pl.BlockSpec((tm, D_MODEL), lambda i: (i, 0)),
            pl.BlockSpec((1, D_MODEL), lambda i: (0, 0)),
            pl.BlockSpec((D_MODEL, IN_WIDTH), lambda i: (0, 0)),
            pl.BlockSpec((8, HALF_WIDTH), lambda i: (0, 0)),
            pl.BlockSpec((tm, LANES), lambda i: (i % nseq, 0)),
            pl.BlockSpec((tm, LANES), lambda i: (i % nseq, 0)),
            pl.BlockSpec((256, 256), lambda i: (0, 0)),
        ],
        out_specs=[pl.BlockSpec((tm, IN_WIDTH), lambda i: (i, 0)),
                   pl.BlockSpec((1, HALF_WIDTH, tm), lambda i: (i // nseq, 0, i % nseq))],
        out_shape=[jax.ShapeDtypeStruct((t, IN_WIDTH), BF16),
                   jax.ShapeDtypeStruct((t // seq, HALF_WIDTH, seq), BF16)],
        compiler_params=_cparams(("parallel",)),
        name="in_proj",
    )(x, g, w, gains, cos, sin, bd)


def _diff_attn_kernel(scal_ref, q_ref, k_ref, v_ref, g_ref, o_ref, acc0_ref, acc1_ref, l0_ref, l1_ref, *,
                      tk, out_scale, bounded):
    q = q_ref[0]
    tq = q.shape[0]
    lane = lax.broadcasted_iota(jnp.int32, q.shape, 1)
    qm = (jnp.where(lane < DIFF_DH, q, jnp.zeros_like(q)), jnp.where(lane >= DIFF_DH, q, jnp.zeros_like(q)))
    accs = (acc0_ref, acc1_ref)
    ls = (l0_ref, l1_ref)
    for r in accs + ls:
        r[...] = jnp.zeros_like(r)
    nk = k_ref.shape[1] // tk

    def lane_fold(p):
        out = p[:, :LANES]
        for c in range(1, tk // LANES):
            out = out + p[:, c * LANES:(c + 1) * LANES]
        return out

    def load(j):
        start = pl.multiple_of(j * tk, tk)
        return k_ref[0, pl.ds(start, tk), :], v_ref[0, pl.ds(start, tk), :]

    if bounded:
        bound = scal_ref[1]

        def body(j, carry):
            k, v = load(j)
            for a in range(2):
                p = jnp.exp2(_dot_nt(qm[a], k) - bound)
                ls[a][...] += lane_fold(p)
                accs[a][...] += _dot(p.astype(BF16), v)
            return carry

        lax.fori_loop(0, nk, body, 0)
    else:
        def body(j, ms):
            k, v = load(j)
            new_ms = []
            for a in range(2):
                s = _dot_nt(qm[a], k)
                m_new = jnp.maximum(ms[a], jnp.max(s, axis=-1, keepdims=True))
                alpha = jnp.exp2(ms[a] - m_new)
                p = jnp.exp2(s - m_new)
                ls[a][...] = alpha * ls[a][...] + lane_fold(p)
                accs[a][...] = alpha * accs[a][...] + _dot(p.astype(BF16), v)
                new_ms.append(m_new)
            return tuple(new_ms)

        neg = jnp.full((tq, 1), -jnp.inf, F32)
        lax.fori_loop(0, nk, body, (neg, neg))

    lam = scal_ref[0]
    l0 = jnp.sum(l0_ref[...], axis=-1, keepdims=True)
    l1 = jnp.sum(l1_ref[...], axis=-1, keepdims=True)
    o = acc0_ref[...] / l0 - lam * (acc1_ref[...] / l1)
    o = _rms(o, g_ref[...]) * out_scale
    o_ref[0] = o.astype(BF16)


def _diff_attn_t_kernel(scal_ref, q_ref, k_ref, vt_ref, g_ref, o_ref, acc0_ref, acc1_ref, l0_ref, l1_ref, *,
                        tk, out_scale):
    q = q_ref[0]
    lane = lax.broadcasted_iota(jnp.int32, q.shape, 1)
    qm = (jnp.where(lane < DIFF_DH, q, jnp.zeros_like(q)), jnp.where(lane >= DIFF_DH, q, jnp.zeros_like(q)))
    accs = (acc0_ref, acc1_ref)
    ls = (l0_ref, l1_ref)
    for r in accs + ls:
        r[...] = jnp.zeros_like(r)
    nk = k_ref.shape[1] // tk
    tq = q.shape[0]
    bound = scal_ref[1]

    def body(j, carry):
        start = pl.multiple_of(j * tk, tk)
        k = k_ref[0, pl.ds(start, tk), :]
        vt = vt_ref[0, :, pl.ds(start, tk)]
        for a in range(2):
            p = jnp.exp2(_dot_nt(k, qm[a]) - bound)
            ls[a][...] += jnp.sum(p.reshape(tk // 8, 8, tq), axis=0)
            accs[a][...] += _dot(vt, p.astype(BF16))
        return carry

    lax.fori_loop(0, nk, body, 0)
    lam = scal_ref[0]
    l0 = jnp.sum(l0_ref[...], axis=0, keepdims=True)
    l1 = jnp.sum(l1_ref[...], axis=0, keepdims=True)
    ot = acc0_ref[...] / l0 - lam * (acc1_ref[...] / l1)
    ot = ot * lax.rsqrt(jnp.mean(ot * ot, axis=0, keepdims=True) + EPS)
    o_ref[0] = (ot.T * g_ref[...] * out_scale).astype(BF16)


def _diff_attn_t_call(proj, vt, scal, g_sub, out_scale, tq=1024, tk=2048):
    b, s, _ = proj.shape
    tq, tk = min(tq, s), min(tk, s)
    kern = functools.partial(_diff_attn_t_kernel, tk=tk, out_scale=out_scale)
    return pl.pallas_call(
        kern,
        grid=(b, DIFF_HEADS, s // tq),
        in_specs=[
            pl.BlockSpec(memory_space=pltpu.SMEM),
            pl.BlockSpec((1, tq, LANES), lambda bi, h, i: (bi, i, h)),
            pl.BlockSpec((1, s, LANES), lambda bi, h, i: (bi, 0, DIFF_HEADS + h)),
            pl.BlockSpec((1, LANES, s), lambda bi, h, i: (bi, h, 0)),
            pl.BlockSpec((1, LANES), lambda bi, h, i: (0, 0)),
        ],
        out_specs=pl.BlockSpec((1, tq, LANES), lambda bi, h, i: (bi, i, h)),
        out_shape=jax.ShapeDtypeStruct((b, s, HALF_WIDTH), BF16),
        scratch_shapes=[pltpu.VMEM((LANES, tq), F32)] * 2 + [pltpu.VMEM((8, tq), F32)] * 2,
        compiler_params=_cparams(("parallel", "parallel", "arbitrary")),
        name="diff_attn_t",
    )(scal, proj, proj, vt, g_sub)


def _diff_attn(proj, vt, scal, g_sub, out_scale):
    return lax.cond(
        scal[1] <= MAX_SAFE_SCORE_BOUND,
        lambda: _diff_attn_t_call(proj, vt, scal, g_sub, out_scale),
        lambda: _diff_attn_call(proj, scal, g_sub, out_scale, False))


def _diff_attn_call(proj, scal, g_sub, out_scale, bounded, tq=1024, tk=1024):
    b, s, _ = proj.shape
    tq, tk = min(tq, s), min(tk, s)
    kern = functools.partial(_diff_attn_kernel, tk=tk, out_scale=out_scale, bounded=bounded)
    return pl.pallas_call(
        kern,
        grid=(b, DIFF_HEADS, s // tq),
        in_specs=[
            pl.BlockSpec(memory_space=pltpu.SMEM),
            pl.BlockSpec((1, tq, LANES), lambda bi, h, i: (bi, i, h)),
            pl.BlockSpec((1, s, LANES), lambda bi, h, i: (bi, 0, DIFF_HEADS + h)),
            pl.BlockSpec((1, s, LANES), lambda bi, h, i: (bi, 0, 2 * DIFF_HEADS + h)),
            pl.BlockSpec((1, LANES), lambda bi, h, i: (0, 0)),
        ],
        out_specs=pl.BlockSpec((1, tq, LANES), lambda bi, h, i: (bi, i, h)),
        out_shape=jax.ShapeDtypeStruct((b, s, HALF_WIDTH), BF16),
        scratch_shapes=[pltpu.VMEM((tq, LANES), F32)] * 4,
        compiler_params=_cparams(("parallel", "parallel", "arbitrary")),
        name="diff_attn_bounded" if bounded else "diff_attn_online",
    )(scal, proj, proj, proj, g_sub)


NA_ROWS_PER_STEP = 8
NA_BLOCK = NA_ROWS_PER_STEP * GRID_W


def _na_attn_kernel(q_ref, kp_ref, kc_ref, kn_ref, vp_ref, vc_ref, vn_ref, bias_ref, g_ref, o_ref,
                    kbuf, vbuf, *, rows):
    i = pl.program_id(1)
    for n, (kr, vr) in enumerate(((kp_ref, vp_ref), (kc_ref, vc_ref), (kn_ref, vn_ref))):
        kbuf[n * NA_BLOCK:(n + 1) * NA_BLOCK, :] = kr[0]
        vbuf[n * NA_BLOCK:(n + 1) * NA_BLOCK, :] = vr[0]
    lane = lax.broadcasted_iota(jnp.int32, (GRID_W, LANES), 1)
    halves = (lane < NA_DH, lane >= NA_DH)

    def row_body(qr, carry):
        r = i * NA_ROWS_PER_STEP + qr
        r0 = jnp.clip(r - NA_KH // 2, 0, rows - NA_KH)
        start = pl.multiple_of((r0 - (i - 1) * NA_ROWS_PER_STEP) * GRID_W, GRID_W)
        rp = r - r0
        qrow = pl.multiple_of(qr * GRID_W, GRID_W)
        npairs = NA_HEADS // 2
        col = [slice(p * LANES, (p + 1) * LANES) for p in range(npairs)]
        scores = []
        for p in range(npairs):
            qp = q_ref[0, pl.ds(qrow, GRID_W), col[p]]
            kp = kbuf[pl.ds(start, NA_KH * GRID_W), col[p]]
            q2 = jnp.concatenate([jnp.where(h, qp, jnp.zeros_like(qp)) for h in halves], axis=0)
            scores.append(_dot_nt(q2, kp) + bias_ref[p, rp])
        probs = []
        for s in scores:
            m = jnp.max(s, axis=-1, keepdims=True)
            pe = jnp.exp2(s - m)
            probs.append((pe.astype(BF16), jnp.sum(pe, axis=-1, keepdims=True)))
        for p in range(npairs):
            vp = vbuf[pl.ds(start, NA_KH * GRID_W), col[p]]
            pe, l = probs[p]
            o_both = _dot(pe, vp) / l
            o = jnp.where(halves[0], o_both[:GRID_W], o_both[GRID_W:])
            o2 = o * o
            ss = [jnp.sum(jnp.where(h, o2, 0.0), axis=-1, keepdims=True) for h in halves]
            inv = jnp.where(halves[0], lax.rsqrt(ss[0] * (1.0 / NA_DH) + EPS),
                            lax.rsqrt(ss[1] * (1.0 / NA_DH) + EPS))
            o_ref[0, pl.ds(qrow, GRID_W), col[p]] = (o * inv * g_ref[...]).astype(BF16)
        return carry

    lax.fori_loop(0, NA_ROWS_PER_STEP, row_body, 0, unroll=4)


def _na_attn(proj, bias, g_out):
    b, s, _ = proj.shape
    rows = s // GRID_W
    nb = rows // NA_ROWS_PER_STEP
    qcol, kcol, vcol = 3, 4, 5
    blk = (1, NA_BLOCK, HALF_WIDTH)

    def prev(bi, i, c):
        return (bi, jnp.maximum(i - 1, 0), c)

    def nxt(bi, i, c):
        return (bi, jnp.minimum(i + 1, nb - 1), c)

    return pl.pallas_call(
        functools.partial(_na_attn_kernel, rows=rows),
        grid=(b, nb),
        in_specs=[
            pl.BlockSpec(blk, lambda bi, i: (bi, i, qcol)),
            pl.BlockSpec(blk, lambda bi, i: prev(bi, i, kcol)),
            pl.BlockSpec(blk, lambda bi, i: (bi, i, kcol)),
            pl.BlockSpec(blk, lambda bi, i: nxt(bi, i, kcol)),
            pl.BlockSpec(blk, lambda bi, i: prev(bi, i, vcol)),
            pl.BlockSpec(blk, lambda bi, i: (bi, i, vcol)),
            pl.BlockSpec(blk, lambda bi, i: nxt(bi, i, vcol)),
            pl.BlockSpec((NA_HEADS // 2, NA_KH, 2 * GRID_W, NA_KH * GRID_W), lambda bi, i: (0, 0, 0, 0)),
            pl.BlockSpec((1, LANES), lambda bi, i: (0, 0)),
        ],
        out_specs=pl.BlockSpec(blk, lambda bi, i: (bi, i, 0)),
        out_shape=jax.ShapeDtypeStruct((b, s, HALF_WIDTH), BF16),
        scratch_shapes=[pltpu.VMEM((3 * NA_BLOCK, HALF_WIDTH), BF16),
                        pltpu.VMEM((3 * NA_BLOCK, HALF_WIDTH), BF16)],
        compiler_params=_cparams(("parallel", "arbitrary")),
        name="na_attn",
    )(proj, proj, proj, proj, proj, proj, proj, bias, g_out)


def _na_bias_table(rpb):
    qc = np.arange(GRID_W)
    kc = np.arange(GRID_W)
    win = np.clip(qc - NA_KW // 2, 0, GRID_W - NA_KW)
    valid = (kc[None, :] >= win[:, None]) & (kc[None, :] < win[:, None] + NA_KW)
    dc = np.clip(kc[None, :] - qc[:, None] + NA_KW - 1, 0, 2 * NA_KW - 2)
    rp = np.arange(NA_KH)
    j = np.arange(NA_KH)
    dr = j[None, :] - rp[:, None] + NA_KH - 1
    t = (rpb.astype(F32) * math.log2(math.e))[:, dr][:, :, :, dc]
    t = jnp.where(jnp.asarray(valid)[None, None, None], t, MASK_VALUE)
    t = t.reshape(NA_HEADS // 2, 2, NA_KH, NA_KH, GRID_W, GRID_W)
    t = t.transpose(0, 2, 1, 4, 3, 5)
    return t.reshape(NA_HEADS // 2, NA_KH, 2 * GRID_W, NA_KH * GRID_W)


def _out_proj_kernel(x_ref, a_ref, b_ref, w_ref, g_ref, xo_ref, h_ref):
    x = x_ref[...] + _dot(a_ref[...], w_ref[:HALF_WIDTH, :]) + _dot(b_ref[...], w_ref[HALF_WIDTH:, :])
    xo_ref[...] = x
    h_ref[...] = _rms(x, g_ref[...]).astype(BF16)


def _out_proj_router_kernel(x_ref, a_ref, b_ref, w_ref, g_ref, wr_ref, xo_ref, h_ref, c_ref):
    x = x_ref[...] + _dot(a_ref[...], w_ref[:HALF_WIDTH, :]) + _dot(b_ref[...], w_ref[HALF_WIDTH:, :])
    xo_ref[...] = x
    h = _rms(x, g_ref[...])
    h_hi = h.astype(BF16)
    h_ref[...] = h_hi
    h_lo = (h - h_hi.astype(F32)).astype(BF16)
    wr = wr_ref[...]
    both = _dot(h_hi, wr)
    lane = lax.broadcasted_iota(jnp.int32, both.shape, 1).astype(F32)
    lo_part = pltpu.roll(both, LANES - N_EXPERTS, 1)
    logits = both + lo_part + _dot(h_lo, wr)
    logits = jnp.where(lane < N_EXPERTS, logits, -jnp.inf)
    m1 = jnp.max(logits, axis=-1, keepdims=True)
    i1 = jnp.min(jnp.where(logits == m1, lane, float(LANES)), axis=-1, keepdims=True)
    rest = jnp.where(lane == i1, -jnp.inf, logits)
    m2 = jnp.max(rest, axis=-1, keepdims=True)
    i2 = jnp.min(jnp.where(rest == m2, lane, float(LANES)), axis=-1, keepdims=True)
    e2 = jnp.exp(m2 - m1)
    g1 = 1.0 / (1.0 + e2)
    g2 = e2 * g1
    c_ref[...] = jnp.where(lane == i1, g1, jnp.where(lane == i2, g2, 0.0))


def _out_proj(x, a, b, w, g, wr=None, tm=512):
    t = x.shape[0]
    tm = min(tm, t)
    row = lambda i: (i, 0)
    const = lambda i: (0, 0)
    in_specs = [
        pl.BlockSpec((tm, D_MODEL), row),
        pl.BlockSpec((tm, HALF_WIDTH), row),
        pl.BlockSpec((tm, HALF_WIDTH), row),
        pl.BlockSpec((D_MODEL, D_MODEL), const),
        pl.BlockSpec((1, D_MODEL), const),
    ]
    out_specs = [pl.BlockSpec((tm, D_MODEL), row), pl.BlockSpec((tm, D_MODEL), row)]
    out_shape = [jax.ShapeDtypeStruct((t, D_MODEL), F32), jax.ShapeDtypeStruct((t, D_MODEL), BF16)]
    args = [x, a, b, w, g]
    kern = _out_proj_kernel
    if wr is not None:
        in_specs.append(pl.BlockSpec((D_MODEL, LANES), const))
        out_specs.append(pl.BlockSpec((tm, LANES), row))
        out_shape.append(jax.ShapeDtypeStruct((t, LANES), F32))
        args.append(wr)
        kern = _out_proj_router_kernel
    return pl.pallas_call(
        kern,
        grid=(t // tm,),
        in_specs=in_specs,
        out_specs=out_specs,
        out_shape=out_shape,
        compiler_params=_cparams(("parallel",)),
        name="out_proj",
    )(*args)


def _swiglu_kernel(x_ref, h_ref, wg_ref, wu_ref, wd_ref, o_ref, acc_ref):
    f = pl.program_id(1)

    @pl.when(f == 0)
    def _():
        acc_ref[...] = x_ref[...]

    h = h_ref[...]
    g = _dot(h, wg_ref[...])
    u = _dot(h, wu_ref[...])
    a = (g * jax.nn.sigmoid(g) * u).astype(BF16)
    acc_ref[...] += _dot(a, wd_ref[...])

    @pl.when(f == pl.num_programs(1) - 1)
    def _():
        o_ref[...] = acc_ref[...]


def _swiglu(x, h, wg, wu, wd, tm=1024, tf=1408):
    t = x.shape[0]
    tm = min(tm, t)
    return pl.pallas_call(
        _swiglu_kernel,
        grid=(t // tm, D_FF // tf),
        in_specs=[
            pl.BlockSpec((tm, D_MODEL), lambda i, f: (i, 0)),
            pl.BlockSpec((tm, D_MODEL), lambda i, f: (i, 0)),
            pl.BlockSpec((D_MODEL, tf), lambda i, f: (0, f)),
            pl.BlockSpec((D_MODEL, tf), lambda i, f: (0, f)),
            pl.BlockSpec((tf, D_MODEL), lambda i, f: (f, 0)),
        ],
        out_specs=pl.BlockSpec((tm, D_MODEL), lambda i, f: (i, 0)),
        out_shape=jax.ShapeDtypeStruct((t, D_MODEL), F32),
        scratch_shapes=[pltpu.VMEM((tm, D_MODEL), F32)],
        compiler_params=_cparams(("parallel", "arbitrary")),
        name="swiglu",
    )(x, h, wg, wu, wd)


MOE_TILE = 1024
MOE_CHUNK = 256
MOE_TAIL = 128
TRI_BLOCK = 512


def _route_kernel(c_ref, ltri_ref, utri_ref, posm_ref, posmt_ref):
    tm = c_ref.shape[0]
    carry = jnp.zeros((1, LANES), F32)
    carry_t = jnp.zeros((LANES, 1), F32)
    for b in range(tm // TRI_BLOCK):
        rows = slice(b * TRI_BLOCK, (b + 1) * TRI_BLOCK)
        sel = c_ref[rows, :] > 0.0
        self = jnp.where(sel, 1.0, 0.0)
        pos = _dot(ltri_ref[...], self.astype(BF16)) + carry
        posm_ref[rows, :] = jnp.where(sel, pos, -1.0)
        sel_t = self.T
        pos_t = _dot(sel_t.astype(BF16), utri_ref[...]) + carry_t
        posmt_ref[0, :, rows] = jnp.where(sel_t > 0.0, pos_t, -1.0)
        carry = carry + jnp.sum(self, axis=0, keepdims=True)
        carry_t = carry_t + jnp.sum(sel_t, axis=1, keepdims=True)


def _route(c, tm):
    t = c.shape[0]
    idx = np.arange(TRI_BLOCK)
    ltri = jnp.asarray(idx[None, :] < idx[:, None], dtype=BF16)
    utri = jnp.asarray(idx[:, None] < idx[None, :], dtype=BF16)
    return pl.pallas_call(
        _route_kernel,
        grid=(t // tm,),
        in_specs=[
            pl.BlockSpec((tm, LANES), lambda i: (i, 0)),
            pl.BlockSpec((TRI_BLOCK, TRI_BLOCK), lambda i: (0, 0)),
            pl.BlockSpec((TRI_BLOCK, TRI_BLOCK), lambda i: (0, 0)),
        ],
        out_specs=[pl.BlockSpec((tm, LANES), lambda i: (i, 0)),
                   pl.BlockSpec((1, LANES, tm), lambda i: (i, 0, 0))],
        out_shape=[jax.ShapeDtypeStruct((t, LANES), F32),
                   jax.ShapeDtypeStruct((t // tm, LANES, tm), F32)],
        compiler_params=_cparams(("parallel",)),
        name="moe_route",
    )(c, ltri, utri)


def _moe_sparse_kernel(x_ref, h_ref, c_ref, posm_ref, posmt_ref, wg_ref, wu_ref, wd_ref, o_ref, hc_ref, y_ref):
    e = pl.program_id(1)
    f = pl.program_id(2)
    last_f = pl.num_programs(2) - 1
    tm = x_ref.shape[0]

    @pl.when((e == 0) & (f == 0))
    def _():
        o_ref[...] = x_ref[...]

    pos_row = posmt_ref[0, pl.ds(e, 1), :]
    load = (jnp.max(pos_row) + 1.0).astype(jnp.int32)
    rem = load % MOE_CHUNK
    nfull = load // MOE_CHUNK + (rem > MOE_TAIL).astype(jnp.int32)
    has_tail = (rem > 0) & (rem <= MOE_TAIL)
    tail_off = nfull * MOE_CHUNK

    def rows_of(off, size):
        return pl.ds(pl.multiple_of(off, MOE_TAIL), size)

    def for_each_chunk(fn):
        def full(k, carry):
            fn(k * MOE_CHUNK, MOE_CHUNK)
            return carry

        lax.fori_loop(0, nfull, full, 0)

        @pl.when(has_tail)
        def _():
            fn(tail_off, MOE_TAIL)

    def gather(off, size):
        slot = (lax.broadcasted_iota(jnp.int32, (size, tm), 0) + off).astype(F32)
        onehot = jnp.where(pos_row == slot, 1.0, 0.0).astype(BF16)
        hc_ref[rows_of(off, size), :] = _dot(onehot, h_ref[...]).astype(BF16)

    def expert(off, size):
        hk = hc_ref[rows_of(off, size), :]
        g = _dot(hk, wg_ref[0])
        u = _dot(hk, wu_ref[0])
        y = _dot((g * jax.nn.sigmoid(g) * u).astype(BF16), wd_ref[0])

        @pl.when(f == 0)
        def _():
            y_ref[rows_of(off, size), :] = y

        @pl.when(f > 0)
        def _():
            y_ref[rows_of(off, size), :] += y

    @pl.when(f == 0)
    def _():
        for_each_chunk(gather)

    for_each_chunk(expert)

    @pl.when(f == last_f)
    def _():
        lane = lax.broadcasted_iota(jnp.int32, (tm, LANES), 1)
        pos = jnp.sum(jnp.where(lane == e, posm_ref[...], 0.0), axis=-1, keepdims=True)
        gate = jnp.sum(jnp.where(lane == e, c_ref[...], 0.0), axis=-1, keepdims=True)

        def scatter(off, size):
            slot = (lax.broadcasted_iota(jnp.int32, (tm, size), 1) + off).astype(F32)
            onehot = jnp.where(pos == slot, 1.0, 0.0).astype(BF16)
            o_ref[...] += gate * _dot(onehot, y_ref[rows_of(off, size), :].astype(BF16))

        for_each_chunk(scatter)


def _moe_sparse(x, h, c, wg, wu, wd, tm=MOE_TILE, tf=1408):
    t = x.shape[0]
    tm = min(tm, t)
    posm, posmt = _route(c, tm)
    return pl.pallas_call(
        _moe_sparse_kernel,
        grid=(t // tm, N_EXPERTS, D_FF // tf),
        in_specs=[
            pl.BlockSpec((tm, D_MODEL), lambda i, e, f: (i, 0)),
            pl.BlockSpec((tm, D_MODEL), lambda i, e, f: (i, 0)),
            pl.BlockSpec((tm, LANES), lambda i, e, f: (i, 0)),
            pl.BlockSpec((tm, LANES), lambda i, e, f: (i, 0)),
            pl.BlockSpec((1, LANES, tm), lambda i, e, f: (i, 0, 0)),
            pl.BlockSpec((1, D_MODEL, tf), lambda i, e, f: (e, 0, f)),
            pl.BlockSpec((1, D_MODEL, tf), lambda i, e, f: (e, 0, f)),
            pl.BlockSpec((1, tf, D_MODEL), lambda i, e, f: (e, f, 0)),
        ],
        out_specs=pl.BlockSpec((tm, D_MODEL), lambda i, e, f: (i, 0)),
        out_shape=jax.ShapeDtypeStruct((t, D_MODEL), F32),
        scratch_shapes=[pltpu.VMEM((tm, D_MODEL), BF16), pltpu.VMEM((tm, D_MODEL), F32)],
        compiler_params=_cparams(("parallel", "arbitrary", "arbitrary")),
        name="moe_sparse",
    )(x, h, c, posm, posmt, wg, wu, wd)


def _rope_tables(seq):
    inv = 1.0 / (ROPE_THETA ** (jnp.arange(0, DIFF_DH, 2, dtype=F32) / DIFF_DH))
    ang = jnp.arange(seq, dtype=F32)[:, None] * inv[None, :]
    ang = jnp.concatenate([ang, ang], axis=-1)
    cos, sin = jnp.cos(ang), jnp.sin(ang)
    sign = jnp.where(jnp.arange(DIFF_DH) < DIFF_DH // 2, -1.0, 1.0).astype(F32)
    return jnp.tile(cos, (1, 2)), jnp.tile(sin * sign, (1, 2))


def _block_diag_ones():
    idx = np.arange(256) // 64
    return jnp.asarray(idx[:, None] == idx[None, :], dtype=BF16)


def _prepare_layer(l, p):
    lam_init = 0.8 - 0.6 * math.exp(-0.3 * l)
    lq = p["lambda_q"][l].astype(F32)
    lk = p["lambda_k"][l].astype(F32)
    lam = jnp.exp(jnp.sum(lq[0] * lk[0])) - jnp.exp(jnp.sum(lq[1] * lk[1])) + lam_init
    scale = DIFF_DH ** -0.5 * math.log2(math.e)
    tile8 = lambda g: jnp.tile(g.astype(F32), HALF_WIDTH // g.shape[0])
    zeros = jnp.zeros((HALF_WIDTH,), F32)
    dq_gain = p["diff_q_norm"][l].astype(F32) * scale
    dk_gain = p["diff_k_norm"][l].astype(F32)
    gains = jnp.stack([
        tile8(dq_gain), tile8(dk_gain), zeros,
        tile8(p["na_q_norm"][l]) * scale, tile8(p["na_k_norm"][l]), zeros, zeros, zeros])
    score_bound = SCORE_BOUND_MARGIN * DIFF_DH * jnp.max(jnp.abs(dq_gain)) * jnp.max(jnp.abs(dk_gain))
    lay = dict(
        scal=jnp.stack([lam, score_bound]).astype(F32),
        out_scale=1.0 - lam_init,
        attn_norm=p["attn_norm"][l].reshape(1, D_MODEL).astype(F32),
        w_in=p["w_in"][l].astype(BF16),
        gains=gains,
        g_sub=p["diff_sub_norm"][l].reshape(1, LANES).astype(F32),
        na_bias=_na_bias_table(p["na_rpb"][l]),
        g_na_out=jnp.tile(p["na_out_norm"][l].astype(F32), 2).reshape(1, LANES),
        w_out=p["w_out"][l].astype(BF16),
        ffn_norm=p["ffn_norm"][l].reshape(1, D_MODEL).astype(F32),
    )
    i = l // 2
    if l % 2 == 0:
        lay.update(wg=p["dense_w_gate"][i].astype(BF16), wu=p["dense_w_up"][i].astype(BF16),
                   wd=p["dense_w_down"][i].astype(BF16))
    else:
        wr = p["moe_router"][i].astype(F32)
        wr_hi = wr.astype(BF16)
        wr_lo = (wr - wr_hi.astype(F32)).astype(BF16)
        wr2 = jnp.zeros((D_MODEL, LANES), BF16)
        wr2 = wr2.at[:, :N_EXPERTS].set(wr_hi).at[:, N_EXPERTS:2 * N_EXPERTS].set(wr_lo)
        lay.update(wr=wr2, wg=p["moe_w_gate"][i].astype(BF16), wu=p["moe_w_up"][i].astype(BF16),
                   wd=p["moe_w_down"][i].astype(BF16))
    return lay


def _run_trunk(x, layers, bd):
    b, s, _ = x.shape
    t = b * s
    cos, sin = _rope_tables(s)
    x = x.reshape(t, D_MODEL)
    for l, lay in enumerate(layers):
        proj, vt = _in_proj(x, lay["attn_norm"], lay["w_in"], lay["gains"], cos, sin, bd, s)
        proj3 = proj.reshape(b, s, IN_WIDTH)
        a_out = _diff_attn(proj3, vt, lay["scal"], lay["g_sub"], lay["out_scale"])
        b_out = _na_attn(proj3, lay["na_bias"], lay["g_na_out"])
        a_out = a_out.reshape(t, HALF_WIDTH)
        b_out = b_out.reshape(t, HALF_WIDTH)
        if l % 2 == 0:
            x, h = _out_proj(x, a_out, b_out, lay["w_out"], lay["ffn_norm"])
            x = _swiglu(x, h, lay["wg"], lay["wu"], lay["wd"])
        else:
            x, h, c = _out_proj(x, a_out, b_out, lay["w_out"], lay["ffn_norm"], lay["wr"])
            x = _moe_sparse(x, h, c, lay["wg"], lay["wu"], lay["wd"])
    return x.reshape(b, s, D_MODEL)


def kernel(x_prompt, x_sample, attn_norm, w_in, w_out, diff_q_norm, diff_k_norm, lambda_q, lambda_k, diff_sub_norm, na_q_norm, na_k_norm, na_rpb, na_out_norm, ffn_norm, dense_w_gate, dense_w_up, dense_w_down, moe_router, moe_w_gate, moe_w_up, moe_w_down):
    p = dict(attn_norm=attn_norm, w_in=w_in, w_out=w_out, diff_q_norm=diff_q_norm, diff_k_norm=diff_k_norm,
             lambda_q=lambda_q, lambda_k=lambda_k, diff_sub_norm=diff_sub_norm, na_q_norm=na_q_norm,
             na_k_norm=na_k_norm, na_rpb=na_rpb, na_out_norm=na_out_norm, ffn_norm=ffn_norm,
             dense_w_gate=dense_w_gate, dense_w_up=dense_w_up, dense_w_down=dense_w_down,
             moe_router=moe_router, moe_w_gate=moe_w_gate, moe_w_up=moe_w_up, moe_w_down=moe_w_down)
    layers = [_prepare_layer(l, p) for l in range(DEPTH)]
    bd = _block_diag_ones()
    return (_run_trunk(x_prompt, layers, bd), _run_trunk(x_sample, layers, bd))
```

```python
import functools
import math

import jax
import jax.numpy as jnp
import numpy as np
from jax import lax
from jax.experimental import pallas as pl
from jax.experimental.pallas import tpu as pltpu

D_MODEL = 1024
DEPTH = 4
GRID_W = 64
DIFF_HEADS = 4
DIFF_DH = 64
NA_HEADS = 8
NA_DH = 64
NA_KH = 8
NA_KW = 16
HALF_WIDTH = 512
IN_WIDTH = 6 * HALF_WIDTH
D_FF = 2816
FF_BLOCK = 1408
N_EXPERTS = 8
ROPE_THETA = 10000.0
EPS = 1e-6
MASK_VALUE = -1e30
SCORE_BOUND_MARGIN = 1.02
MAX_SAFE_SCORE_BOUND = 60.0

LANES = 128
VMEM_LIMIT = 56 * 1024 * 1024

BF16 = jnp.bfloat16
F32 = jnp.float32


def _cparams(sem):
    return pltpu.CompilerParams(dimension_semantics=sem, vmem_limit_bytes=VMEM_LIMIT)


def _dot(a, b):
    return jnp.dot(a, b, preferred_element_type=F32)


def _dot_nt(a, b):
    return lax.dot_general(a, b, (((1,), (1,)), ((), ())), preferred_element_type=F32)


def _rms(x, g):
    return x * lax.rsqrt(jnp.mean(x * x, axis=-1, keepdims=True) + EPS) * g


def _group_norm64(y, gain, bd_ref):
    parts = []
    for c in range(HALF_WIDTH // 256):
        yc = y[:, c * 256:(c + 1) * 256]
        ss = _dot((yc * yc).astype(BF16), bd_ref[...])
        parts.append(yc * lax.rsqrt(ss * (1.0 / 64) + EPS))
    return jnp.concatenate(parts, axis=1) * gain


def _rope64(y, cos, sin_signed):
    n = y.shape[1]
    lane = lax.broadcasted_iota(jnp.int32, y.shape, 1)
    first_half = (lane % 64) < 32
    rot = jnp.where(first_half, pltpu.roll(y, n - 32, 1), pltpu.roll(y, 32, 1))
    return y * cos + rot * sin_signed


DV_SECTION = 2


def _in_proj_kernel(x_ref, g_ref, w_ref, gains_ref, cos_ref, sin_ref, bd_ref, o_ref, vt_ref):
    x = x_ref[...]
    h = _rms(x, g_ref[...]).astype(BF16)
    cos = jnp.concatenate([cos_ref[...]] * (HALF_WIDTH // LANES), axis=1)
    sin = jnp.concatenate([sin_ref[...]] * (HALF_WIDTH // LANES), axis=1)
    for sec in range(6):
        y = _dot(h, w_ref[:, sec * HALF_WIDTH:(sec + 1) * HALF_WIDTH])
        if sec in (0, 1, 3, 4):
            y = _group_norm64(y, gains_ref[sec:sec + 1, :], bd_ref)
        if sec in (0, 1):
            y = _rope64(y, cos, sin)
        o_ref[:, sec * HALF_WIDTH:(sec + 1) * HALF_WIDTH] = y.astype(BF16)
        if sec == DV_SECTION:
            vt_ref[0] = y.T.astype(BF16)


def _in_proj(x, g, w, gains, cos, sin, bd, seq, tm=512):
    t = x.shape[0]
    tm = min(tm, seq)
    nseq = seq // tm
    return pl.pallas_call(
        _in_proj_kernel,
        grid=(t // tm,),
        in_specs=[
            pl.BlockSpec((tm, D_MODEL), lambda i: (i, 0)),
            pl.BlockSpec((1, D_MODEL), lambda i: (0, 0)),
            pl.BlockSpec((D_MODEL, IN_WIDTH), lambda i: (0, 0)),
            pl.BlockSpec((8, HALF_WIDTH), lambda i: (0, 0)),
            pl.BlockSpec((tm, LANES), lambda i: (i % nseq, 0)),
            pl.BlockSpec((tm, LANES), lambda i: (i % nseq, 0)),
            pl.BlockSpec((256, 256), lambda i: (0, 0)),
        ],
        out_specs=[pl.BlockSpec((tm, IN_WIDTH), lambda i: (i, 0)),
                   pl.BlockSpec((1, HALF_WIDTH, tm), lambda i: (i // nseq, 0, i % nseq))],
        out_shape=[jax.ShapeDtypeStruct((t, IN_WIDTH), BF16),
                   jax.ShapeDtypeStruct((t // seq, HALF_WIDTH, seq), BF16)],
        compiler_params=_cparams(("parallel",)),
        name="in_proj",
    )(x, g, w, gains, cos, sin, bd)


def _diff_attn_kernel(scal_ref, q_ref, k_ref, v_ref, g_ref, o_ref, acc0_ref, acc1_ref, l0_ref, l1_ref, *,
                      tk, out_scale, bounded):
    q = q_ref[0]
    tq = q.shape[0]
    lane = lax.broadcasted_iota(jnp.int32, q.shape, 1)
    qm = (jnp.where(lane < DIFF_DH, q, jnp.zeros_like(q)), jnp.where(lane >= DIFF_DH, q, jnp.zeros_like(q)))
    accs = (acc0_ref, acc1_ref)
    ls = (l0_ref, l1_ref)
    for r in accs + ls:
        r[...] = jnp.zeros_like(r)
    nk = k_ref.shape[1] // tk

    def lane_fold(p):
        out = p[:, :LANES]
        for c in range(1, tk // LANES):
            out = out + p[:, c * LANES:(c + 1) * LANES]
        return out

    def load(j):
        start = pl.multiple_of(j * tk, tk)
        return k_ref[0, pl.ds(start, tk), :], v_ref[0, pl.ds(start, tk), :]

    if bounded:
        bound = scal_ref[1]

        def body(j, carry):
            k, v = load(j)
            for a in range(2):
                p = jnp.exp2(_dot_nt(qm[a], k) - bound)
                ls[a][...] += lane_fold(p)
                accs[a][...] += _dot(p.astype(BF16), v)
            return carry

        lax.fori_loop(0, nk, body, 0)
    else:
        def body(j, ms):
            k, v = load(j)
            new_ms = []
            for a in range(2):
                s = _dot_nt(qm[a], k)
                m_new = jnp.maximum(ms[a], jnp.max(s, axis=-1, keepdims=True))
                alpha = jnp.exp2(ms[a] - m_new)
                p = jnp.exp2(s - m_new)
                ls[a][...] = alpha * ls[a][...] + lane_fold(p)
                accs[a][...] = alpha * accs[a][...] + _dot(p.astype(BF16), v)
                new_ms.append(m_new)
            return tuple(new_ms)

        neg = jnp.full((tq, 1), -jnp.inf, F32)
        lax.fori_loop(0, nk, body, (neg, neg))

    lam = scal_ref[0]
    l0 = jnp.sum(l0_ref[...], axis=-1, keepdims=True)
    l1 = jnp.sum(l1_ref[...], axis=-1, keepdims=True)
    o = acc0_ref[...] / l0 - lam * (acc1_ref[...] / l1)
    o = _rms(o, g_ref[...]) * out_scale
    o_ref[0] = o.astype(BF16)


def _diff_attn_t_kernel(scal_ref, q_ref, k_ref, vt_ref, g_ref, o_ref, acc0_ref, acc1_ref, l0_ref, l1_ref, *,
                        tk, out_scale):
    q = q_ref[0]
    lane = lax.broadcasted_iota(jnp.int32, q.shape, 1)
    qm = (jnp.where(lane < DIFF_DH, q, jnp.zeros_like(q)), jnp.where(lane >= DIFF_DH, q, jnp.zeros_like(q)))
    accs = (acc0_ref, acc1_ref)
    ls = (l0_ref, l1_ref)
    for r in accs + ls:
        r[...] = jnp.zeros_like(r)
    nk = k_ref.shape[1] // tk
    tq = q.shape[0]
    bound = scal_ref[1]

    def body(j, carry):
        start = pl.multiple_of(j * tk, tk)
        k = k_ref[0, pl.ds(start, tk), :]
        vt = vt_ref[0, :, pl.ds(start, tk)]
        for a in range(2):
            p = jnp.exp2(_dot_nt(k, qm[a]) - bound)
            ls[a][...] += jnp.sum(p.reshape(tk // 8, 8, tq), axis=0)
            accs[a][...] += _dot(vt, p.astype(BF16))
        return carry

    lax.fori_loop(0, nk, body, 0)
    lam = scal_ref[0]
    l0 = jnp.sum(l0_ref[...], axis=0, keepdims=True)
    l1 = jnp.sum(l1_ref[...], axis=0, keepdims=True)
    ot = acc0_ref[...] / l0 - lam * (acc1_ref[...] / l1)
    ot = ot * lax.rsqrt(jnp.mean(ot * ot, axis=0, keepdims=True) + EPS)
    o_ref[0] = (ot.T * g_ref[...] * out_scale).astype(BF16)


def _diff_attn_t_call(proj, vt, scal, g_sub, out_scale, tq=1024, tk=2048):
    b, s, _ = proj.shape
    tq, tk = min(tq, s), min(tk, s)
    kern = functools.partial(_diff_attn_t_kernel, tk=tk, out_scale=out_scale)
    return pl.pallas_call(
        kern,
        grid=(b, DIFF_HEADS, s // tq),
        in_specs=[
            pl.BlockSpec(memory_space=pltpu.SMEM),
            pl.BlockSpec((1, tq, LANES), lambda bi, h, i: (bi, i, h)),
            pl.BlockSpec((1, s, LANES), lambda bi, h, i: (bi, 0, DIFF_HEADS + h)),
            pl.BlockSpec((1, LANES, s), lambda bi, h, i: (bi, h, 0)),
            pl.BlockSpec((1, LANES), lambda bi, h, i: (0, 0)),
        ],
        out_specs=pl.BlockSpec((1, tq, LANES), lambda bi, h, i: (bi, i, h)),
        out_shape=jax.ShapeDtypeStruct((b, s, HALF_WIDTH), BF16),
        scratch_shapes=[pltpu.VMEM((LANES, tq), F32)] * 2 + [pltpu.VMEM((8, tq), F32)] * 2,
        compiler_params=_cparams(("parallel", "parallel", "arbitrary")),
        name="diff_attn_t",
    )(scal, proj, proj, vt, g_sub)


def _diff_attn(proj, vt, scal, g_sub, out_scale):
    return lax.cond(
        scal[1] <= MAX_SAFE_SCORE_BOUND,
        lambda: _diff_attn_t_call(proj, vt, scal, g_sub, out_scale),
        lambda: _diff_attn_call(proj, scal, g_sub, out_scale, False))


def _diff_attn_call(proj, scal, g_sub, out_scale, bounded, tq=1024, tk=1024):
    b, s, _ = proj.shape
    tq, tk = min(tq, s), min(tk, s)
    kern = functools.partial(_diff_attn_kernel, tk=tk, out_scale=out_scale, bounded=bounded)
    return pl.pallas_call(
        kern,
        grid=(b, DIFF_HEADS, s // tq),
        in_specs=[
            pl.BlockSpec(memory_space=pltpu.SMEM),
            pl.BlockSpec((1, tq, LANES), lambda bi, h, i: (bi, i, h)),
            pl.BlockSpec((1, s, LANES), lambda bi, h, i: (bi, 0, DIFF_HEADS + h)),
            pl.BlockSpec((1, s, LANES), lambda bi, h, i: (bi, 0, 2 * DIFF_HEADS + h)),
            pl.BlockSpec((1, LANES), lambda bi, h, i: (0, 0)),
        ],
        out_specs=pl.BlockSpec((1, tq, LANES), lambda bi, h, i: (bi, i, h)),
        out_shape=jax.ShapeDtypeStruct((b, s, HALF_WIDTH), BF16),
        scratch_shapes=[pltpu.VMEM((tq, LANES), F32)] * 4,
        compiler_params=_cparams(("parallel", "parallel", "arbitrary")),
        name="diff_attn_bounded" if bounded else "diff_attn_online",
    )(scal, proj, proj, proj, g_sub)


NA_ROWS_PER_STEP = 8
NA_BLOCK = NA_ROWS_PER_STEP * GRID_W


def _na_attn_kernel(q_ref, kp_ref, kc_ref, kn_ref, vp_ref, vc_ref, vn_ref, bias_ref, g_ref, o_ref,
                    kbuf, vbuf, *, rows):
    i = pl.program_id(1)
    for n, (kr, vr) in enumerate(((kp_ref, vp_ref), (kc_ref, vc_ref), (kn_ref, vn_ref))):
        kbuf[n * NA_BLOCK:(n + 1) * NA_BLOCK, :] = kr[0]
        vbuf[n * NA_BLOCK:(n + 1) * NA_BLOCK, :] = vr[0]
    lane = lax.broadcasted_iota(jnp.int32, (GRID_W, LANES), 1)
    halves = (lane < NA_DH, lane >= NA_DH)

    def row_body(qr, carry):
        r = i * NA_ROWS_PER_STEP + qr
        r0 = jnp.clip(r - NA_KH // 2, 0, rows - NA_KH)
        start = pl.multiple_of((r0 - (i - 1) * NA_ROWS_PER_STEP) * GRID_W, GRID_W)
        rp = r - r0
        qrow = pl.multiple_of(qr * GRID_W, GRID_W)
        npairs = NA_HEADS // 2
        col = [slice(p * LANES, (p + 1) * LANES) for p in range(npairs)]
        scores = []
        for p in range(npairs):
            qp = q_ref[0, pl.ds(qrow, GRID_W), col[p]]
            kp = kbuf[pl.ds(start, NA_KH * GRID_W), col[p]]
            q2 = jnp.concatenate([jnp.where(h, qp, jnp.zeros_like(qp)) for h in halves], axis=0)
            scores.append(_dot_nt(q2, kp) + bias_ref[p, rp])
        probs = []
        for s in scores:
            m = jnp.max(s, axis=-1, keepdims=True)
            pe = jnp.exp2(s - m)
            probs.append((pe.astype(BF16), jnp.sum(pe, axis=-1, keepdims=True)))
        for p in range(npairs):
            vp = vbuf[pl.ds(start, NA_KH * GRID_W), col[p]]
            pe, l = probs[p]
            o_both = _dot(pe, vp) / l
            o = jnp.where(halves[0], o_both[:GRID_W], o_both[GRID_W:])
            o2 = o * o
            ss = [jnp.sum(jnp.where(h, o2, 0.0), axis=-1, keepdims=True) for h in halves]
            inv = jnp.where(halves[0], lax.rsqrt(ss[0] * (1.0 / NA_DH) + EPS),
                            lax.rsqrt(ss[1] * (1.0 / NA_DH) + EPS))
            o_ref[0, pl.ds(qrow, GRID_W), col[p]] = (o * inv * g_ref[...]).astype(BF16)
        return carry

    lax.fori_loop(0, NA_ROWS_PER_STEP, row_body, 0, unroll=True)


def _na_attn(proj, bias, g_out):
    b, s, _ = proj.shape
    rows = s // GRID_W
    nb = rows // NA_ROWS_PER_STEP
    qcol, kcol, vcol = 3, 4, 5
    blk = (1, NA_BLOCK, HALF_WIDTH)

    def prev(bi, i, c):
        return (bi, jnp.maximum(i - 1, 0), c)

    def nxt(bi, i, c):
        return (bi, jnp.minimum(i + 1, nb - 1), c)

    return pl.pallas_call(
        functools.partial(_na_attn_kernel, rows=rows),
        grid=(b, nb),
        in_specs=[
            pl.BlockSpec(blk, lambda bi, i: (bi, i, qcol)),
            pl.BlockSpec(blk, lambda bi, i: prev(bi, i, kcol)),
            pl.BlockSpec(blk, lambda bi, i: (bi, i, kcol)),
            pl.BlockSpec(blk, lambda bi, i: nxt(bi, i, kcol)),
            pl.BlockSpec(blk, lambda bi, i: prev(bi, i, vcol)),
            pl.BlockSpec(blk, lambda bi, i: (bi, i, vcol)),
            pl.BlockSpec(blk, lambda bi, i: nxt(bi, i, vcol)),
            pl.BlockSpec((NA_HEADS // 2, NA_KH, 2 * GRID_W, NA_KH * GRID_W), lambda bi, i: (0, 0, 0, 0)),
            pl.BlockSpec((1, LANES), lambda bi, i: (0, 0)),
        ],
        out_specs=pl.BlockSpec(blk, lambda bi, i: (bi, i, 0)),
        out_shape=jax.ShapeDtypeStruct((b, s, HALF_WIDTH), BF16),
        scratch_shapes=[pltpu.VMEM((3 * NA_BLOCK, HALF_WIDTH), BF16),
                        pltpu.VMEM((3 * NA_BLOCK, HALF_WIDTH), BF16)],
        compiler_params=_cparams(("parallel", "arbitrary")),
        name="na_attn",
    )(proj, proj, proj, proj, proj, proj, proj, bias, g_out)


def _na_bias_table(rpb):
    qc = np.arange(GRID_W)
    kc = np.arange(GRID_W)
    win = np.clip(qc - NA_KW // 2, 0, GRID_W - NA_KW)
    valid = (kc[None, :] >= win[:, None]) & (kc[None, :] < win[:, None] + NA_KW)
    padded = jnp.pad(rpb.astype(F32) * math.log2(math.e), ((0, 0), (0, 0), (GRID_W, GRID_W)))
    by_col = jnp.stack([padded[:, :, GRID_W + NA_KW - 1 - c:2 * GRID_W + NA_KW - 1 - c] for c in range(GRID_W)],
                       axis=2)
    by_col = jnp.where(jnp.asarray(valid)[None, None], by_col, MASK_VALUE)
    t = jnp.stack([by_col[:, NA_KH - 1 - r:2 * NA_KH - 1 - r] for r in range(NA_KH)], axis=1)
    t = t.reshape(NA_HEADS // 2, 2, NA_KH, NA_KH, GRID_W, GRID_W)
    t = t.transpose(0, 2, 1, 4, 3, 5)
    return t.reshape(NA_HEADS // 2, NA_KH, 2 * GRID_W, NA_KH * GRID_W)


def _out_proj_kernel(x_ref, a_ref, b_ref, w_ref, g_ref, xo_ref, h_ref):
    x = x_ref[...] + _dot(a_ref[...], w_ref[:HALF_WIDTH, :]) + _dot(b_ref[...], w_ref[HALF_WIDTH:, :])
    xo_ref[...] = x
    h_ref[...] = _rms(x, g_ref[...]).astype(BF16)


def _out_proj_router_kernel(x_ref, a_ref, b_ref, w_ref, g_ref, wr_ref, xo_ref, h_ref, c_ref):
    x = x_ref[...] + _dot(a_ref[...], w_ref[:HALF_WIDTH, :]) + _dot(b_ref[...], w_ref[HALF_WIDTH:, :])
    xo_ref[...] = x
    h = _rms(x, g_ref[...])
    h_hi = h.astype(BF16)
    h_ref[...] = h_hi
    h_lo = (h - h_hi.astype(F32)).astype(BF16)
    wr = wr_ref[...]
    both = _dot(h_hi, wr)
    lane = lax.broadcasted_iota(jnp.int32, both.shape, 1).astype(F32)
    lo_part = pltpu.roll(both, LANES - N_EXPERTS, 1)
    logits = both + lo_part + _dot(h_lo, wr)
    logits = jnp.where(lane < N_EXPERTS, logits, -jnp.inf)
    m1 = jnp.max(logits, axis=-1, keepdims=True)
    i1 = jnp.min(jnp.where(logits == m1, lane, float(LANES)), axis=-1, keepdims=True)
    rest = jnp.where(lane == i1, -jnp.inf, logits)
    m2 = jnp.max(rest, axis=-1, keepdims=True)
    i2 = jnp.min(jnp.where(rest == m2, lane, float(LANES)), axis=-1, keepdims=True)
    e2 = jnp.exp(m2 - m1)
    g1 = 1.0 / (1.0 + e2)
    g2 = e2 * g1
    c_ref[...] = jnp.where(lane == i1, g1, jnp.where(lane == i2, g2, 0.0))


def _out_proj(x, a, b, w, g, wr=None, tm=512):
    t = x.shape[0]
    tm = min(tm, t)
    row = lambda i: (i, 0)
    const = lambda i: (0, 0)
    in_specs = [
        pl.BlockSpec((tm, D_MODEL), row),
        pl.BlockSpec((tm, HALF_WIDTH), row),
        pl.BlockSpec((tm, HALF_WIDTH), row),
        pl.BlockSpec((D_MODEL, D_MODEL), const),
        pl.BlockSpec((1, D_MODEL), const),
    ]
    out_specs = [pl.BlockSpec((tm, D_MODEL), row), pl.BlockSpec((tm, D_MODEL), row)]
    out_shape = [jax.ShapeDtypeStruct((t, D_MODEL), F32), jax.ShapeDtypeStruct((t, D_MODEL), BF16)]
    args = [x, a, b, w, g]
    kern = _out_proj_kernel
    if wr is not None:
        in_specs.append(pl.BlockSpec((D_MODEL, LANES), const))
        out_specs.append(pl.BlockSpec((tm, LANES), row))
        out_shape.append(jax.ShapeDtypeStruct((t, LANES), F32))
        args.append(wr)
        kern = _out_proj_router_kernel
    return pl.pallas_call(
        kern,
        grid=(t // tm,),
        in_specs=in_specs,
        out_specs=out_specs,
        out_shape=out_shape,
        compiler_params=_cparams(("parallel",)),
        name="out_proj",
    )(*args)


def _swiglu_kernel(x_ref, h_ref, wg_ref, wu_ref, wd_ref, o_ref, acc_ref):
    f = pl.program_id(1)

    @pl.when(f == 0)
    def _():
        acc_ref[...] = x_ref[...]

    h = h_ref[...]
    g = _dot(h, wg_ref[0])
    u = _dot(h, wu_ref[0])
    a = (g * jax.nn.sigmoid(g) * u).astype(BF16)
    acc_ref[...] += _dot(a, wd_ref[...])

    @pl.when(f == pl.num_programs(1) - 1)
    def _():
        o_ref[...] = acc_ref[...]


def _swiglu(x, h, wg, wu, wd, tm=1024):
    t = x.shape[0]
    tf = FF_BLOCK
    tm = min(tm, t)
    return pl.pallas_call(
        _swiglu_kernel,
        grid=(t // tm, D_FF // tf),
        in_specs=[
            pl.BlockSpec((tm, D_MODEL), lambda i, f: (i, 0)),
            pl.BlockSpec((tm, D_MODEL), lambda i, f: (i, 0)),
            pl.BlockSpec((1, D_MODEL, tf), lambda i, f: (f, 0, 0)),
            pl.BlockSpec((1, D_MODEL, tf), lambda i, f: (f, 0, 0)),
            pl.BlockSpec((tf, D_MODEL), lambda i, f: (f, 0)),
        ],
        out_specs=pl.BlockSpec((tm, D_MODEL), lambda i, f: (i, 0)),
        out_shape=jax.ShapeDtypeStruct((t, D_MODEL), F32),
        scratch_shapes=[pltpu.VMEM((tm, D_MODEL), F32)],
        compiler_params=_cparams(("parallel", "arbitrary")),
        name="swiglu",
    )(x, h, wg, wu, wd)


MOE_TILE = 1024
MOE_CHUNK = 256
MOE_TAIL = 128
TRI_BLOCK = 512


def _route_kernel(c_ref, ltri_ref, utri_ref, posm_ref, posmt_ref):
    tm = c_ref.shape[0]
    carry = jnp.zeros((1, LANES), F32)
    carry_t = jnp.zeros((LANES, 1), F32)
    for b in range(tm // TRI_BLOCK):
        rows = slice(b * TRI_BLOCK, (b + 1) * TRI_BLOCK)
        sel = c_ref[rows, :] > 0.0
        self = jnp.where(sel, 1.0, 0.0)
        pos = _dot(ltri_ref[...], self.astype(BF16)) + carry
        posm_ref[rows, :] = jnp.where(sel, pos, -1.0)
        sel_t = self.T
        pos_t = _dot(sel_t.astype(BF16), utri_ref[...]) + carry_t
        posmt_ref[0, :, rows] = jnp.where(sel_t > 0.0, pos_t, -1.0)
        carry = carry + jnp.sum(self, axis=0, keepdims=True)
        carry_t = carry_t + jnp.sum(sel_t, axis=1, keepdims=True)


def _route(c, tm):
    t = c.shape[0]
    idx = np.arange(TRI_BLOCK)
    ltri = jnp.asarray(idx[None, :] < idx[:, None], dtype=BF16)
    utri = jnp.asarray(idx[:, None] < idx[None, :], dtype=BF16)
    return pl.pallas_call(
        _route_kernel,
        grid=(t // tm,),
        in_specs=[
            pl.BlockSpec((tm, LANES), lambda i: (i, 0)),
            pl.BlockSpec((TRI_BLOCK, TRI_BLOCK), lambda i: (0, 0)),
            pl.BlockSpec((TRI_BLOCK, TRI_BLOCK), lambda i: (0, 0)),
        ],
        out_specs=[pl.BlockSpec((tm, LANES), lambda i: (i, 0)),
                   pl.BlockSpec((1, LANES, tm), lambda i: (i, 0, 0))],
        out_shape=[jax.ShapeDtypeStruct((t, LANES), F32),
                   jax.ShapeDtypeStruct((t // tm, LANES, tm), F32)],
        compiler_params=_cparams(("parallel",)),
        name="moe_route",
    )(c, ltri, utri)


def _moe_sparse_kernel(x_ref, h_ref, c_ref, posm_ref, posmt_ref, wg_ref, wu_ref, wd_ref, o_ref, hc_ref, y_ref):
    e = pl.program_id(1)
    f = pl.program_id(2)
    last_f = pl.num_programs(2) - 1
    tm = x_ref.shape[0]

    @pl.when((e == 0) & (f == 0))
    def _():
        o_ref[...] = x_ref[...]

    pos_row = posmt_ref[0, pl.ds(e, 1), :]
    load = (jnp.max(pos_row) + 1.0).astype(jnp.int32)
    rem = load % MOE_CHUNK
    nfull = load // MOE_CHUNK + (rem > MOE_TAIL).astype(jnp.int32)
    has_tail = (rem > 0) & (rem <= MOE_TAIL)
    tail_off = nfull * MOE_CHUNK

    def rows_of(off, size):
        return pl.ds(pl.multiple_of(off, MOE_TAIL), size)

    def for_each_chunk(fn):
        def full(k, carry):
            fn(k * MOE_CHUNK, MOE_CHUNK)
            return carry

        lax.fori_loop(0, nfull, full, 0)

        @pl.when(has_tail)
        def _():
            fn(tail_off, MOE_TAIL)

    def gather(off, size):
        slot = (lax.broadcasted_iota(jnp.int32, (size, tm), 0) + off).astype(F32)
        onehot = jnp.where(pos_row == slot, 1.0, 0.0).astype(BF16)
        hc_ref[rows_of(off, size), :] = _dot(onehot, h_ref[...]).astype(BF16)

    def expert(off, size):
        hk = hc_ref[rows_of(off, size), :]
        g = _dot(hk, wg_ref[0, 0])
        u = _dot(hk, wu_ref[0, 0])
        y = _dot((g * jax.nn.sigmoid(g) * u).astype(BF16), wd_ref[0])

        @pl.when(f == 0)
        def _():
            y_ref[rows_of(off, size), :] = y

        @pl.when(f > 0)
        def _():
            y_ref[rows_of(off, size), :] += y

    @pl.when(f == 0)
    def _():
        for_each_chunk(gather)

    for_each_chunk(expert)

    @pl.when(f == last_f)
    def _():
        lane = lax.broadcasted_iota(jnp.int32, (tm, LANES), 1)
        pos = jnp.sum(jnp.where(lane == e, posm_ref[...], 0.0), axis=-1, keepdims=True)
        gate = jnp.sum(jnp.where(lane == e, c_ref[...], 0.0), axis=-1, keepdims=True)

        def scatter(off, size):
            slot = (lax.broadcasted_iota(jnp.int32, (tm, size), 1) + off).astype(F32)
            onehot = jnp.where(pos == slot, 1.0, 0.0).astype(BF16)
            o_ref[...] += gate * _dot(onehot, y_ref[rows_of(off, size), :].astype(BF16))

        for_each_chunk(scatter)


def _moe_sparse(x, h, c, wg, wu, wd, tm=MOE_TILE):
    t = x.shape[0]
    tf = FF_BLOCK
    tm = min(tm, t)
    posm, posmt = _route(c, tm)
    return pl.pallas_call(
        _moe_sparse_kernel,
        grid=(t // tm, N_EXPERTS, D_FF // tf),
        in_specs=[
            pl.BlockSpec((tm, D_MODEL), lambda i, e, f: (i, 0)),
            pl.BlockSpec((tm, D_MODEL), lambda i, e, f: (i, 0)),
            pl.BlockSpec((tm, LANES), lambda i, e, f: (i, 0)),
            pl.BlockSpec((tm, LANES), lambda i, e, f: (i, 0)),
            pl.BlockSpec((1, LANES, tm), lambda i, e, f: (i, 0, 0)),
            pl.BlockSpec((1, 1, D_MODEL, tf), lambda i, e, f: (e, f, 0, 0)),
            pl.BlockSpec((1, 1, D_MODEL, tf), lambda i, e, f: (e, f, 0, 0)),
            pl.BlockSpec((1, tf, D_MODEL), lambda i, e, f: (e, f, 0)),
        ],
        out_specs=pl.BlockSpec((tm, D_MODEL), lambda i, e, f: (i, 0)),
        out_shape=jax.ShapeDtypeStruct((t, D_MODEL), F32),
        scratch_shapes=[pltpu.VMEM((tm, D_MODEL), BF16), pltpu.VMEM((tm, D_MODEL), F32)],
        compiler_params=_cparams(("parallel", "arbitrary", "arbitrary")),
        name="moe_sparse",
    )(x, h, c, posm, posmt, wg, wu, wd)


def _rope_tables(seq):
    inv = 1.0 / (ROPE_THETA ** (jnp.arange(0, DIFF_DH, 2, dtype=F32) / DIFF_DH))
    ang = jnp.arange(seq, dtype=F32)[:, None] * inv[None, :]
    ang = jnp.concatenate([ang, ang], axis=-1)
    cos, sin = jnp.cos(ang), jnp.sin(ang)
    sign = jnp.where(jnp.arange(DIFF_DH) < DIFF_DH // 2, -1.0, 1.0).astype(F32)
    return jnp.tile(cos, (1, 2)), jnp.tile(sin * sign, (1, 2))


def _block_diag_ones():
    idx = np.arange(256) // 64
    return jnp.asarray(idx[:, None] == idx[None, :], dtype=BF16)


def _ff_major(w):
    w = w.astype(BF16).reshape(w.shape[:-1] + (D_FF // FF_BLOCK, FF_BLOCK))
    return jnp.swapaxes(w, -2, -3)


def _prepare_layer(l, p):
    lam_init = 0.8 - 0.6 * math.exp(-0.3 * l)
    lq = p["lambda_q"][l].astype(F32)
    lk = p["lambda_k"][l].astype(F32)
    lam = jnp.exp(jnp.sum(lq[0] * lk[0])) - jnp.exp(jnp.sum(lq[1] * lk[1])) + lam_init
    scale = DIFF_DH ** -0.5 * math.log2(math.e)
    tile8 = lambda g: jnp.tile(g.astype(F32), HALF_WIDTH // g.shape[0])
    zeros = jnp.zeros((HALF_WIDTH,), F32)
    dq_gain = p["diff_q_norm"][l].astype(F32) * scale
    dk_gain = p["diff_k_norm"][l].astype(F32)
    gains = jnp.stack([
        tile8(dq_gain), tile8(dk_gain), zeros,
        tile8(p["na_q_norm"][l]) * scale, tile8(p["na_k_norm"][l]), zeros, zeros, zeros])
    score_bound = SCORE_BOUND_MARGIN * DIFF_DH * jnp.max(jnp.abs(dq_gain)) * jnp.max(jnp.abs(dk_gain))
    lay = dict(
        scal=jnp.stack([lam, score_bound]).astype(F32),
        out_scale=1.0 - lam_init,
        attn_norm=p["attn_norm"][l].reshape(1, D_MODEL).astype(F32),
        w_in=p["w_in"][l].astype(BF16),
        gains=gains,
        g_sub=p["diff_sub_norm"][l].reshape(1, LANES).astype(F32),
        na_bias=_na_bias_table(p["na_rpb"][l]),
        g_na_out=jnp.tile(p["na_out_norm"][l].astype(F32), 2).reshape(1, LANES),
        w_out=p["w_out"][l].astype(BF16),
        ffn_norm=p["ffn_norm"][l].reshape(1, D_MODEL).astype(F32),
    )
    i = l // 2
    if l % 2 == 0:
        lay.update(wg=_ff_major(p["dense_w_gate"][i]), wu=_ff_major(p["dense_w_up"][i]),
                   wd=p["dense_w_down"][i].astype(BF16))
    else:
        wr = p["moe_router"][i].astype(F32)
        wr_hi = wr.astype(BF16)
        wr_lo = (wr - wr_hi.astype(F32)).astype(BF16)
        wr2 = jnp.zeros((D_MODEL, LANES), BF16)
        wr2 = wr2.at[:, :N_EXPERTS].set(wr_hi).at[:, N_EXPERTS:2 * N_EXPERTS].set(wr_lo)
        lay.update(wr=wr2, wg=_ff_major(p["moe_w_gate"][i]), wu=_ff_major(p["moe_w_up"][i]),
                   wd=p["moe_w_down"][i].astype(BF16))
    return lay


def _run_trunk(x, layers, bd):
    b, s, _ = x.shape
    t = b * s
    cos, sin = _rope_tables(s)
    x = x.reshape(t, D_MODEL)
    for l, lay in enumerate(layers):
        proj, vt = _in_proj(x, lay["attn_norm"], lay["w_in"], lay["gains"], cos, sin, bd, s)
        proj3 = proj.reshape(b, s, IN_WIDTH)
        a_out = _diff_attn(proj3, vt, lay["scal"], lay["g_sub"], lay["out_scale"])
        b_out = _na_attn(proj3, lay["na_bias"], lay["g_na_out"])
        a_out = a_out.reshape(t, HALF_WIDTH)
        b_out = b_out.reshape(t, HALF_WIDTH)
        if l % 2 == 0:
            x, h = _out_proj(x, a_out, b_out, lay["w_out"], lay["ffn_norm"])
            x = _swiglu(x, h, lay["wg"], lay["wu"], lay["wd"])
        else:
            x, h, c = _out_proj(x, a_out, b_out, lay["w_out"], lay["ffn_norm"], lay["wr"])
            x = _moe_sparse(x, h, c, lay["wg"], lay["wu"], lay["wd"])
    return x.reshape(b, s, D_MODEL)


def kernel(x_prompt, x_sample, attn_norm, w_in, w_out, diff_q_norm, diff_k_norm, lambda_q, lambda_k, diff_sub_norm, na_q_norm, na_k_norm, na_rpb, na_out_norm, ffn_norm, dense_w_gate, dense_w_up, dense_w_down, moe_router, moe_w_gate, moe_w_up, moe_w_down):
    p = dict(attn_norm=attn_norm, w_in=w_in, w_out=w_out, diff_q_norm=diff_q_norm, diff_k_norm=diff_k_norm,
             lambda_q=lambda_q, lambda_k=lambda_k, diff_sub_norm=diff_sub_norm, na_q_norm=na_q_norm,
             na_k_norm=na_k_norm, na_rpb=na_rpb, na_out_norm=na_out_norm, ffn_norm=ffn_norm,
             dense_w_gate=dense_w_gate, dense_w_up=dense_w_up, dense_w_down=dense_w_down,
             moe_router=moe_router, moe_w_gate=moe_w_gate, moe_w_up=moe_w_up, moe_w_down=moe_w_down)
    layers = [_prepare_layer(l, p) for l in range(DEPTH)]
    bd = _block_diag_ones()
    return (_run_trunk(x_prompt, layers, bd), _run_trunk(x_sample, layers, bd))
```

```python
import functools
import math

import jax
import jax.numpy as jnp
import numpy as np
from jax import lax
from jax.experimental import pallas as pl
from jax.experimental.pallas import tpu as pltpu

D_MODEL = 1024
DEPTH = 4
GRID_W = 64
DIFF_HEADS = 4
DIFF_DH = 64
NA_HEADS = 8
NA_DH = 64
NA_KH = 8
NA_KW = 16
HALF_WIDTH = 512
IN_WIDTH = 6 * HALF_WIDTH
D_FF = 2816
FF_BLOCK = 1408
N_EXPERTS = 8
ROPE_THETA = 10000.0
EPS = 1e-6
MASK_VALUE = -1e30
SCORE_BOUND_MARGIN = 1.02
MAX_SAFE_SCORE_BOUND = 60.0

LANES = 128
VMEM_LIMIT = 56 * 1024 * 1024

BF16 = jnp.bfloat16
F32 = jnp.float32


def _cparams(sem):
    return pltpu.CompilerParams(dimension_semantics=sem, vmem_limit_bytes=VMEM_LIMIT)


def _dot(a, b):
    return jnp.dot(a, b, preferred_element_type=F32)


def _dot_nt(a, b):
    return lax.dot_general(a, b, (((1,), (1,)), ((), ())), preferred_element_type=F32)


def _rms(x, g):
    return x * lax.rsqrt(jnp.mean(x * x, axis=-1, keepdims=True) + EPS) * g


def _group_norm64(y, gain, bd_ref):
    parts = []
    for c in range(HALF_WIDTH // 256):
        yc = y[:, c * 256:(c + 1) * 256]
        ss = _dot((yc * yc).astype(BF16), bd_ref[...])
        parts.append(yc * lax.rsqrt(ss * (1.0 / 64) + EPS))
    return jnp.concatenate(parts, axis=1) * gain


def _rope64(y, cos, sin_signed):
    n = y.shape[1]
    lane = lax.broadcasted_iota(jnp.int32, y.shape, 1)
    first_half = (lane % 64) < 32
    rot = jnp.where(first_half, pltpu.roll(y, n - 32, 1), pltpu.roll(y, 32, 1))
    return y * cos + rot * sin_signed


DV_SECTION = 2


def _in_proj_kernel(x_ref, g_ref, w_ref, gains_ref, cos_ref, sin_ref, bd_ref, o_ref, vt_ref):
    x = x_ref[...]
    h = _rms(x, g_ref[...]).astype(BF16)
    cos = jnp.concatenate([cos_ref[...]] * (HALF_WIDTH // LANES), axis=1)
    sin = jnp.concatenate([sin_ref[...]] * (HALF_WIDTH // LANES), axis=1)
    for sec in range(6):
        y = _dot(h, w_ref[:, sec * HALF_WIDTH:(sec + 1) * HALF_WIDTH])
        if sec in (0, 1, 3, 4):
            y = _group_norm64(y, gains_ref[sec:sec + 1, :], bd_ref)
        if sec in (0, 1):
            y = _rope64(y, cos, sin)
        o_ref[:, sec * HALF_WIDTH:(sec + 1) * HALF_WIDTH] = y.astype(BF16)
        if sec == DV_SECTION:
            vt_ref[0] = y.T.astype(BF16)


def _in_proj(x, g, w, gains, cos, sin, bd, seq, tm=512):
    t = x.shape[0]
    tm = min(tm, seq)
    nseq = seq // tm
    return pl.pallas_call(
        _in_proj_kernel,
        grid=(t // tm,),
        in_specs=[
            pl.BlockSpec((tm, D_MODEL), lambda i: (i, 0)),
            pl.BlockSpec((1, D_MODEL), lambda i: (0, 0)),
            pl.BlockSpec((D_MODEL, IN_WIDTH), lambda i: (0, 0)),
            pl.BlockSpec((8, HALF_WIDTH), lambda i: (0, 0)),
            pl.BlockSpec((tm, LANES), lambda i: (i % nseq, 0)),
            pl.BlockSpec((tm, LANES), lambda i: (i % nseq, 0)),
            pl.BlockSpec((256, 256), lambda i: (0, 0)),
        ],
        out_specs=[pl.BlockSpec((tm, IN_WIDTH), lambda i: (i, 0)),
                   pl.BlockSpec((1, HALF_WIDTH, tm), lambda i: (i // nseq, 0, i % nseq))],
        out_shape=[jax.ShapeDtypeStruct((t, IN_WIDTH), BF16),
                   jax.ShapeDtypeStruct((t // seq, HALF_WIDTH, seq), BF16)],
        compiler_params=_cparams(("parallel",)),
        name="in_proj",
    )(x, g, w, gains, cos, sin, bd)


def _diff_attn_kernel(scal_ref, q_ref, k_ref, v_ref, g_ref, o_ref, acc0_ref, acc1_ref, l0_ref, l1_ref, *,
                      tk, out_scale, bounded):
    q = q_ref[0]
    tq = q.shape[0]
    lane = lax.broadcasted_iota(jnp.int32, q.shape, 1)
    qm = (jnp.where(lane < DIFF_DH, q, jnp.zeros_like(q)), jnp.where(lane >= DIFF_DH, q, jnp.zeros_like(q)))
    accs = (acc0_ref, acc1_ref)
    ls = (l0_ref, l1_ref)
    for r in accs + ls:
        r[...] = jnp.zeros_like(r)
    nk = k_ref.shape[1] // tk

    def lane_fold(p):
        out = p[:, :LANES]
        for c in range(1, tk // LANES):
            out = out + p[:, c * LANES:(c + 1) * LANES]
        return out

    def load(j):
        start = pl.multiple_of(j * tk, tk)
        return k_ref[0, pl.ds(start, tk), :], v_ref[0, pl.ds(start, tk), :]

    if bounded:
        bound = scal_ref[1]

        def body(j, carry):
            k, v = load(j)
            for a in range(2):
                p = jnp.exp2(_dot_nt(qm[a], k) - bound)
                ls[a][...] += lane_fold(p)
                accs[a][...] += _dot(p.astype(BF16), v)
            return carry

        lax.fori_loop(0, nk, body, 0)
    else:
        def body(j, ms):
            k, v = load(j)
            new_ms = []
            for a in range(2):
                s = _dot_nt(qm[a], k)
                m_new = jnp.maximum(ms[a], jnp.max(s, axis=-1, keepdims=True))
                alpha = jnp.exp2(ms[a] - m_new)
                p = jnp.exp2(s - m_new)
                ls[a][...] = alpha * ls[a][...] + lane_fold(p)
                accs[a][...] = alpha * accs[a][...] + _dot(p.astype(BF16), v)
                new_ms.append(m_new)
            return tuple(new_ms)

        neg = jnp.full((tq, 1), -jnp.inf, F32)
        lax.fori_loop(0, nk, body, (neg, neg))

    lam = scal_ref[0]
    l0 = jnp.sum(l0_ref[...], axis=-1, keepdims=True)
    l1 = jnp.sum(l1_ref[...], axis=-1, keepdims=True)
    o = acc0_ref[...] / l0 - lam * (acc1_ref[...] / l1)
    o = _rms(o, g_ref[...]) * out_scale
    o_ref[0] = o.astype(BF16)


def _diff_attn_t_kernel(scal_ref, q_ref, k_ref, vt_ref, g_ref, o_ref, acc0_ref, acc1_ref, l0_ref, l1_ref, *,
                        tk, out_scale):
    q = q_ref[0]
    lane = lax.broadcasted_iota(jnp.int32, q.shape, 1)
    qm = (jnp.where(lane < DIFF_DH, q, jnp.zeros_like(q)), jnp.where(lane >= DIFF_DH, q, jnp.zeros_like(q)))
    accs = (acc0_ref, acc1_ref)
    ls = (l0_ref, l1_ref)
    for r in accs + ls:
        r[...] = jnp.zeros_like(r)
    nk = k_ref.shape[1] // tk
    tq = q.shape[0]
    bound = scal_ref[1]

    def body(j, carry):
        start = pl.multiple_of(j * tk, tk)
        k = k_ref[0, pl.ds(start, tk), :]
        vt = vt_ref[0, :, pl.ds(start, tk)]
        for a in range(2):
            p = jnp.exp2(_dot_nt(k, qm[a]) - bound)
            ls[a][...] += jnp.sum(p.reshape(tk // 8, 8, tq), axis=0)
            accs[a][...] += _dot(vt, p.astype(BF16))
        return carry

    lax.fori_loop(0, nk, body, 0)
    lam = scal_ref[0]
    l0 = jnp.sum(l0_ref[...], axis=0, keepdims=True)
    l1 = jnp.sum(l1_ref[...], axis=0, keepdims=True)
    ot = acc0_ref[...] / l0 - lam * (acc1_ref[...] / l1)
    ot = ot * lax.rsqrt(jnp.mean(ot * ot, axis=0, keepdims=True) + EPS)
    o_ref[0] = (ot.T * g_ref[...] * out_scale).astype(BF16)


def _diff_attn_t_call(proj, vt, scal, g_sub, out_scale, tq=1024, tk=2048):
    b, s, _ = proj.shape
    tq, tk = min(tq, s), min(tk, s)
    kern = functools.partial(_diff_attn_t_kernel, tk=tk, out_scale=out_scale)
    return pl.pallas_call(
        kern,
        grid=(b, DIFF_HEADS, s // tq),
        in_specs=[
            pl.BlockSpec(memory_space=pltpu.SMEM),
            pl.BlockSpec((1, tq, LANES), lambda bi, h, i: (bi, i, h)),
            pl.BlockSpec((1, s, LANES), lambda bi, h, i: (bi, 0, DIFF_HEADS + h)),
            pl.BlockSpec((1, LANES, s), lambda bi, h, i: (bi, h, 0)),
            pl.BlockSpec((1, LANES), lambda bi, h, i: (0, 0)),
        ],
        out_specs=pl.BlockSpec((1, tq, LANES), lambda bi, h, i: (bi, i, h)),
        out_shape=jax.ShapeDtypeStruct((b, s, HALF_WIDTH), BF16),
        scratch_shapes=[pltpu.VMEM((LANES, tq), F32)] * 2 + [pltpu.VMEM((8, tq), F32)] * 2,
        compiler_params=_cparams(("parallel", "parallel", "arbitrary")),
        name="diff_attn_t",
    )(scal, proj, proj, vt, g_sub)


def _diff_attn(proj, vt, scal, g_sub, out_scale):
    return lax.cond(
        scal[1] <= MAX_SAFE_SCORE_BOUND,
        lambda: _diff_attn_t_call(proj, vt, scal, g_sub, out_scale),
        lambda: _diff_attn_call(proj, scal, g_sub, out_scale, False))


def _diff_attn_call(proj, scal, g_sub, out_scale, bounded, tq=1024, tk=1024):
    b, s, _ = proj.shape
    tq, tk = min(tq, s), min(tk, s)
    kern = functools.partial(_diff_attn_kernel, tk=tk, out_scale=out_scale, bounded=bounded)
    return pl.pallas_call(
        kern,
        grid=(b, DIFF_HEADS, s // tq),
        in_specs=[
            pl.BlockSpec(memory_space=pltpu.SMEM),
            pl.BlockSpec((1, tq, LANES), lambda bi, h, i: (bi, i, h)),
            pl.BlockSpec((1, s, LANES), lambda bi, h, i: (bi, 0, DIFF_HEADS + h)),
            pl.BlockSpec((1, s, LANES), lambda bi, h, i: (bi, 0, 2 * DIFF_HEADS + h)),
            pl.BlockSpec((1, LANES), lambda bi, h, i: (0, 0)),
        ],
        out_specs=pl.BlockSpec((1, tq, LANES), lambda bi, h, i: (bi, i, h)),
        out_shape=jax.ShapeDtypeStruct((b, s, HALF_WIDTH), BF16),
        scratch_shapes=[pltpu.VMEM((tq, LANES), F32)] * 4,
        compiler_params=_cparams(("parallel", "parallel", "arbitrary")),
        name="diff_attn_bounded" if bounded else "diff_attn_online",
    )(scal, proj, proj, proj, g_sub)


NA_ROWS_PER_STEP = 8
NA_BLOCK = NA_ROWS_PER_STEP * GRID_W


def _na_attn_kernel(q_ref, kp_ref, kc_ref, kn_ref, vp_ref, vc_ref, vn_ref, bias_ref, g_ref, o_ref,
                    kbuf, vbuf, *, rows):
    i = pl.program_id(1)
    for n, (kr, vr) in enumerate(((kp_ref, vp_ref), (kc_ref, vc_ref), (kn_ref, vn_ref))):
        kbuf[n * NA_BLOCK:(n + 1) * NA_BLOCK, :] = kr[0]
        vbuf[n * NA_BLOCK:(n + 1) * NA_BLOCK, :] = vr[0]
    lane = lax.broadcasted_iota(jnp.int32, (GRID_W, LANES), 1)
    halves = (lane < NA_DH, lane >= NA_DH)

    def row_body(qr, carry):
        r = i * NA_ROWS_PER_STEP + qr
        r0 = jnp.clip(r - NA_KH // 2, 0, rows - NA_KH)
        start = pl.multiple_of((r0 - (i - 1) * NA_ROWS_PER_STEP) * GRID_W, GRID_W)
        rp = r - r0
        qrow = pl.multiple_of(qr * GRID_W, GRID_W)
        npairs = NA_HEADS // 2
        col = [slice(p * LANES, (p + 1) * LANES) for p in range(npairs)]
        scores = []
        for p in range(npairs):
            qp = q_ref[0, pl.ds(qrow, GRID_W), col[p]]
            kp = kbuf[pl.ds(start, NA_KH * GRID_W), col[p]]
            q2 = jnp.concatenate([jnp.where(h, qp, jnp.zeros_like(qp)) for h in halves], axis=0)
            scores.append(_dot_nt(q2, kp) + bias_ref[p, rp])
        probs = []
        for s in scores:
            m = jnp.max(s, axis=-1, keepdims=True)
            pe = jnp.exp2(s - m)
            probs.append((pe.astype(BF16), jnp.sum(pe, axis=-1, keepdims=True)))
        for p in range(npairs):
            vp = vbuf[pl.ds(start, NA_KH * GRID_W), col[p]]
            pe, l = probs[p]
            o_both = _dot(pe, vp) / l
            o = jnp.where(halves[0], o_both[:GRID_W], o_both[GRID_W:])
            o2 = o * o
            ss = [jnp.sum(jnp.where(h, o2, 0.0), axis=-1, keepdims=True) for h in halves]
            inv = jnp.where(halves[0], lax.rsqrt(ss[0] * (1.0 / NA_DH) + EPS),
                            lax.rsqrt(ss[1] * (1.0 / NA_DH) + EPS))
            o_ref[0, pl.ds(qrow, GRID_W), col[p]] = (o * inv * g_ref[...]).astype(BF16)
        return carry

    lax.fori_loop(0, NA_ROWS_PER_STEP, row_body, 0, unroll=True)


def _na_attn(proj, bias, g_out):
    b, s, _ = proj.shape
    rows = s // GRID_W
    nb = rows // NA_ROWS_PER_STEP
    qcol, kcol, vcol = 3, 4, 5
    blk = (1, NA_BLOCK, HALF_WIDTH)

    def prev(bi, i, c):
        return (bi, jnp.maximum(i - 1, 0), c)

    def nxt(bi, i, c):
        return (bi, jnp.minimum(i + 1, nb - 1), c)

    return pl.pallas_call(
        functools.partial(_na_attn_kernel, rows=rows),
        grid=(b, nb),
        in_specs=[
            pl.BlockSpec(blk, lambda bi, i: (bi, i, qcol)),
            pl.BlockSpec(blk, lambda bi, i: prev(bi, i, kcol)),
            pl.BlockSpec(blk, lambda bi, i: (bi, i, kcol)),
            pl.BlockSpec(blk, lambda bi, i: nxt(bi, i, kcol)),
            pl.BlockSpec(blk, lambda bi, i: prev(bi, i, vcol)),
            pl.BlockSpec(blk, lambda bi, i: (bi, i, vcol)),
            pl.BlockSpec(blk, lambda bi, i: nxt(bi, i, vcol)),
            pl.BlockSpec((NA_HEADS // 2, NA_KH, 2 * GRID_W, NA_KH * GRID_W), lambda bi, i: (0, 0, 0, 0)),
            pl.BlockSpec((1, LANES), lambda bi, i: (0, 0)),
        ],
        out_specs=pl.BlockSpec(blk, lambda bi, i: (bi, i, 0)),
        out_shape=jax.ShapeDtypeStruct((b, s, HALF_WIDTH), BF16),
        scratch_shapes=[pltpu.VMEM((3 * NA_BLOCK, HALF_WIDTH), BF16),
                        pltpu.VMEM((3 * NA_BLOCK, HALF_WIDTH), BF16)],
        compiler_params=_cparams(("parallel", "arbitrary")),
        name="na_attn",
    )(proj, proj, proj, proj, proj, proj, proj, bias, g_out)


def _na_bias_table(rpb):
    qc = np.arange(GRID_W)
    kc = np.arange(GRID_W)
    win = np.clip(qc - NA_KW // 2, 0, GRID_W - NA_KW)
    valid = (kc[None, :] >= win[:, None]) & (kc[None, :] < win[:, None] + NA_KW)
    padded = jnp.pad(rpb.astype(F32) * math.log2(math.e), ((0, 0), (0, 0), (GRID_W, GRID_W)))
    by_col = jnp.stack([padded[:, :, GRID_W + NA_KW - 1 - c:2 * GRID_W + NA_KW - 1 - c] for c in range(GRID_W)],
                       axis=2)
    by_col = jnp.where(jnp.asarray(valid)[None, None], by_col, MASK_VALUE)
    t = jnp.stack([by_col[:, NA_KH - 1 - r:2 * NA_KH - 1 - r] for r in range(NA_KH)], axis=1)
    t = t.reshape(NA_HEADS // 2, 2, NA_KH, NA_KH, GRID_W, GRID_W)
    t = t.transpose(0, 2, 1, 4, 3, 5)
    return t.reshape(NA_HEADS // 2, NA_KH, 2 * GRID_W, NA_KH * GRID_W)


def _out_proj_kernel(x_ref, a_ref, b_ref, w_ref, g_ref, xo_ref, h_ref):
    x = x_ref[...] + _dot(a_ref[...], w_ref[:HALF_WIDTH, :]) + _dot(b_ref[...], w_ref[HALF_WIDTH:, :])
    xo_ref[...] = x
    h_ref[...] = _rms(x, g_ref[...]).astype(BF16)


def _out_proj_router_kernel(x_ref, a_ref, b_ref, w_ref, g_ref, wr_ref, xo_ref, h_ref, c_ref):
    x = x_ref[...] + _dot(a_ref[...], w_ref[:HALF_WIDTH, :]) + _dot(b_ref[...], w_ref[HALF_WIDTH:, :])
    xo_ref[...] = x
    h = _rms(x, g_ref[...])
    h_hi = h.astype(BF16)
    h_ref[...] = h_hi
    h_lo = (h - h_hi.astype(F32)).astype(BF16)
    wr = wr_ref[...]
    both = _dot(h_hi, wr)
    lane = lax.broadcasted_iota(jnp.int32, both.shape, 1).astype(F32)
    lo_part = pltpu.roll(both, LANES - N_EXPERTS, 1)
    logits = both + lo_part + _dot(h_lo, wr)
    logits = jnp.where(lane < N_EXPERTS, logits, -jnp.inf)
    m1 = jnp.max(logits, axis=-1, keepdims=True)
    i1 = jnp.min(jnp.where(logits == m1, lane, float(LANES)), axis=-1, keepdims=True)
    rest = jnp.where(lane == i1, -jnp.inf, logits)
    m2 = jnp.max(rest, axis=-1, keepdims=True)
    i2 = jnp.min(jnp.where(rest == m2, lane, float(LANES)), axis=-1, keepdims=True)
    e2 = jnp.exp(m2 - m1)
    g1 = 1.0 / (1.0 + e2)
    g2 = e2 * g1
    c_ref[...] = jnp.where(lane == i1, g1, jnp.where(lane == i2, g2, 0.0))


def _out_proj(x, a, b, w, g, wr=None, tm=512):
    t = x.shape[0]
    tm = min(tm, t)
    row = lambda i: (i, 0)
    const = lambda i: (0, 0)
    in_specs = [
        pl.BlockSpec((tm, D_MODEL), row),
        pl.BlockSpec((tm, HALF_WIDTH), row),
        pl.BlockSpec((tm, HALF_WIDTH), row),
        pl.BlockSpec((D_MODEL, D_MODEL), const),
        pl.BlockSpec((1, D_MODEL), const),
    ]
    out_specs = [pl.BlockSpec((tm, D_MODEL), row), pl.BlockSpec((tm, D_MODEL), row)]
    out_shape = [jax.ShapeDtypeStruct((t, D_MODEL), F32), jax.ShapeDtypeStruct((t, D_MODEL), BF16)]
    args = [x, a, b, w, g]
    kern = _out_proj_kernel
    if wr is not None:
        in_specs.append(pl.BlockSpec((D_MODEL, LANES), const))
        out_specs.append(pl.BlockSpec((tm, LANES), row))
        out_shape.append(jax.ShapeDtypeStruct((t, LANES), F32))
        args.append(wr)
        kern = _out_proj_router_kernel
    return pl.pallas_call(
        kern,
        grid=(t // tm,),
        in_specs=in_specs,
        out_specs=out_specs,
        out_shape=out_shape,
        compiler_params=_cparams(("parallel",)),
        name="out_proj",
    )(*args)


def _gate_up(h, wgu):
    r = _dot(h, wgu)
    g, u = r[:, :FF_BLOCK], r[:, FF_BLOCK:]
    return (g * jax.nn.sigmoid(g) * u).astype(BF16)


def _swiglu_kernel(x_ref, h_ref, wgu_ref, wd_ref, o_ref, acc_ref):
    f = pl.program_id(1)

    @pl.when(f == 0)
    def _():
        acc_ref[...] = x_ref[...]

    acc_ref[...] += _dot(_gate_up(h_ref[...], wgu_ref[0]), wd_ref[...])

    @pl.when(f == pl.num_programs(1) - 1)
    def _():
        o_ref[...] = acc_ref[...]


def _swiglu(x, h, wgu, wd, tm=1024):
    t = x.shape[0]
    tf = FF_BLOCK
    tm = min(tm, t)
    return pl.pallas_call(
        _swiglu_kernel,
        grid=(t // tm, D_FF // tf),
        in_specs=[
            pl.BlockSpec((tm, D_MODEL), lambda i, f: (i, 0)),
            pl.BlockSpec((tm, D_MODEL), lambda i, f: (i, 0)),
            pl.BlockSpec((1, D_MODEL, 2 * tf), lambda i, f: (f, 0, 0)),
            pl.BlockSpec((tf, D_MODEL), lambda i, f: (f, 0)),
        ],
        out_specs=pl.BlockSpec((tm, D_MODEL), lambda i, f: (i, 0)),
        out_shape=jax.ShapeDtypeStruct((t, D_MODEL), F32),
        scratch_shapes=[pltpu.VMEM((tm, D_MODEL), F32)],
        compiler_params=_cparams(("parallel", "arbitrary")),
        name="swiglu",
    )(x, h, wgu, wd)


MOE_TILE = 1024
MOE_CHUNK = 256
MOE_TAIL = 128
TRI_BLOCK = 512


def _route_kernel(c_ref, ltri_ref, utri_ref, posm_ref, posmt_ref):
    tm = c_ref.shape[0]
    carry = jnp.zeros((1, LANES), F32)
    carry_t = jnp.zeros((LANES, 1), F32)
    for b in range(tm // TRI_BLOCK):
        rows = slice(b * TRI_BLOCK, (b + 1) * TRI_BLOCK)
        sel = c_ref[rows, :] > 0.0
        self = jnp.where(sel, 1.0, 0.0)
        pos = _dot(ltri_ref[...], self.astype(BF16)) + carry
        posm_ref[rows, :] = jnp.where(sel, pos, -1.0)
        sel_t = self.T
        pos_t = _dot(sel_t.astype(BF16), utri_ref[...]) + carry_t
        posmt_ref[0, :, rows] = jnp.where(sel_t > 0.0, pos_t, -1.0)
        carry = carry + jnp.sum(self, axis=0, keepdims=True)
        carry_t = carry_t + jnp.sum(sel_t, axis=1, keepdims=True)


def _route(c, tm):
    t = c.shape[0]
    idx = np.arange(TRI_BLOCK)
    ltri = jnp.asarray(idx[None, :] < idx[:, None], dtype=BF16)
    utri = jnp.asarray(idx[:, None] < idx[None, :], dtype=BF16)
    return pl.pallas_call(
        _route_kernel,
        grid=(t // tm,),
        in_specs=[
            pl.BlockSpec((tm, LANES), lambda i: (i, 0)),
            pl.BlockSpec((TRI_BLOCK, TRI_BLOCK), lambda i: (0, 0)),
            pl.BlockSpec((TRI_BLOCK, TRI_BLOCK), lambda i: (0, 0)),
        ],
        out_specs=[pl.BlockSpec((tm, LANES), lambda i: (i, 0)),
                   pl.BlockSpec((1, LANES, tm), lambda i: (i, 0, 0))],
        out_shape=[jax.ShapeDtypeStruct((t, LANES), F32),
                   jax.ShapeDtypeStruct((t // tm, LANES, tm), F32)],
        compiler_params=_cparams(("parallel",)),
        name="moe_route",
    )(c, ltri, utri)


def _moe_sparse_kernel(x_ref, h_ref, c_ref, posm_ref, posmt_ref, wgu_ref, wd_ref, o_ref, hc_ref, y_ref):
    e = pl.program_id(1)
    f = pl.program_id(2)
    last_f = pl.num_programs(2) - 1
    tm = x_ref.shape[0]

    @pl.when((e == 0) & (f == 0))
    def _():
        o_ref[...] = x_ref[...]

    pos_row = posmt_ref[0, pl.ds(e, 1), :]
    load = (jnp.max(pos_row) + 1.0).astype(jnp.int32)
    rem = load % MOE_CHUNK
    nfull = load // MOE_CHUNK + (rem > MOE_TAIL).astype(jnp.int32)
    has_tail = (rem > 0) & (rem <= MOE_TAIL)
    tail_off = nfull * MOE_CHUNK

    def rows_of(off, size):
        return pl.ds(pl.multiple_of(off, MOE_TAIL), size)

    def for_each_chunk(fn):
        def full(k, carry):
            fn(k * MOE_CHUNK, MOE_CHUNK)
            return carry

        lax.fori_loop(0, nfull, full, 0)

        @pl.when(has_tail)
        def _():
            fn(tail_off, MOE_TAIL)

    def gather(off, size):
        slot = (lax.broadcasted_iota(jnp.int32, (size, tm), 0) + off).astype(F32)
        onehot = jnp.where(pos_row == slot, 1.0, 0.0).astype(BF16)
        hc_ref[rows_of(off, size), :] = _dot(onehot, h_ref[...]).astype(BF16)

    def expert(off, size):
        y = _dot(_gate_up(hc_ref[rows_of(off, size), :], wgu_ref[0, 0]), wd_ref[0])

        @pl.when(f == 0)
        def _():
            y_ref[rows_of(off, size), :] = y

        @pl.when(f > 0)
        def _():
            y_ref[rows_of(off, size), :] += y

    @pl.when(f == 0)
    def _():
        for_each_chunk(gather)

    for_each_chunk(expert)

    @pl.when(f == last_f)
    def _():
        lane = lax.broadcasted_iota(jnp.int32, (tm, LANES), 1)
        pos = jnp.sum(jnp.where(lane == e, posm_ref[...], 0.0), axis=-1, keepdims=True)
        gate = jnp.sum(jnp.where(lane == e, c_ref[...], 0.0), axis=-1, keepdims=True)

        def scatter(off, size):
            slot = (lax.broadcasted_iota(jnp.int32, (tm, size), 1) + off).astype(F32)
            onehot = jnp.where(pos == slot, 1.0, 0.0).astype(BF16)
            o_ref[...] += gate * _dot(onehot, y_ref[rows_of(off, size), :].astype(BF16))

        for_each_chunk(scatter)


def _moe_sparse(x, h, c, wgu, wd, tm=MOE_TILE):
    t = x.shape[0]
    tf = FF_BLOCK
    tm = min(tm, t)
    posm, posmt = _route(c, tm)
    return pl.pallas_call(
        _moe_sparse_kernel,
        grid=(t // tm, N_EXPERTS, D_FF // tf),
        in_specs=[
            pl.BlockSpec((tm, D_MODEL), lambda i, e, f: (i, 0)),
            pl.BlockSpec((tm, D_MODEL), lambda i, e, f: (i, 0)),
            pl.BlockSpec((tm, LANES), lambda i, e, f: (i, 0)),
            pl.BlockSpec((tm, LANES), lambda i, e, f: (i, 0)),
            pl.BlockSpec((1, LANES, tm), lambda i, e, f: (i, 0, 0)),
            pl.BlockSpec((1, 1, D_MODEL, 2 * tf), lambda i, e, f: (e, f, 0, 0)),
            pl.BlockSpec((1, tf, D_MODEL), lambda i, e, f: (e, f, 0)),
        ],
        out_specs=pl.BlockSpec((tm, D_MODEL), lambda i, e, f: (i, 0)),
        out_shape=jax.ShapeDtypeStruct((t, D_MODEL), F32),
        scratch_shapes=[pltpu.VMEM((tm, D_MODEL), BF16), pltpu.VMEM((tm, D_MODEL), F32)],
        compiler_params=_cparams(("parallel", "arbitrary", "arbitrary")),
        name="moe_sparse",
    )(x, h, c, posm, posmt, wgu, wd)


def _rope_tables(seq):
    inv = 1.0 / (ROPE_THETA ** (jnp.arange(0, DIFF_DH, 2, dtype=F32) / DIFF_DH))
    ang = jnp.arange(seq, dtype=F32)[:, None] * inv[None, :]
    ang = jnp.concatenate([ang, ang], axis=-1)
    cos, sin = jnp.cos(ang), jnp.sin(ang)
    sign = jnp.where(jnp.arange(DIFF_DH) < DIFF_DH // 2, -1.0, 1.0).astype(F32)
    return jnp.tile(cos, (1, 2)), jnp.tile(sin * sign, (1, 2))


def _block_diag_ones():
    idx = np.arange(256) // 64
    return jnp.asarray(idx[:, None] == idx[None, :], dtype=BF16)


def _gate_up_blocks(w_gate, w_up):
    def blocks(w):
        return w.astype(BF16).reshape(w.shape[:-1] + (D_FF // FF_BLOCK, FF_BLOCK))

    return jnp.swapaxes(jnp.concatenate([blocks(w_gate), blocks(w_up)], axis=-1), -2, -3)


def _prepare_layer(l, p):
    lam_init = 0.8 - 0.6 * math.exp(-0.3 * l)
    lq = p["lambda_q"][l].astype(F32)
    lk = p["lambda_k"][l].astype(F32)
    lam = jnp.exp(jnp.sum(lq[0] * lk[0])) - jnp.exp(jnp.sum(lq[1] * lk[1])) + lam_init
    scale = DIFF_DH ** -0.5 * math.log2(math.e)
    tile8 = lambda g: jnp.tile(g.astype(F32), HALF_WIDTH // g.shape[0])
    zeros = jnp.zeros((HALF_WIDTH,), F32)
    dq_gain = p["diff_q_norm"][l].astype(F32) * scale
    dk_gain = p["diff_k_norm"][l].astype(F32)
    gains = jnp.stack([
        tile8(dq_gain), tile8(dk_gain), zeros,
        tile8(p["na_q_norm"][l]) * scale, tile8(p["na_k_norm"][l]), zeros, zeros, zeros])
    score_bound = SCORE_BOUND_MARGIN * DIFF_DH * jnp.max(jnp.abs(dq_gain)) * jnp.max(jnp.abs(dk_gain))
    lay = dict(
        scal=jnp.stack([lam, score_bound]).astype(F32),
        out_scale=1.0 - lam_init,
        attn_norm=p["attn_norm"][l].reshape(1, D_MODEL).astype(F32),
        w_in=p["w_in"][l].astype(BF16),
        gains=gains,
        g_sub=p["diff_sub_norm"][l].reshape(1, LANES).astype(F32),
        na_bias=_na_bias_table(p["na_rpb"][l]),
        g_na_out=jnp.tile(p["na_out_norm"][l].astype(F32), 2).reshape(1, LANES),
        w_out=p["w_out"][l].astype(BF16),
        ffn_norm=p["ffn_norm"][l].reshape(1, D_MODEL).astype(F32),
    )
    i = l // 2
    if l % 2 == 0:
        lay.update(wgu=_gate_up_blocks(p["dense_w_gate"][i], p["dense_w_up"][i]),
                   wd=p["dense_w_down"][i].astype(BF16))
    else:
        wr = p["moe_router"][i].astype(F32)
        wr_hi = wr.astype(BF16)
        wr_lo = (wr - wr_hi.astype(F32)).astype(BF16)
        wr2 = jnp.zeros((D_MODEL, LANES), BF16)
        wr2 = wr2.at[:, :N_EXPERTS].set(wr_hi).at[:, N_EXPERTS:2 * N_EXPERTS].set(wr_lo)
        lay.update(wr=wr2, wgu=_gate_up_blocks(p["moe_w_gate"][i], p["moe_w_up"][i]),
                   wd=p["moe_w_down"][i].astype(BF16))
    return lay


def _run_trunk(x, layers, bd):
    b, s, _ = x.shape
    t = b * s
    cos, sin = _rope_tables(s)
    x = x.reshape(t, D_MODEL)
    for l, lay in enumerate(layers):
        proj, vt = _in_proj(x, lay["attn_norm"], lay["w_in"], lay["gains"], cos, sin, bd, s)
        proj3 = proj.reshape(b, s, IN_WIDTH)
        a_out = _diff_attn(proj3, vt, lay["scal"], lay["g_sub"], lay["out_scale"])
        b_out = _na_attn(proj3, lay["na_bias"], lay["g_na_out"])
        a_out = a_out.reshape(t, HALF_WIDTH)
        b_out = b_out.reshape(t, HALF_WIDTH)
        if l % 2 == 0:
            x, h = _out_proj(x, a_out, b_out, lay["w_out"], lay["ffn_norm"])
            x = _swiglu(x, h, lay["wgu"], lay["wd"])
        else:
            x, h, c = _out_proj(x, a_out, b_out, lay["w_out"], lay["ffn_norm"], lay["wr"])
            x = _moe_sparse(x, h, c, lay["wgu"], lay["wd"])
    return x.reshape(b, s, D_MODEL)


def kernel(x_prompt, x_sample, attn_norm, w_in, w_out, diff_q_norm, diff_k_norm, lambda_q, lambda_k, diff_sub_norm, na_q_norm, na_k_norm, na_rpb, na_out_norm, ffn_norm, dense_w_gate, dense_w_up, dense_w_down, moe_router, moe_w_gate, moe_w_up, moe_w_down):
    p = dict(attn_norm=attn_norm, w_in=w_in, w_out=w_out, diff_q_norm=diff_q_norm, diff_k_norm=diff_k_norm,
             lambda_q=lambda_q, lambda_k=lambda_k, diff_sub_norm=diff_sub_norm, na_q_norm=na_q_norm,
             na_k_norm=na_k_norm, na_rpb=na_rpb, na_out_norm=na_out_norm, ffn_norm=ffn_norm,
             dense_w_gate=dense_w_gate, dense_w_up=dense_w_up, dense_w_down=dense_w_down,
             moe_router=moe_router, moe_w_gate=moe_w_gate, moe_w_up=moe_w_up, moe_w_down=moe_w_down)
    layers = [_prepare_layer(l, p) for l in range(DEPTH)]
    bd = _block_diag_ones()
    return (_run_trunk(x_prompt, layers, bd), _run_trunk(x_sample, layers, bd))
```

```python
import functools
import math

import jax
import jax.numpy as jnp
import numpy as np
from jax import lax
from jax.experimental import pallas as pl
from jax.experimental.pallas import tpu as pltpu

D_MODEL = 1024
DEPTH = 4
GRID_W = 64
DIFF_HEADS = 4
DIFF_DH = 64
NA_HEADS = 8
NA_DH = 64
NA_KH = 8
NA_KW = 16
HALF_WIDTH = 512
SEC_DQ, SEC_DK, SEC_DV, SEC_NQ, SEC_NK, SEC_NV = range(6)
IN_WIDTH = 6 * HALF_WIDTH
D_FF = 2816
FF_BLOCK = 1408
N_EXPERTS = 8
ROPE_THETA = 10000.0
EPS = 1e-6
MASK_VALUE = -1e30
SCORE_BOUND_MARGIN = 1.02
MAX_SAFE_SCORE_BOUND = 60.0

LANES = 128
MXU_TILE = 256
VMEM_LIMIT = 56 * 1024 * 1024

BF16 = jnp.bfloat16
F32 = jnp.float32


def _cparams(sem):
    return pltpu.CompilerParams(dimension_semantics=sem, vmem_limit_bytes=VMEM_LIMIT)


def _dot(a, b):
    return jnp.dot(a, b, preferred_element_type=F32)


def _dot_nt(a, b):
    return lax.dot_general(a, b, (((1,), (1,)), ((), ())), preferred_element_type=F32)


def _rms(x, g):
    return x * lax.rsqrt(jnp.mean(x * x, axis=-1, keepdims=True) + EPS) * g


def _group_norm64(y, gain, bd_ref):
    parts = []
    for c in range(HALF_WIDTH // MXU_TILE):
        yc = y[:, c * MXU_TILE:(c + 1) * MXU_TILE]
        ss = _dot((yc * yc).astype(BF16), bd_ref[...])
        parts.append(yc * lax.rsqrt(ss * (1.0 / 64) + EPS))
    return jnp.concatenate(parts, axis=1) * gain


def _rope64(y, cos, sin_signed):
    n = y.shape[1]
    lane = lax.broadcasted_iota(jnp.int32, y.shape, 1)
    first_half = (lane % 64) < 32
    rot = jnp.where(first_half, pltpu.roll(y, n - 32, 1), pltpu.roll(y, 32, 1))
    return y * cos + rot * sin_signed


def _in_proj_kernel(x_ref, g_ref, w_ref, gains_ref, cos_ref, sin_ref, bd_ref, o_ref, vt_ref):
    x = x_ref[...]
    h = _rms(x, g_ref[...]).astype(BF16)
    cos = jnp.concatenate([cos_ref[...]] * (HALF_WIDTH // LANES), axis=1)
    sin = jnp.concatenate([sin_ref[...]] * (HALF_WIDTH // LANES), axis=1)
    for sec in range(6):
        y = _dot(h, w_ref[:, sec * HALF_WIDTH:(sec + 1) * HALF_WIDTH])
        if sec in (SEC_DQ, SEC_DK, SEC_NQ, SEC_NK):
            y = _group_norm64(y, gains_ref[sec:sec + 1, :], bd_ref)
        if sec in (SEC_DQ, SEC_DK):
            y = _rope64(y, cos, sin)
        o_ref[:, sec * HALF_WIDTH:(sec + 1) * HALF_WIDTH] = y.astype(BF16)
        if sec == SEC_DV:
            vt_ref[0] = y.T.astype(BF16)


def _in_proj(x, g, w, gains, cos, sin, bd, seq, tm=512):
    t = x.shape[0]
    tm = min(tm, seq)
    nseq = seq // tm
    return pl.pallas_call(
        _in_proj_kernel,
        grid=(t // tm,),
        in_specs=[
            pl.BlockSpec((tm, D_MODEL), lambda i: (i, 0)),
            pl.BlockSpec((1, D_MODEL), lambda i: (0, 0)),
            pl.BlockSpec((D_MODEL, IN_WIDTH), lambda i: (0, 0)),
            pl.BlockSpec((8, HALF_WIDTH), lambda i: (0, 0)),
            pl.BlockSpec((tm, LANES), lambda i: (i % nseq, 0)),
            pl.BlockSpec((tm, LANES), lambda i: (i % nseq, 0)),
            pl.BlockSpec((MXU_TILE, MXU_TILE), lambda i: (0, 0)),
        ],
        out_specs=[pl.BlockSpec((tm, IN_WIDTH), lambda i: (i, 0)),
                   pl.BlockSpec((1, HALF_WIDTH, tm), lambda i: (i // nseq, 0, i % nseq))],
        out_shape=[jax.ShapeDtypeStruct((t, IN_WIDTH), BF16),
                   jax.ShapeDtypeStruct((t // seq, HALF_WIDTH, seq), BF16)],
        compiler_params=_cparams(("parallel",)),
        name="in_proj",
    )(x, g, w, gains, cos, sin, bd)


def _split_maps(q):
    lane = lax.broadcasted_iota(jnp.int32, q.shape, 1)
    zero = jnp.zeros_like(q)
    return jnp.where(lane < DIFF_DH, q, zero), jnp.where(lane >= DIFF_DH, q, zero)


def _diff_attn_online_kernel(scal_ref, q_ref, k_ref, v_ref, g_ref, o_ref, acc0_ref, acc1_ref, l0_ref, l1_ref, *,
                             tk, out_scale):
    qm = _split_maps(q_ref[0])
    tq = q_ref.shape[1]
    accs = (acc0_ref, acc1_ref)
    ls = (l0_ref, l1_ref)
    for r in accs + ls:
        r[...] = jnp.zeros_like(r)

    def lane_fold(p):
        out = p[:, :LANES]
        for c in range(1, tk // LANES):
            out = out + p[:, c * LANES:(c + 1) * LANES]
        return out

    def body(j, ms):
        start = pl.multiple_of(j * tk, tk)
        k = k_ref[0, pl.ds(start, tk), :]
        v = v_ref[0, pl.ds(start, tk), :]
        new_ms = []
        for a in range(2):
            s = _dot_nt(qm[a], k)
            m_new = jnp.maximum(ms[a], jnp.max(s, axis=-1, keepdims=True))
            alpha = jnp.exp2(ms[a] - m_new)
            p = jnp.exp2(s - m_new)
            ls[a][...] = alpha * ls[a][...] + lane_fold(p)
            accs[a][...] = alpha * accs[a][...] + _dot(p.astype(BF16), v)
            new_ms.append(m_new)
        return tuple(new_ms)

    neg = jnp.full((tq, 1), -jnp.inf, F32)
    lax.fori_loop(0, k_ref.shape[1] // tk, body, (neg, neg))
    lam = scal_ref[0]
    l0 = jnp.sum(l0_ref[...], axis=-1, keepdims=True)
    l1 = jnp.sum(l1_ref[...], axis=-1, keepdims=True)
    o = acc0_ref[...] / l0 - lam * (acc1_ref[...] / l1)
    o_ref[0] = (_rms(o, g_ref[...]) * out_scale).astype(BF16)


def _diff_attn_t_kernel(scal_ref, q_ref, k_ref, vt_ref, g_ref, o_ref, acc0_ref, acc1_ref, l0_ref, l1_ref, *,
                        tk, out_scale):
    qm = _split_maps(q_ref[0])
    accs = (acc0_ref, acc1_ref)
    ls = (l0_ref, l1_ref)
    for r in accs + ls:
        r[...] = jnp.zeros_like(r)
    nk = k_ref.shape[1] // tk
    tq = q_ref.shape[1]
    bound = scal_ref[1]

    def body(j, carry):
        start = pl.multiple_of(j * tk, tk)
        k = k_ref[0, pl.ds(start, tk), :]
        vt = vt_ref[0, :, pl.ds(start, tk)]
        for a in range(2):
            p = jnp.exp2(_dot_nt(k, qm[a]) - bound)
            ls[a][...] += jnp.sum(p.reshape(tk // 8, 8, tq), axis=0)
            accs[a][...] += _dot(vt, p.astype(BF16))
        return carry

    lax.fori_loop(0, nk, body, 0)
    lam = scal_ref[0]
    l0 = jnp.sum(l0_ref[...], axis=0, keepdims=True)
    l1 = jnp.sum(l1_ref[...], axis=0, keepdims=True)
    ot = acc0_ref[...] / l0 - lam * (acc1_ref[...] / l1)
    ot = ot * lax.rsqrt(jnp.mean(ot * ot, axis=0, keepdims=True) + EPS)
    o_ref[0] = (ot.T * g_ref[...] * out_scale).astype(BF16)


DIFF_TQ = 1024
DIFF_TK_BOUNDED = 2048
DIFF_TK_ONLINE = 1024


def _diff_attn_call(proj, vt, scal, g_sub, out_scale, bounded):
    b, s, _ = proj.shape
    tq = min(DIFF_TQ, s)
    q_spec = pl.BlockSpec((1, tq, LANES), lambda bi, h, i: (bi, i, h))
    k_spec = pl.BlockSpec((1, s, LANES), lambda bi, h, i: (bi, 0, DIFF_HEADS + h))
    if bounded:
        kern = functools.partial(_diff_attn_t_kernel, tk=min(DIFF_TK_BOUNDED, s), out_scale=out_scale)
        v, v_spec = vt, pl.BlockSpec((1, LANES, s), lambda bi, h, i: (bi, h, 0))
        scratch = [pltpu.VMEM((LANES, tq), F32)] * 2 + [pltpu.VMEM((8, tq), F32)] * 2
    else:
        kern = functools.partial(_diff_attn_online_kernel, tk=min(DIFF_TK_ONLINE, s), out_scale=out_scale)
        v, v_spec = proj, pl.BlockSpec((1, s, LANES), lambda bi, h, i: (bi, 0, 2 * DIFF_HEADS + h))
        scratch = [pltpu.VMEM((tq, LANES), F32)] * 4
    return pl.pallas_call(
        kern,
        grid=(b, DIFF_HEADS, s // tq),
        in_specs=[pl.BlockSpec(memory_space=pltpu.SMEM), q_spec, k_spec, v_spec,
                  pl.BlockSpec((1, LANES), lambda bi, h, i: (0, 0))],
        out_specs=pl.BlockSpec((1, tq, LANES), lambda bi, h, i: (bi, i, h)),
        out_shape=jax.ShapeDtypeStruct((b, s, HALF_WIDTH), BF16),
        scratch_shapes=scratch,
        compiler_params=_cparams(("parallel", "parallel", "arbitrary")),
        name="diff_attn_t" if bounded else "diff_attn_online",
    )(scal, proj, proj, v, g_sub)


def _diff_attn(proj, vt, scal, g_sub, out_scale):
    return lax.cond(
        scal[1] <= MAX_SAFE_SCORE_BOUND,
        lambda: _diff_attn_call(proj, vt, scal, g_sub, out_scale, True),
        lambda: _diff_attn_call(proj, vt, scal, g_sub, out_scale, False))


NA_ROWS_PER_STEP = 8
NA_BLOCK = NA_ROWS_PER_STEP * GRID_W


def _na_attn_kernel(q_ref, kp_ref, kc_ref, kn_ref, vp_ref, vc_ref, vn_ref, bias_ref, g_ref, o_ref,
                    kbuf, vbuf, *, rows):
    i = pl.program_id(1)
    for n, (kr, vr) in enumerate(((kp_ref, vp_ref), (kc_ref, vc_ref), (kn_ref, vn_ref))):
        kbuf[n * NA_BLOCK:(n + 1) * NA_BLOCK, :] = kr[0]
        vbuf[n * NA_BLOCK:(n + 1) * NA_BLOCK, :] = vr[0]
    lane = lax.broadcasted_iota(jnp.int32, (GRID_W, LANES), 1)
    halves = (lane < NA_DH, lane >= NA_DH)

    def row_body(qr, carry):
        r = i * NA_ROWS_PER_STEP + qr
        r0 = jnp.clip(r - NA_KH // 2, 0, rows - NA_KH)
        start = pl.multiple_of((r0 - (i - 1) * NA_ROWS_PER_STEP) * GRID_W, GRID_W)
        rp = r - r0
        qrow = pl.multiple_of(qr * GRID_W, GRID_W)
        npairs = NA_HEADS // 2
        col = [slice(p * LANES, (p + 1) * LANES) for p in range(npairs)]
        scores = []
        for p in range(npairs):
            qp = q_ref[0, pl.ds(qrow, GRID_W), col[p]]
            kp = kbuf[pl.ds(start, NA_KH * GRID_W), col[p]]
            q2 = jnp.concatenate([jnp.where(h, qp, jnp.zeros_like(qp)) for h in halves], axis=0)
            scores.append(_dot_nt(q2, kp) + bias_ref[p, rp])
        probs = []
        for s in scores:
            m = jnp.max(s, axis=-1, keepdims=True)
            pe = jnp.exp2(s - m)
            probs.append((pe.astype(BF16), jnp.sum(pe, axis=-1, keepdims=True)))
        for p in range(npairs):
            vp = vbuf[pl.ds(start, NA_KH * GRID_W), col[p]]
            pe, l = probs[p]
            o_both = _dot(pe, vp) / l
            o = jnp.where(halves[0], o_both[:GRID_W], o_both[GRID_W:])
            o2 = o * o
            ss = [jnp.sum(jnp.where(h, o2, 0.0), axis=-1, keepdims=True) for h in halves]
            inv = jnp.where(halves[0], lax.rsqrt(ss[0] * (1.0 / NA_DH) + EPS),
                            lax.rsqrt(ss[1] * (1.0 / NA_DH) + EPS))
            o_ref[0, pl.ds(qrow, GRID_W), col[p]] = (o * inv * g_ref[...]).astype(BF16)
        return carry

    lax.fori_loop(0, NA_ROWS_PER_STEP, row_body, 0, unroll=True)


def _na_attn(proj, bias, g_out):
    b, s, _ = proj.shape
    rows = s // GRID_W
    nb = rows // NA_ROWS_PER_STEP
    qcol, kcol, vcol = SEC_NQ, SEC_NK, SEC_NV
    blk = (1, NA_BLOCK, HALF_WIDTH)

    def prev(bi, i, c):
        return (bi, jnp.maximum(i - 1, 0), c)

    def nxt(bi, i, c):
        return (bi, jnp.minimum(i + 1, nb - 1), c)

    return pl.pallas_call(
        functools.partial(_na_attn_kernel, rows=rows),
        grid=(b, nb),
        in_specs=[
            pl.BlockSpec(blk, lambda bi, i: (bi, i, qcol)),
            pl.BlockSpec(blk, lambda bi, i: prev(bi, i, kcol)),
            pl.BlockSpec(blk, lambda bi, i: (bi, i, kcol)),
            pl.BlockSpec(blk, lambda bi, i: nxt(bi, i, kcol)),
            pl.BlockSpec(blk, lambda bi, i: prev(bi, i, vcol)),
            pl.BlockSpec(blk, lambda bi, i: (bi, i, vcol)),
            pl.BlockSpec(blk, lambda bi, i: nxt(bi, i, vcol)),
            pl.BlockSpec((NA_HEADS // 2, NA_KH, 2 * GRID_W, NA_KH * GRID_W), lambda bi, i: (0, 0, 0, 0)),
            pl.BlockSpec((1, LANES), lambda bi, i: (0, 0)),
        ],
        out_specs=pl.BlockSpec(blk, lambda bi, i: (bi, i, 0)),
        out_shape=jax.ShapeDtypeStruct((b, s, HALF_WIDTH), BF16),
        scratch_shapes=[pltpu.VMEM((3 * NA_BLOCK, HALF_WIDTH), BF16),
                        pltpu.VMEM((3 * NA_BLOCK, HALF_WIDTH), BF16)],
        compiler_params=_cparams(("parallel", "arbitrary")),
        name="na_attn",
    )(proj, proj, proj, proj, proj, proj, proj, bias, g_out)


def _na_bias_table(rpb):
    qc = np.arange(GRID_W)
    kc = np.arange(GRID_W)
    win = np.clip(qc - NA_KW // 2, 0, GRID_W - NA_KW)
    valid = (kc[None, :] >= win[:, None]) & (kc[None, :] < win[:, None] + NA_KW)
    padded = jnp.pad(rpb.astype(F32) * math.log2(math.e), ((0, 0), (0, 0), (GRID_W, GRID_W)))
    by_col = jnp.stack([padded[:, :, GRID_W + NA_KW - 1 - c:2 * GRID_W + NA_KW - 1 - c] for c in range(GRID_W)],
                       axis=2)
    by_col = jnp.where(jnp.asarray(valid)[None, None], by_col, MASK_VALUE)
    t = jnp.stack([by_col[:, NA_KH - 1 - r:2 * NA_KH - 1 - r] for r in range(NA_KH)], axis=1)
    t = t.reshape(NA_HEADS // 2, 2, NA_KH, NA_KH, GRID_W, GRID_W)
    t = t.transpose(0, 2, 1, 4, 3, 5)
    return t.reshape(NA_HEADS // 2, NA_KH, 2 * GRID_W, NA_KH * GRID_W)


def _out_proj_kernel(x_ref, a_ref, b_ref, w_ref, g_ref, xo_ref, h_ref):
    x = x_ref[...] + _dot(a_ref[...], w_ref[:HALF_WIDTH, :]) + _dot(b_ref[...], w_ref[HALF_WIDTH:, :])
    xo_ref[...] = x
    h_ref[...] = _rms(x, g_ref[...]).astype(BF16)


def _out_proj_router_kernel(x_ref, a_ref, b_ref, w_ref, g_ref, wr_ref, xo_ref, h_ref, c_ref):
    x = x_ref[...] + _dot(a_ref[...], w_ref[:HALF_WIDTH, :]) + _dot(b_ref[...], w_ref[HALF_WIDTH:, :])
    xo_ref[...] = x
    h = _rms(x, g_ref[...])
    h_hi = h.astype(BF16)
    h_ref[...] = h_hi
    h_lo = (h - h_hi.astype(F32)).astype(BF16)
    wr = wr_ref[...]
    both = _dot(h_hi, wr)
    lane = lax.broadcasted_iota(jnp.int32, both.shape, 1).astype(F32)
    lo_part = pltpu.roll(both, LANES - N_EXPERTS, 1)
    logits = both + lo_part + _dot(h_lo, wr)
    logits = jnp.where(lane < N_EXPERTS, logits, -jnp.inf)
    m1 = jnp.max(logits, axis=-1, keepdims=True)
    i1 = jnp.min(jnp.where(logits == m1, lane, float(LANES)), axis=-1, keepdims=True)
    rest = jnp.where(lane == i1, -jnp.inf, logits)
    m2 = jnp.max(rest, axis=-1, keepdims=True)
    i2 = jnp.min(jnp.where(rest == m2, lane, float(LANES)), axis=-1, keepdims=True)
    e2 = jnp.exp(m2 - m1)
    g1 = 1.0 / (1.0 + e2)
    g2 = e2 * g1
    c_ref[...] = jnp.where(lane == i1, g1, jnp.where(lane == i2, g2, 0.0))


def _out_proj(x, a, b, w, g, wr=None, tm=512):
    t = x.shape[0]
    tm = min(tm, t)
    row = lambda i: (i, 0)
    const = lambda i: (0, 0)
    in_specs = [
        pl.BlockSpec((tm, D_MODEL), row),
        pl.BlockSpec((tm, HALF_WIDTH), row),
        pl.BlockSpec((tm, HALF_WIDTH), row),
        pl.BlockSpec((D_MODEL, D_MODEL), const),
        pl.BlockSpec((1, D_MODEL), const),
    ]
    out_specs = [pl.BlockSpec((tm, D_MODEL), row), pl.BlockSpec((tm, D_MODEL), row)]
    out_shape = [jax.ShapeDtypeStruct((t, D_MODEL), F32), jax.ShapeDtypeStruct((t, D_MODEL), BF16)]
    args = [x, a, b, w, g]
    kern = _out_proj_kernel
    if wr is not None:
        in_specs.append(pl.BlockSpec((D_MODEL, LANES), const))
        out_specs.append(pl.BlockSpec((tm, LANES), row))
        out_shape.append(jax.ShapeDtypeStruct((t, LANES), F32))
        args.append(wr)
        kern = _out_proj_router_kernel
    return pl.pallas_call(
        kern,
        grid=(t // tm,),
        in_specs=in_specs,
        out_specs=out_specs,
        out_shape=out_shape,
        compiler_params=_cparams(("parallel",)),
        name="out_proj",
    )(*args)


def _gate_up(h, wgu):
    r = _dot(h, wgu)
    g, u = r[:, :FF_BLOCK], r[:, FF_BLOCK:]
    return (g * jax.nn.sigmoid(g) * u).astype(BF16)


def _swiglu_kernel(x_ref, h_ref, wgu_ref, wd_ref, o_ref):
    @pl.when(pl.program_id(1) == 0)
    def _():
        o_ref[...] = x_ref[...]

    o_ref[...] += _dot(_gate_up(h_ref[...], wgu_ref[0]), wd_ref[...])


def _swiglu(x, h, wgu, wd, tm=1024):
    t = x.shape[0]
    tf = FF_BLOCK
    tm = min(tm, t)
    return pl.pallas_call(
        _swiglu_kernel,
        grid=(t // tm, D_FF // tf),
        in_specs=[
            pl.BlockSpec((tm, D_MODEL), lambda i, f: (i, 0)),
            pl.BlockSpec((tm, D_MODEL), lambda i, f: (i, 0)),
            pl.BlockSpec((1, D_MODEL, 2 * tf), lambda i, f: (f, 0, 0)),
            pl.BlockSpec((tf, D_MODEL), lambda i, f: (f, 0)),
        ],
        out_specs=pl.BlockSpec((tm, D_MODEL), lambda i, f: (i, 0)),
        out_shape=jax.ShapeDtypeStruct((t, D_MODEL), F32),
        compiler_params=_cparams(("parallel", "arbitrary")),
        name="swiglu",
    )(x, h, wgu, wd)


MOE_TILE = 1024
MOE_CHUNK = 256
MOE_TAIL = 128
TRI_BLOCK = 512


def _route_kernel(c_ref, ltri_ref, utri_ref, posm_ref, posmt_ref):
    tm = c_ref.shape[0]
    carry = jnp.zeros((1, LANES), F32)
    carry_t = jnp.zeros((LANES, 1), F32)
    for b in range(tm // TRI_BLOCK):
        rows = slice(b * TRI_BLOCK, (b + 1) * TRI_BLOCK)
        sel = c_ref[rows, :] > 0.0
        self = jnp.where(sel, 1.0, 0.0)
        pos = _dot(ltri_ref[...], self.astype(BF16)) + carry
        posm_ref[rows, :] = jnp.where(sel, pos, -1.0)
        sel_t = self.T
        pos_t = _dot(sel_t.astype(BF16), utri_ref[...]) + carry_t
        posmt_ref[0, :, rows] = jnp.where(sel_t > 0.0, pos_t, -1.0)
        carry = carry + jnp.sum(self, axis=0, keepdims=True)
        carry_t = carry_t + jnp.sum(sel_t, axis=1, keepdims=True)


def _route(c, tm):
    t = c.shape[0]
    idx = np.arange(TRI_BLOCK)
    ltri = jnp.asarray(idx[None, :] < idx[:, None], dtype=BF16)
    utri = jnp.asarray(idx[:, None] < idx[None, :], dtype=BF16)
    return pl.pallas_call(
        _route_kernel,
        grid=(t // tm,),
        in_specs=[
            pl.BlockSpec((tm, LANES), lambda i: (i, 0)),
            pl.BlockSpec((TRI_BLOCK, TRI_BLOCK), lambda i: (0, 0)),
            pl.BlockSpec((TRI_BLOCK, TRI_BLOCK), lambda i: (0, 0)),
        ],
        out_specs=[pl.BlockSpec((tm, LANES), lambda i: (i, 0)),
                   pl.BlockSpec((1, LANES, tm), lambda i: (i, 0, 0))],
        out_shape=[jax.ShapeDtypeStruct((t, LANES), F32),
                   jax.ShapeDtypeStruct((t // tm, LANES, tm), F32)],
        compiler_params=_cparams(("parallel",)),
        name="moe_route",
    )(c, ltri, utri)


def _moe_sparse_kernel(x_ref, h_ref, c_ref, posm_ref, posmt_ref, wgu_ref, wd_ref, o_ref, hc_ref, y_ref):
    e = pl.program_id(1)
    f = pl.program_id(2)
    last_f = pl.num_programs(2) - 1
    tm = x_ref.shape[0]

    @pl.when((e == 0) & (f == 0))
    def _():
        o_ref[...] = x_ref[...]

    pos_row = posmt_ref[0, pl.ds(e, 1), :]
    load = (jnp.max(pos_row) + 1.0).astype(jnp.int32)
    rem = load % MOE_CHUNK
    nfull = load // MOE_CHUNK + (rem > MOE_TAIL).astype(jnp.int32)
    has_tail = (rem > 0) & (rem <= MOE_TAIL)
    tail_off = nfull * MOE_CHUNK

    def rows_of(off, size):
        return pl.ds(pl.multiple_of(off, MOE_TAIL), size)

    def for_each_chunk(fn):
        def full(k, carry):
            fn(k * MOE_CHUNK, MOE_CHUNK)
            return carry

        lax.fori_loop(0, nfull, full, 0)

        @pl.when(has_tail)
        def _():
            fn(tail_off, MOE_TAIL)

    def gather(off, size):
        slot = (lax.broadcasted_iota(jnp.int32, (size, tm), 0) + off).astype(F32)
        onehot = jnp.where(pos_row == slot, 1.0, 0.0).astype(BF16)
        hc_ref[rows_of(off, size), :] = _dot(onehot, h_ref[...]).astype(BF16)

    def expert(off, size):
        y = _dot(_gate_up(hc_ref[rows_of(off, size), :], wgu_ref[0, 0]), wd_ref[0])

        @pl.when(f == 0)
        def _():
            y_ref[rows_of(off, size), :] = y

        @pl.when(f > 0)
        def _():
            y_ref[rows_of(off, size), :] += y

    @pl.when(f == 0)
    def _():
        for_each_chunk(gather)

    for_each_chunk(expert)

    @pl.when(f == last_f)
    def _():
        lane = lax.broadcasted_iota(jnp.int32, (tm, LANES), 1)
        pos = jnp.sum(jnp.where(lane == e, posm_ref[...], 0.0), axis=-1, keepdims=True)
        gate = jnp.sum(jnp.where(lane == e, c_ref[...], 0.0), axis=-1, keepdims=True)

        def scatter(off, size):
            slot = (lax.broadcasted_iota(jnp.int32, (tm, size), 1) + off).astype(F32)
            onehot = jnp.where(pos == slot, 1.0, 0.0).astype(BF16)
            o_ref[...] += gate * _dot(onehot, y_ref[rows_of(off, size), :].astype(BF16))

        for_each_chunk(scatter)


def _moe_sparse(x, h, c, wgu, wd, tm=MOE_TILE):
    t = x.shape[0]
    tf = FF_BLOCK
    tm = min(tm, t)
    posm, posmt = _route(c, tm)
    return pl.pallas_call(
        _moe_sparse_kernel,
        grid=(t // tm, N_EXPERTS, D_FF // tf),
        in_specs=[
            pl.BlockSpec((tm, D_MODEL), lambda i, e, f: (i, 0)),
            pl.BlockSpec((tm, D_MODEL), lambda i, e, f: (i, 0)),
            pl.BlockSpec((tm, LANES), lambda i, e, f: (i, 0)),
            pl.BlockSpec((tm, LANES), lambda i, e, f: (i, 0)),
            pl.BlockSpec((1, LANES, tm), lambda i, e, f: (i, 0, 0)),
            pl.BlockSpec((1, 1, D_MODEL, 2 * tf), lambda i, e, f: (e, f, 0, 0)),
            pl.BlockSpec((1, tf, D_MODEL), lambda i, e, f: (e, f, 0)),
        ],
        out_specs=pl.BlockSpec((tm, D_MODEL), lambda i, e, f: (i, 0)),
        out_shape=jax.ShapeDtypeStruct((t, D_MODEL), F32),
        scratch_shapes=[pltpu.VMEM((tm, D_MODEL), BF16), pltpu.VMEM((tm, D_MODEL), F32)],
        compiler_params=_cparams(("parallel", "arbitrary", "arbitrary")),
        name="moe_sparse",
    )(x, h, c, posm, posmt, wgu, wd)


def _rope_tables(seq):
    inv = 1.0 / (ROPE_THETA ** (jnp.arange(0, DIFF_DH, 2, dtype=F32) / DIFF_DH))
    ang = jnp.arange(seq, dtype=F32)[:, None] * inv[None, :]
    ang = jnp.concatenate([ang, ang], axis=-1)
    cos, sin = jnp.cos(ang), jnp.sin(ang)
    sign = jnp.where(jnp.arange(DIFF_DH) < DIFF_DH // 2, -1.0, 1.0).astype(F32)
    return jnp.tile(cos, (1, 2)), jnp.tile(sin * sign, (1, 2))


def _block_diag_ones():
    idx = np.arange(MXU_TILE) // DIFF_DH
    return jnp.asarray(idx[:, None] == idx[None, :], dtype=BF16)


def _gate_up_blocks(w_gate, w_up):
    def blocks(w):
        return w.astype(BF16).reshape(w.shape[:-1] + (D_FF // FF_BLOCK, FF_BLOCK))

    return jnp.swapaxes(jnp.concatenate([blocks(w_gate), blocks(w_up)], axis=-1), -2, -3)


def _prepare_layer(l, p):
    lam_init = 0.8 - 0.6 * math.exp(-0.3 * l)
    lq = p["lambda_q"][l].astype(F32)
    lk = p["lambda_k"][l].astype(F32)
    lam = jnp.exp(jnp.sum(lq[0] * lk[0])) - jnp.exp(jnp.sum(lq[1] * lk[1])) + lam_init
    scale = DIFF_DH ** -0.5 * math.log2(math.e)
    tile8 = lambda g: jnp.tile(g.astype(F32), HALF_WIDTH // g.shape[0])
    zeros = jnp.zeros((HALF_WIDTH,), F32)
    dq_gain = p["diff_q_norm"][l].astype(F32) * scale
    dk_gain = p["diff_k_norm"][l].astype(F32)
    gains = jnp.stack([
        tile8(dq_gain), tile8(dk_gain), zeros,
        tile8(p["na_q_norm"][l]) * scale, tile8(p["na_k_norm"][l]), zeros, zeros, zeros])
    score_bound = SCORE_BOUND_MARGIN * DIFF_DH * jnp.max(jnp.abs(dq_gain)) * jnp.max(jnp.abs(dk_gain))
    lay = dict(
        scal=jnp.stack([lam, score_bound]).astype(F32),
        out_scale=1.0 - lam_init,
        attn_norm=p["attn_norm"][l].reshape(1, D_MODEL).astype(F32),
        w_in=p["w_in"][l].astype(BF16),
        gains=gains,
        g_sub=p["diff_sub_norm"][l].reshape(1, LANES).astype(F32),
        na_bias=_na_bias_table(p["na_rpb"][l]),
        g_na_out=jnp.tile(p["na_out_norm"][l].astype(F32), 2).reshape(1, LANES),
        w_out=p["w_out"][l].astype(BF16),
        ffn_norm=p["ffn_norm"][l].reshape(1, D_MODEL).astype(F32),
    )
    i = l // 2
    if l % 2 == 0:
        lay.update(wgu=_gate_up_blocks(p["dense_w_gate"][i], p["dense_w_up"][i]),
                   wd=p["dense_w_down"][i].astype(BF16))
    else:
        wr = p["moe_router"][i].astype(F32)
        wr_hi = wr.astype(BF16)
        wr_lo = (wr - wr_hi.astype(F32)).astype(BF16)
        wr2 = jnp.zeros((D_MODEL, LANES), BF16)
        wr2 = wr2.at[:, :N_EXPERTS].set(wr_hi).at[:, N_EXPERTS:2 * N_EXPERTS].set(wr_lo)
        lay.update(wr=wr2, wgu=_gate_up_blocks(p["moe_w_gate"][i], p["moe_w_up"][i]),
                   wd=p["moe_w_down"][i].astype(BF16))
    return lay


def _run_trunk(x, layers, bd):
    b, s, _ = x.shape
    t = b * s
    cos, sin = _rope_tables(s)
    x = x.reshape(t, D_MODEL)
    for l, lay in enumerate(layers):
        proj, vt = _in_proj(x, lay["attn_norm"], lay["w_in"], lay["gains"], cos, sin, bd, s)
        proj3 = proj.reshape(b, s, IN_WIDTH)
        a_out = _diff_attn(proj3, vt, lay["scal"], lay["g_sub"], lay["out_scale"])
        b_out = _na_attn(proj3, lay["na_bias"], lay["g_na_out"])
        a_out = a_out.reshape(t, HALF_WIDTH)
        b_out = b_out.reshape(t, HALF_WIDTH)
        if l % 2 == 0:
            x, h = _out_proj(x, a_out, b_out, lay["w_out"], lay["ffn_norm"])
            x = _swiglu(x, h, lay["wgu"], lay["wd"])
        else:
            x, h, c = _out_proj(x, a_out, b_out, lay["w_out"], lay["ffn_norm"], lay["wr"])
            x = _moe_sparse(x, h, c, lay["wgu"], lay["wd"])
    return x.reshape(b, s, D_MODEL)


def kernel(x_prompt, x_sample, attn_norm, w_in, w_out, diff_q_norm, diff_k_norm, lambda_q, lambda_k, diff_sub_norm, na_q_norm, na_k_norm, na_rpb, na_out_norm, ffn_norm, dense_w_gate, dense_w_up, dense_w_down, moe_router, moe_w_gate, moe_w_up, moe_w_down):
    p = dict(attn_norm=attn_norm, w_in=w_in, w_out=w_out, diff_q_norm=diff_q_norm, diff_k_norm=diff_k_norm,
             lambda_q=lambda_q, lambda_k=lambda_k, diff_sub_norm=diff_sub_norm, na_q_norm=na_q_norm,
             na_k_norm=na_k_norm, na_rpb=na_rpb, na_out_norm=na_out_norm, ffn_norm=ffn_norm,
             dense_w_gate=dense_w_gate, dense_w_up=dense_w_up, dense_w_down=dense_w_down,
             moe_router=moe_router, moe_w_gate=moe_w_gate, moe_w_up=moe_w_up, moe_w_down=moe_w_down)
    layers = [_prepare_layer(l, p) for l in range(DEPTH)]
    bd = _block_diag_ones()
    return (_run_trunk(x_prompt, layers, bd), _run_trunk(x_sample, layers, bd))
```

```python
import functools
import math

import jax
import jax.numpy as jnp
import numpy as np
from jax import lax
from jax.experimental import pallas as pl
from jax.experimental.pallas import tpu as pltpu

D_MODEL = 1024
DEPTH = 4
GRID_W = 64
DIFF_HEADS = 4
DIFF_DH = 64
NA_HEADS = 8
NA_DH = 64
NA_KH = 8
NA_KW = 16
HALF_WIDTH = 512
SEC_DQ, SEC_DK, SEC_DV, SEC_NQ, SEC_NK, SEC_NV = range(6)
IN_WIDTH = 6 * HALF_WIDTH
D_FF = 2816
FF_BLOCK = 1408
N_EXPERTS = 8
ROPE_THETA = 10000.0
EPS = 1e-6
MASK_VALUE = -1e30
SCORE_BOUND_MARGIN = 1.02
MAX_SAFE_SCORE_BOUND = 60.0

LANES = 128
MXU_TILE = 256
VMEM_LIMIT = 56 * 1024 * 1024

BF16 = jnp.bfloat16
F32 = jnp.float32


def _cparams(sem):
    return pltpu.CompilerParams(dimension_semantics=sem, vmem_limit_bytes=VMEM_LIMIT)


def _dot(a, b):
    return jnp.dot(a, b, preferred_element_type=F32)


def _dot_nt(a, b):
    return lax.dot_general(a, b, (((1,), (1,)), ((), ())), preferred_element_type=F32)


def _rms(x, g):
    return x * lax.rsqrt(jnp.mean(x * x, axis=-1, keepdims=True) + EPS) * g


def _group_norm64(y, gain, bd_ref):
    parts = []
    for c in range(HALF_WIDTH // MXU_TILE):
        yc = y[:, c * MXU_TILE:(c + 1) * MXU_TILE]
        ss = _dot((yc * yc).astype(BF16), bd_ref[...])
        parts.append(yc * lax.rsqrt(ss * (1.0 / 64) + EPS))
    return jnp.concatenate(parts, axis=1) * gain


def _rope64(y, cos, sin_signed):
    n = y.shape[1]
    lane = lax.broadcasted_iota(jnp.int32, y.shape, 1)
    first_half = (lane % 64) < 32
    rot = jnp.where(first_half, pltpu.roll(y, n - 32, 1), pltpu.roll(y, 32, 1))
    return y * cos + rot * sin_signed


def _in_proj_kernel(x_ref, g_ref, w_ref, gains_ref, cos_ref, sin_ref, bd_ref, o_ref, vt_ref):
    x = x_ref[...]
    h = _rms(x, g_ref[...]).astype(BF16)
    cos = jnp.concatenate([cos_ref[...]] * (HALF_WIDTH // LANES), axis=1)
    sin = jnp.concatenate([sin_ref[...]] * (HALF_WIDTH // LANES), axis=1)
    for sec in range(6):
        y = _dot(h, w_ref[:, sec * HALF_WIDTH:(sec + 1) * HALF_WIDTH])
        if sec in (SEC_DQ, SEC_DK, SEC_NQ, SEC_NK):
            y = _group_norm64(y, gains_ref[sec:sec + 1, :], bd_ref)
        if sec in (SEC_DQ, SEC_DK):
            y = _rope64(y, cos, sin)
        o_ref[:, sec * HALF_WIDTH:(sec + 1) * HALF_WIDTH] = y.astype(BF16)
        if sec == SEC_DV:
            vt_ref[0] = y.T.astype(BF16)


def _in_proj(x, g, w, gains, cos, sin, bd, seq, tm=512):
    t = x.shape[0]
    tm = min(tm, seq)
    nseq = seq // tm
    return pl.pallas_call(
        _in_proj_kernel,
        grid=(t // tm,),
        in_specs=[
            pl.BlockSpec((tm, D_MODEL), lambda i: (i, 0)),
            pl.BlockSpec((1, D_MODEL), lambda i: (0, 0)),
            pl.BlockSpec((D_MODEL, IN_WIDTH), lambda i: (0, 0)),
            pl.BlockSpec((8, HALF_WIDTH), lambda i: (0, 0)),
            pl.BlockSpec((tm, LANES), lambda i: (i % nseq, 0)),
            pl.BlockSpec((tm, LANES), lambda i: (i % nseq, 0)),
            pl.BlockSpec((MXU_TILE, MXU_TILE), lambda i: (0, 0)),
        ],
        out_specs=[pl.BlockSpec((tm, IN_WIDTH), lambda i: (i, 0)),
                   pl.BlockSpec((1, HALF_WIDTH, tm), lambda i: (i // nseq, 0, i % nseq))],
        out_shape=[jax.ShapeDtypeStruct((t, IN_WIDTH), BF16),
                   jax.ShapeDtypeStruct((t // seq, HALF_WIDTH, seq), BF16)],
        compiler_params=_cparams(("parallel",)),
        name="in_proj",
    )(x, g, w, gains, cos, sin, bd)


def _split_maps(q):
    lane = lax.broadcasted_iota(jnp.int32, q.shape, 1)
    zero = jnp.zeros_like(q)
    return jnp.where(lane < DIFF_DH, q, zero), jnp.where(lane >= DIFF_DH, q, zero)


def _diff_attn_online_kernel(scal_ref, q_ref, k_ref, v_ref, g_ref, o_ref, acc0_ref, acc1_ref, l0_ref, l1_ref, *,
                             tk, out_scale):
    qm = _split_maps(q_ref[0])
    tq = q_ref.shape[1]
    accs = (acc0_ref, acc1_ref)
    ls = (l0_ref, l1_ref)
    for r in accs + ls:
        r[...] = jnp.zeros_like(r)

    def lane_fold(p):
        out = p[:, :LANES]
        for c in range(1, tk // LANES):
            out = out + p[:, c * LANES:(c + 1) * LANES]
        return out

    def body(j, ms):
        start = pl.multiple_of(j * tk, tk)
        k = k_ref[0, pl.ds(start, tk), :]
        v = v_ref[0, pl.ds(start, tk), :]
        new_ms = []
        for a in range(2):
            s = _dot_nt(qm[a], k)
            m_new = jnp.maximum(ms[a], jnp.max(s, axis=-1, keepdims=True))
            alpha = jnp.exp2(ms[a] - m_new)
            p = jnp.exp2(s - m_new)
            ls[a][...] = alpha * ls[a][...] + lane_fold(p)
            accs[a][...] = alpha * accs[a][...] + _dot(p.astype(BF16), v)
            new_ms.append(m_new)
        return tuple(new_ms)

    neg = jnp.full((tq, 1), -jnp.inf, F32)
    lax.fori_loop(0, k_ref.shape[1] // tk, body, (neg, neg))
    lam = scal_ref[0]
    l0 = jnp.sum(l0_ref[...], axis=-1, keepdims=True)
    l1 = jnp.sum(l1_ref[...], axis=-1, keepdims=True)
    o = acc0_ref[...] / l0 - lam * (acc1_ref[...] / l1)
    o_ref[0] = (_rms(o, g_ref[...]) * out_scale).astype(BF16)


def _diff_attn_t_kernel(scal_ref, q_ref, k_ref, vt_ref, g_ref, o_ref, acc0_ref, acc1_ref, l0_ref, l1_ref, *,
                        tk, out_scale):
    qm = _split_maps(q_ref[0])
    accs = (acc0_ref, acc1_ref)
    ls = (l0_ref, l1_ref)
    for r in accs + ls:
        r[...] = jnp.zeros_like(r)
    nk = k_ref.shape[1] // tk
    tq = q_ref.shape[1]
    bound = scal_ref[1]

    def body(j, carry):
        start = pl.multiple_of(j * tk, tk)
        k = k_ref[0, pl.ds(start, tk), :]
        vt = vt_ref[0, :, pl.ds(start, tk)]
        for a in range(2):
            p = jnp.exp2(_dot_nt(k, qm[a]) - bound)
            ls[a][...] += jnp.sum(p.reshape(tk // 8, 8, tq), axis=0)
            accs[a][...] += _dot(vt, p.astype(BF16))
        return carry

    lax.fori_loop(0, nk, body, 0, unroll=2)
    lam = scal_ref[0]
    l0 = jnp.sum(l0_ref[...], axis=0, keepdims=True)
    l1 = jnp.sum(l1_ref[...], axis=0, keepdims=True)
    ot = acc0_ref[...] / l0 - lam * (acc1_ref[...] / l1)
    ot = ot * lax.rsqrt(jnp.mean(ot * ot, axis=0, keepdims=True) + EPS)
    o_ref[0] = (ot.T * g_ref[...] * out_scale).astype(BF16)


DIFF_TQ = 1024
DIFF_TK_BOUNDED = 2048
DIFF_TK_ONLINE = 1024


def _diff_attn_call(proj, vt, scal, g_sub, out_scale, bounded):
    b, s, _ = proj.shape
    tq = min(DIFF_TQ, s)
    q_spec = pl.BlockSpec((1, tq, LANES), lambda bi, h, i: (bi, i, h))
    k_spec = pl.BlockSpec((1, s, LANES), lambda bi, h, i: (bi, 0, DIFF_HEADS + h))
    if bounded:
        kern = functools.partial(_diff_attn_t_kernel, tk=min(DIFF_TK_BOUNDED, s), out_scale=out_scale)
        v, v_spec = vt, pl.BlockSpec((1, LANES, s), lambda bi, h, i: (bi, h, 0))
        scratch = [pltpu.VMEM((LANES, tq), F32)] * 2 + [pltpu.VMEM((8, tq), F32)] * 2
    else:
        kern = functools.partial(_diff_attn_online_kernel, tk=min(DIFF_TK_ONLINE, s), out_scale=out_scale)
        v, v_spec = proj, pl.BlockSpec((1, s, LANES), lambda bi, h, i: (bi, 0, 2 * DIFF_HEADS + h))
        scratch = [pltpu.VMEM((tq, LANES), F32)] * 4
    return pl.pallas_call(
        kern,
        grid=(b, DIFF_HEADS, s // tq),
        in_specs=[pl.BlockSpec(memory_space=pltpu.SMEM), q_spec, k_spec, v_spec,
                  pl.BlockSpec((1, LANES), lambda bi, h, i: (0, 0))],
        out_specs=pl.BlockSpec((1, tq, LANES), lambda bi, h, i: (bi, i, h)),
        out_shape=jax.ShapeDtypeStruct((b, s, HALF_WIDTH), BF16),
        scratch_shapes=scratch,
        compiler_params=_cparams(("parallel", "parallel", "arbitrary")),
        name="diff_attn_t" if bounded else "diff_attn_online",
    )(scal, proj, proj, v, g_sub)


def _diff_attn(proj, vt, scal, g_sub, out_scale):
    return lax.cond(
        scal[1] <= MAX_SAFE_SCORE_BOUND,
        lambda: _diff_attn_call(proj, vt, scal, g_sub, out_scale, True),
        lambda: _diff_attn_call(proj, vt, scal, g_sub, out_scale, False))


NA_ROWS_PER_STEP = 8
NA_BLOCK = NA_ROWS_PER_STEP * GRID_W


def _na_attn_kernel(q_ref, kp_ref, kc_ref, kn_ref, vp_ref, vc_ref, vn_ref, bias_ref, g_ref, o_ref,
                    kbuf, vbuf, *, rows):
    i = pl.program_id(1)
    for n, (kr, vr) in enumerate(((kp_ref, vp_ref), (kc_ref, vc_ref), (kn_ref, vn_ref))):
        kbuf[n * NA_BLOCK:(n + 1) * NA_BLOCK, :] = kr[0]
        vbuf[n * NA_BLOCK:(n + 1) * NA_BLOCK, :] = vr[0]
    lane = lax.broadcasted_iota(jnp.int32, (GRID_W, LANES), 1)
    halves = (lane < NA_DH, lane >= NA_DH)

    def row_body(qr, carry):
        r = i * NA_ROWS_PER_STEP + qr
        r0 = jnp.clip(r - NA_KH // 2, 0, rows - NA_KH)
        start = pl.multiple_of((r0 - (i - 1) * NA_ROWS_PER_STEP) * GRID_W, GRID_W)
        rp = r - r0
        qrow = pl.multiple_of(qr * GRID_W, GRID_W)
        npairs = NA_HEADS // 2
        col = [slice(p * LANES, (p + 1) * LANES) for p in range(npairs)]
        scores = []
        for p in range(npairs):
            qp = q_ref[0, pl.ds(qrow, GRID_W), col[p]]
            kp = kbuf[pl.ds(start, NA_KH * GRID_W), col[p]]
            q2 = jnp.concatenate([jnp.where(h, qp, jnp.zeros_like(qp)) for h in halves], axis=0)
            scores.append(_dot_nt(q2, kp) + bias_ref[p, rp])
        probs = []
        for s in scores:
            m = jnp.max(s, axis=-1, keepdims=True)
            pe = jnp.exp2(s - m)
            probs.append((pe.astype(BF16), jnp.sum(pe, axis=-1, keepdims=True)))
        for p in range(npairs):
            vp = vbuf[pl.ds(start, NA_KH * GRID_W), col[p]]
            pe, l = probs[p]
            o_both = _dot(pe, vp) / l
            o = jnp.where(halves[0], o_both[:GRID_W], o_both[GRID_W:])
            o2 = o * o
            ss = [jnp.sum(jnp.where(h, o2, 0.0), axis=-1, keepdims=True) for h in halves]
            inv = jnp.where(halves[0], lax.rsqrt(ss[0] * (1.0 / NA_DH) + EPS),
                            lax.rsqrt(ss[1] * (1.0 / NA_DH) + EPS))
            o_ref[0, pl.ds(qrow, GRID_W), col[p]] = (o * inv * g_ref[...]).astype(BF16)
        return carry

    lax.fori_loop(0, NA_ROWS_PER_STEP, row_body, 0, unroll=True)


def _na_attn(proj, bias, g_out):
    b, s, _ = proj.shape
    rows = s // GRID_W
    nb = rows // NA_ROWS_PER_STEP
    qcol, kcol, vcol = SEC_NQ, SEC_NK, SEC_NV
    blk = (1, NA_BLOCK, HALF_WIDTH)

    def prev(bi, i, c):
        return (bi, jnp.maximum(i - 1, 0), c)

    def nxt(bi, i, c):
        return (bi, jnp.minimum(i + 1, nb - 1), c)

    return pl.pallas_call(
        functools.partial(_na_attn_kernel, rows=rows),
        grid=(b, nb),
        in_specs=[
            pl.BlockSpec(blk, lambda bi, i: (bi, i, qcol)),
            pl.BlockSpec(blk, lambda bi, i: prev(bi, i, kcol)),
            pl.BlockSpec(blk, lambda bi, i: (bi, i, kcol)),
            pl.BlockSpec(blk, lambda bi, i: nxt(bi, i, kcol)),
            pl.BlockSpec(blk, lambda bi, i: prev(bi, i, vcol)),
            pl.BlockSpec(blk, lambda bi, i: (bi, i, vcol)),
            pl.BlockSpec(blk, lambda bi, i: nxt(bi, i, vcol)),
            pl.BlockSpec((NA_HEADS // 2, NA_KH, 2 * GRID_W, NA_KH * GRID_W), lambda bi, i: (0, 0, 0, 0)),
            pl.BlockSpec((1, LANES), lambda bi, i: (0, 0)),
        ],
        out_specs=pl.BlockSpec(blk, lambda bi, i: (bi, i, 0)),
        out_shape=jax.ShapeDtypeStruct((b, s, HALF_WIDTH), BF16),
        scratch_shapes=[pltpu.VMEM((3 * NA_BLOCK, HALF_WIDTH), BF16),
                        pltpu.VMEM((3 * NA_BLOCK, HALF_WIDTH), BF16)],
        compiler_params=_cparams(("parallel", "arbitrary")),
        name="na_attn",
    )(proj, proj, proj, proj, proj, proj, proj, bias, g_out)


def _na_bias_table(rpb):
    qc = np.arange(GRID_W)
    kc = np.arange(GRID_W)
    win = np.clip(qc - NA_KW // 2, 0, GRID_W - NA_KW)
    valid = (kc[None, :] >= win[:, None]) & (kc[None, :] < win[:, None] + NA_KW)
    padded = jnp.pad(rpb.astype(F32) * math.log2(math.e), ((0, 0), (0, 0), (GRID_W, GRID_W)))
    by_col = jnp.stack([padded[:, :, GRID_W + NA_KW - 1 - c:2 * GRID_W + NA_KW - 1 - c] for c in range(GRID_W)],
                       axis=2)
    by_col = jnp.where(jnp.asarray(valid)[None, None], by_col, MASK_VALUE)
    t = jnp.stack([by_col[:, NA_KH - 1 - r:2 * NA_KH - 1 - r] for r in range(NA_KH)], axis=1)
    t = t.reshape(NA_HEADS // 2, 2, NA_KH, NA_KH, GRID_W, GRID_W)
    t = t.transpose(0, 2, 1, 4, 3, 5)
    return t.reshape(NA_HEADS // 2, NA_KH, 2 * GRID_W, NA_KH * GRID_W)


def _out_proj_kernel(x_ref, a_ref, b_ref, w_ref, g_ref, xo_ref, h_ref):
    x = x_ref[...] + _dot(a_ref[...], w_ref[:HALF_WIDTH, :]) + _dot(b_ref[...], w_ref[HALF_WIDTH:, :])
    xo_ref[...] = x
    h_ref[...] = _rms(x, g_ref[...]).astype(BF16)


def _out_proj_router_kernel(x_ref, a_ref, b_ref, w_ref, g_ref, wr_ref, xo_ref, h_ref, c_ref):
    x = x_ref[...] + _dot(a_ref[...], w_ref[:HALF_WIDTH, :]) + _dot(b_ref[...], w_ref[HALF_WIDTH:, :])
    xo_ref[...] = x
    h = _rms(x, g_ref[...])
    h_hi = h.astype(BF16)
    h_ref[...] = h_hi
    h_lo = (h - h_hi.astype(F32)).astype(BF16)
    wr = wr_ref[...]
    both = _dot(h_hi, wr)
    lane = lax.broadcasted_iota(jnp.int32, both.shape, 1).astype(F32)
    lo_part = pltpu.roll(both, LANES - N_EXPERTS, 1)
    logits = both + lo_part + _dot(h_lo, wr)
    logits = jnp.where(lane < N_EXPERTS, logits, -jnp.inf)
    m1 = jnp.max(logits, axis=-1, keepdims=True)
    i1 = jnp.min(jnp.where(logits == m1, lane, float(LANES)), axis=-1, keepdims=True)
    rest = jnp.where(lane == i1, -jnp.inf, logits)
    m2 = jnp.max(rest, axis=-1, keepdims=True)
    i2 = jnp.min(jnp.where(rest == m2, lane, float(LANES)), axis=-1, keepdims=True)
    e2 = jnp.exp(m2 - m1)
    g1 = 1.0 / (1.0 + e2)
    g2 = e2 * g1
    c_ref[...] = jnp.where(lane == i1, g1, jnp.where(lane == i2, g2, 0.0))


def _out_proj(x, a, b, w, g, wr=None, tm=512):
    t = x.shape[0]
    tm = min(tm, t)
    row = lambda i: (i, 0)
    const = lambda i: (0, 0)
    in_specs = [
        pl.BlockSpec((tm, D_MODEL), row),
        pl.BlockSpec((tm, HALF_WIDTH), row),
        pl.BlockSpec((tm, HALF_WIDTH), row),
        pl.BlockSpec((D_MODEL, D_MODEL), const),
        pl.BlockSpec((1, D_MODEL), const),
    ]
    out_specs = [pl.BlockSpec((tm, D_MODEL), row), pl.BlockSpec((tm, D_MODEL), row)]
    out_shape = [jax.ShapeDtypeStruct((t, D_MODEL), F32), jax.ShapeDtypeStruct((t, D_MODEL), BF16)]
    args = [x, a, b, w, g]
    kern = _out_proj_kernel
    if wr is not None:
        in_specs.append(pl.BlockSpec((D_MODEL, LANES), const))
        out_specs.append(pl.BlockSpec((tm, LANES), row))
        out_shape.append(jax.ShapeDtypeStruct((t, LANES), F32))
        args.append(wr)
        kern = _out_proj_router_kernel
    return pl.pallas_call(
        kern,
        grid=(t // tm,),
        in_specs=in_specs,
        out_specs=out_specs,
        out_shape=out_shape,
        compiler_params=_cparams(("parallel",)),
        name="out_proj",
    )(*args)


def _gate_up(h, wgu):
    r = _dot(h, wgu)
    g, u = r[:, :FF_BLOCK], r[:, FF_BLOCK:]
    return (g * jax.nn.sigmoid(g) * u).astype(BF16)


def _swiglu_kernel(x_ref, h_ref, wgu_ref, wd_ref, o_ref):
    @pl.when(pl.program_id(1) == 0)
    def _():
        o_ref[...] = x_ref[...]

    o_ref[...] += _dot(_gate_up(h_ref[...], wgu_ref[0]), wd_ref[...])


def _swiglu(x, h, wgu, wd, tm=1024):
    t = x.shape[0]
    tf = FF_BLOCK
    tm = min(tm, t)
    return pl.pallas_call(
        _swiglu_kernel,
        grid=(t // tm, D_FF // tf),
        in_specs=[
            pl.BlockSpec((tm, D_MODEL), lambda i, f: (i, 0)),
            pl.BlockSpec((tm, D_MODEL), lambda i, f: (i, 0)),
            pl.BlockSpec((1, D_MODEL, 2 * tf), lambda i, f: (f, 0, 0)),
            pl.BlockSpec((tf, D_MODEL), lambda i, f: (f, 0)),
        ],
        out_specs=pl.BlockSpec((tm, D_MODEL), lambda i, f: (i, 0)),
        out_shape=jax.ShapeDtypeStruct((t, D_MODEL), F32),
        compiler_params=_cparams(("parallel", "arbitrary")),
        name="swiglu",
    )(x, h, wgu, wd)


MOE_TILE = 1024
MOE_CHUNK = 256
MOE_TAIL = 128
TRI_BLOCK = 512


def _route_kernel(c_ref, ltri_ref, utri_ref, posm_ref, posmt_ref):
    tm = c_ref.shape[0]
    carry = jnp.zeros((1, LANES), F32)
    carry_t = jnp.zeros((LANES, 1), F32)
    for b in range(tm // TRI_BLOCK):
        rows = slice(b * TRI_BLOCK, (b + 1) * TRI_BLOCK)
        sel = c_ref[rows, :] > 0.0
        self = jnp.where(sel, 1.0, 0.0)
        pos = _dot(ltri_ref[...], self.astype(BF16)) + carry
        posm_ref[rows, :] = jnp.where(sel, pos, -1.0)
        sel_t = self.T
        pos_t = _dot(sel_t.astype(BF16), utri_ref[...]) + carry_t
        posmt_ref[0, :, rows] = jnp.where(sel_t > 0.0, pos_t, -1.0)
        carry = carry + jnp.sum(self, axis=0, keepdims=True)
        carry_t = carry_t + jnp.sum(sel_t, axis=1, keepdims=True)


def _route(c, tm):
    t = c.shape[0]
    idx = np.arange(TRI_BLOCK)
    ltri = jnp.asarray(idx[None, :] < idx[:, None], dtype=BF16)
    utri = jnp.asarray(idx[:, None] < idx[None, :], dtype=BF16)
    return pl.pallas_call(
        _route_kernel,
        grid=(t // tm,),
        in_specs=[
            pl.BlockSpec((tm, LANES), lambda i: (i, 0)),
            pl.BlockSpec((TRI_BLOCK, TRI_BLOCK), lambda i: (0, 0)),
            pl.BlockSpec((TRI_BLOCK, TRI_BLOCK), lambda i: (0, 0)),
        ],
        out_specs=[pl.BlockSpec((tm, LANES), lambda i: (i, 0)),
                   pl.BlockSpec((1, LANES, tm), lambda i: (i, 0, 0))],
        out_shape=[jax.ShapeDtypeStruct((t, LANES), F32),
                   jax.ShapeDtypeStruct((t // tm, LANES, tm), F32)],
        compiler_params=_cparams(("parallel",)),
        name="moe_route",
    )(c, ltri, utri)


def _moe_sparse_kernel(x_ref, h_ref, c_ref, posm_ref, posmt_ref, wgu_ref, wd_ref, o_ref, hc_ref, y_ref):
    e = pl.program_id(1)
    f = pl.program_id(2)
    last_f = pl.num_programs(2) - 1
    tm = x_ref.shape[0]

    @pl.when((e == 0) & (f == 0))
    def _():
        o_ref[...] = x_ref[...]

    pos_row = posmt_ref[0, pl.ds(e, 1), :]
    load = (jnp.max(pos_row) + 1.0).astype(jnp.int32)
    rem = load % MOE_CHUNK
    nfull = load // MOE_CHUNK + (rem > MOE_TAIL).astype(jnp.int32)
    has_tail = (rem > 0) & (rem <= MOE_TAIL)
    tail_off = nfull * MOE_CHUNK

    def rows_of(off, size):
        return pl.ds(pl.multiple_of(off, MOE_TAIL), size)

    def for_each_chunk(fn):
        def full(k, carry):
            fn(k * MOE_CHUNK, MOE_CHUNK)
            return carry

        lax.fori_loop(0, nfull, full, 0)

        @pl.when(has_tail)
        def _():
            fn(tail_off, MOE_TAIL)

    def gather(off, size):
        slot = (lax.broadcasted_iota(jnp.int32, (size, tm), 0) + off).astype(F32)
        onehot = jnp.where(pos_row == slot, 1.0, 0.0).astype(BF16)
        hc_ref[rows_of(off, size), :] = _dot(onehot, h_ref[...]).astype(BF16)

    def expert(off, size):
        y = _dot(_gate_up(hc_ref[rows_of(off, size), :], wgu_ref[0, 0]), wd_ref[0])

        @pl.when(f == 0)
        def _():
            y_ref[rows_of(off, size), :] = y

        @pl.when(f > 0)
        def _():
            y_ref[rows_of(off, size), :] += y

    @pl.when(f == 0)
    def _():
        for_each_chunk(gather)

    for_each_chunk(expert)

    @pl.when(f == last_f)
    def _():
        lane = lax.broadcasted_iota(jnp.int32, (tm, LANES), 1)
        pos = jnp.sum(jnp.where(lane == e, posm_ref[...], 0.0), axis=-1, keepdims=True)
        gate = jnp.sum(jnp.where(lane == e, c_ref[...], 0.0), axis=-1, keepdims=True)

        def scatter(off, size):
            slot = (lax.broadcasted_iota(jnp.int32, (tm, size), 1) + off).astype(F32)
            onehot = jnp.where(pos == slot, 1.0, 0.0).astype(BF16)
            o_ref[...] += gate * _dot(onehot, y_ref[rows_of(off, size), :].astype(BF16))

        for_each_chunk(scatter)


def _moe_sparse(x, h, c, wgu, wd, tm=MOE_TILE):
    t = x.shape[0]
    tf = FF_BLOCK
    tm = min(tm, t)
    posm, posmt = _route(c, tm)
    return pl.pallas_call(
        _moe_sparse_kernel,
        grid=(t // tm, N_EXPERTS, D_FF // tf),
        in_specs=[
            pl.BlockSpec((tm, D_MODEL), lambda i, e, f: (i, 0)),
            pl.BlockSpec((tm, D_MODEL), lambda i, e, f: (i, 0)),
            pl.BlockSpec((tm, LANES), lambda i, e, f: (i, 0)),
            pl.BlockSpec((tm, LANES), lambda i, e, f: (i, 0)),
            pl.BlockSpec((1, LANES, tm), lambda i, e, f: (i, 0, 0)),
            pl.BlockSpec((1, 1, D_MODEL, 2 * tf), lambda i, e, f: (e, f, 0, 0)),
            pl.BlockSpec((1, tf, D_MODEL), lambda i, e, f: (e, f, 0)),
        ],
        out_specs=pl.BlockSpec((tm, D_MODEL), lambda i, e, f: (i, 0)),
        out_shape=jax.ShapeDtypeStruct((t, D_MODEL), F32),
        scratch_shapes=[pltpu.VMEM((tm, D_MODEL), BF16), pltpu.VMEM((tm, D_MODEL), F32)],
        compiler_params=_cparams(("parallel", "arbitrary", "arbitrary")),
        name="moe_sparse",
    )(x, h, c, posm, posmt, wgu, wd)


def _rope_tables(seq):
    inv = 1.0 / (ROPE_THETA ** (jnp.arange(0, DIFF_DH, 2, dtype=F32) / DIFF_DH))
    ang = jnp.arange(seq, dtype=F32)[:, None] * inv[None, :]
    ang = jnp.concatenate([ang, ang], axis=-1)
    cos, sin = jnp.cos(ang), jnp.sin(ang)
    sign = jnp.where(jnp.arange(DIFF_DH) < DIFF_DH // 2, -1.0, 1.0).astype(F32)
    return jnp.tile(cos, (1, 2)), jnp.tile(sin * sign, (1, 2))


def _block_diag_ones():
    idx = np.arange(MXU_TILE) // DIFF_DH
    return jnp.asarray(idx[:, None] == idx[None, :], dtype=BF16)


def _gate_up_blocks(w_gate, w_up):
    def blocks(w):
        return w.astype(BF16).reshape(w.shape[:-1] + (D_FF // FF_BLOCK, FF_BLOCK))

    return jnp.swapaxes(jnp.concatenate([blocks(w_gate), blocks(w_up)], axis=-1), -2, -3)


def _prepare_layer(l, p):
    lam_init = 0.8 - 0.6 * math.exp(-0.3 * l)
    lq = p["lambda_q"][l].astype(F32)
    lk = p["lambda_k"][l].astype(F32)
    lam = jnp.exp(jnp.sum(lq[0] * lk[0])) - jnp.exp(jnp.sum(lq[1] * lk[1])) + lam_init
    scale = DIFF_DH ** -0.5 * math.log2(math.e)
    tile8 = lambda g: jnp.tile(g.astype(F32), HALF_WIDTH // g.shape[0])
    zeros = jnp.zeros((HALF_WIDTH,), F32)
    dq_gain = p["diff_q_norm"][l].astype(F32) * scale
    dk_gain = p["diff_k_norm"][l].astype(F32)
    gains = jnp.stack([
        tile8(dq_gain), tile8(dk_gain), zeros,
        tile8(p["na_q_norm"][l]) * scale, tile8(p["na_k_norm"][l]), zeros, zeros, zeros])
    score_bound = SCORE_BOUND_MARGIN * DIFF_DH * jnp.max(jnp.abs(dq_gain)) * jnp.max(jnp.abs(dk_gain))
    lay = dict(
        scal=jnp.stack([lam, score_bound]).astype(F32),
        out_scale=1.0 - lam_init,
        attn_norm=p["attn_norm"][l].reshape(1, D_MODEL).astype(F32),
        w_in=p["w_in"][l].astype(BF16),
        gains=gains,
        g_sub=p["diff_sub_norm"][l].reshape(1, LANES).astype(F32),
        na_bias=_na_bias_table(p["na_rpb"][l]),
        g_na_out=jnp.tile(p["na_out_norm"][l].astype(F32), 2).reshape(1, LANES),
        w_out=p["w_out"][l].astype(BF16),
        ffn_norm=p["ffn_norm"][l].reshape(1, D_MODEL).astype(F32),
    )
    i = l // 2
    if l % 2 == 0:
        lay.update(wgu=_gate_up_blocks(p["dense_w_gate"][i], p["dense_w_up"][i]),
                   wd=p["dense_w_down"][i].astype(BF16))
    else:
        wr = p["moe_router"][i].astype(F32)
        wr_hi = wr.astype(BF16)
        wr_lo = (wr - wr_hi.astype(F32)).astype(BF16)
        wr2 = jnp.zeros((D_MODEL, LANES), BF16)
        wr2 = wr2.at[:, :N_EXPERTS].set(wr_hi).at[:, N_EXPERTS:2 * N_EXPERTS].set(wr_lo)
        lay.update(wr=wr2, wgu=_gate_up_blocks(p["moe_w_gate"][i], p["moe_w_up"][i]),
                   wd=p["moe_w_down"][i].astype(BF16))
    return lay


def _run_trunk(x, layers, bd):
    b, s, _ = x.shape
    t = b * s
    cos, sin = _rope_tables(s)
    x = x.reshape(t, D_MODEL)
    for l, lay in enumerate(layers):
        proj, vt = _in_proj(x, lay["attn_norm"], lay["w_in"], lay["gains"], cos, sin, bd, s)
        proj3 = proj.reshape(b, s, IN_WIDTH)
        a_out = _diff_attn(proj3, vt, lay["scal"], lay["g_sub"], lay["out_scale"])
        b_out = _na_attn(proj3, lay["na_bias"], lay["g_na_out"])
        a_out = a_out.reshape(t, HALF_WIDTH)
        b_out = b_out.reshape(t, HALF_WIDTH)
        if l % 2 == 0:
            x, h = _out_proj(x, a_out, b_out, lay["w_out"], lay["ffn_norm"])
            x = _swiglu(x, h, lay["wgu"], lay["wd"])
        else:
            x, h, c = _out_proj(x, a_out, b_out, lay["w_out"], lay["ffn_norm"], lay["wr"])
            x = _moe_sparse(x, h, c, lay["wgu"], lay["wd"])
    return x.reshape(b, s, D_MODEL)


def kernel(x_prompt, x_sample, attn_norm, w_in, w_out, diff_q_norm, diff_k_norm, lambda_q, lambda_k, diff_sub_norm, na_q_norm, na_k_norm, na_rpb, na_out_norm, ffn_norm, dense_w_gate, dense_w_up, dense_w_down, moe_router, moe_w_gate, moe_w_up, moe_w_down):
    p = dict(attn_norm=attn_norm, w_in=w_in, w_out=w_out, diff_q_norm=diff_q_norm, diff_k_norm=diff_k_norm,
             lambda_q=lambda_q, lambda_k=lambda_k, diff_sub_norm=diff_sub_norm, na_q_norm=na_q_norm,
             na_k_norm=na_k_norm, na_rpb=na_rpb, na_out_norm=na_out_norm, ffn_norm=ffn_norm,
             dense_w_gate=dense_w_gate, dense_w_up=dense_w_up, dense_w_down=dense_w_down,
             moe_router=moe_router, moe_w_gate=moe_w_gate, moe_w_up=moe_w_up, moe_w_down=moe_w_down)
    layers = [_prepare_layer(l, p) for l in range(DEPTH)]
    bd = _block_diag_ones()
    return (_run_trunk(x_prompt, layers, bd), _run_trunk(x_sample, layers, bd))
```

```python
import functools
import math

import jax
import jax.numpy as jnp
import numpy as np
from jax import lax
from jax.experimental import pallas as pl
from jax.experimental.pallas import tpu as pltpu

D_MODEL = 1024
DEPTH = 4
GRID_W = 64
DIFF_HEADS = 4
DIFF_DH = 64
NA_HEADS = 8
NA_DH = 64
NA_KH = 8
NA_KW = 16
HALF_WIDTH = 512
SEC_DQ, SEC_DK, SEC_DV, SEC_NQ, SEC_NK, SEC_NV = range(6)
IN_WIDTH = 6 * HALF_WIDTH
D_FF = 2816
FF_BLOCK = 1408
N_EXPERTS = 8
ROPE_THETA = 10000.0
EPS = 1e-6
MASK_VALUE = -1e30
SCORE_BOUND_MARGIN = 1.02
MAX_SAFE_SCORE_BOUND = 60.0

LANES = 128
MXU_TILE = 256
VMEM_LIMIT = 56 * 1024 * 1024

BF16 = jnp.bfloat16
F32 = jnp.float32


def _cparams(sem):
    return pltpu.CompilerParams(dimension_semantics=sem, vmem_limit_bytes=VMEM_LIMIT)


def _dot(a, b):
    return jnp.dot(a, b, preferred_element_type=F32)


def _dot_nt(a, b):
    return lax.dot_general(a, b, (((1,), (1,)), ((), ())), preferred_element_type=F32)


def _rms(x, g):
    return x * lax.rsqrt(jnp.mean(x * x, axis=-1, keepdims=True) + EPS) * g


def _group_norm64(y, gain, bd_ref):
    parts = []
    for c in range(HALF_WIDTH // MXU_TILE):
        yc = y[:, c * MXU_TILE:(c + 1) * MXU_TILE]
        ss = _dot((yc * yc).astype(BF16), bd_ref[...])
        parts.append(yc * lax.rsqrt(ss * (1.0 / 64) + EPS))
    return jnp.concatenate(parts, axis=1) * gain


def _rope64(y, cos, sin_signed):
    n = y.shape[1]
    lane = lax.broadcasted_iota(jnp.int32, y.shape, 1)
    first_half = (lane % 64) < 32
    rot = jnp.where(first_half, pltpu.roll(y, n - 32, 1), pltpu.roll(y, 32, 1))
    return y * cos + rot * sin_signed


def _in_proj_kernel(x_ref, g_ref, w_ref, gains_ref, cos_ref, sin_ref, bd_ref, o_ref, vt_ref):
    x = x_ref[...]
    h = _rms(x, g_ref[...]).astype(BF16)
    cos = jnp.concatenate([cos_ref[...]] * (HALF_WIDTH // LANES), axis=1)
    sin = jnp.concatenate([sin_ref[...]] * (HALF_WIDTH // LANES), axis=1)
    for sec in range(6):
        y = _dot(h, w_ref[:, sec * HALF_WIDTH:(sec + 1) * HALF_WIDTH])
        if sec in (SEC_DQ, SEC_DK, SEC_NQ, SEC_NK):
            y = _group_norm64(y, gains_ref[sec:sec + 1, :], bd_ref)
        if sec in (SEC_DQ, SEC_DK):
            y = _rope64(y, cos, sin)
        o_ref[:, sec * HALF_WIDTH:(sec + 1) * HALF_WIDTH] = y.astype(BF16)
        if sec == SEC_DV:
            vt_ref[0] = y.T.astype(BF16)


def _in_proj(x, g, w, gains, cos, sin, bd, seq, tm=512):
    t = x.shape[0]
    tm = min(tm, seq)
    nseq = seq // tm
    return pl.pallas_call(
        _in_proj_kernel,
        grid=(t // tm,),
        in_specs=[
            pl.BlockSpec((tm, D_MODEL), lambda i: (i, 0)),
            pl.BlockSpec((1, D_MODEL), lambda i: (0, 0)),
            pl.BlockSpec((D_MODEL, IN_WIDTH), lambda i: (0, 0)),
            pl.BlockSpec((8, HALF_WIDTH), lambda i: (0, 0)),
            pl.BlockSpec((tm, LANES), lambda i: (i % nseq, 0)),
            pl.BlockSpec((tm, LANES), lambda i: (i % nseq, 0)),
            pl.BlockSpec((MXU_TILE, MXU_TILE), lambda i: (0, 0)),
        ],
        out_specs=[pl.BlockSpec((tm, IN_WIDTH), lambda i: (i, 0)),
                   pl.BlockSpec((1, HALF_WIDTH, tm), lambda i: (i // nseq, 0, i % nseq))],
        out_shape=[jax.ShapeDtypeStruct((t, IN_WIDTH), BF16),
                   jax.ShapeDtypeStruct((t // seq, HALF_WIDTH, seq), BF16)],
        compiler_params=_cparams(("parallel",)),
        name="in_proj",
    )(x, g, w, gains, cos, sin, bd)


def _split_maps(q):
    lane = lax.broadcasted_iota(jnp.int32, q.shape, 1)
    zero = jnp.zeros_like(q)
    return jnp.where(lane < DIFF_DH, q, zero), jnp.where(lane >= DIFF_DH, q, zero)


def _diff_attn_online_kernel(scal_ref, q_ref, k_ref, v_ref, g_ref, o_ref, acc0_ref, acc1_ref, l0_ref, l1_ref, *,
                             tk, out_scale):
    qm = _split_maps(q_ref[0])
    tq = q_ref.shape[1]
    accs = (acc0_ref, acc1_ref)
    ls = (l0_ref, l1_ref)
    for r in accs + ls:
        r[...] = jnp.zeros_like(r)

    def lane_fold(p):
        out = p[:, :LANES]
        for c in range(1, tk // LANES):
            out = out + p[:, c * LANES:(c + 1) * LANES]
        return out

    def body(j, ms):
        start = pl.multiple_of(j * tk, tk)
        k = k_ref[0, pl.ds(start, tk), :]
        v = v_ref[0, pl.ds(start, tk), :]
        new_ms = []
        for a in range(2):
            s = _dot_nt(qm[a], k)
            m_new = jnp.maximum(ms[a], jnp.max(s, axis=-1, keepdims=True))
            alpha = jnp.exp2(ms[a] - m_new)
            p = jnp.exp2(s - m_new)
            ls[a][...] = alpha * ls[a][...] + lane_fold(p)
            accs[a][...] = alpha * accs[a][...] + _dot(p.astype(BF16), v)
            new_ms.append(m_new)
        return tuple(new_ms)

    neg = jnp.full((tq, 1), -jnp.inf, F32)
    lax.fori_loop(0, k_ref.shape[1] // tk, body, (neg, neg))
    lam = scal_ref[0]
    l0 = jnp.sum(l0_ref[...], axis=-1, keepdims=True)
    l1 = jnp.sum(l1_ref[...], axis=-1, keepdims=True)
    o = acc0_ref[...] / l0 - lam * (acc1_ref[...] / l1)
    o_ref[0] = (_rms(o, g_ref[...]) * out_scale).astype(BF16)


def _diff_attn_t_kernel(scal_ref, q_ref, k_ref, vt_ref, g_ref, o_ref, acc0_ref, acc1_ref, l0_ref, l1_ref, *,
                        tk, out_scale):
    qm = _split_maps(q_ref[0])
    accs = (acc0_ref, acc1_ref)
    ls = (l0_ref, l1_ref)
    for r in accs + ls:
        r[...] = jnp.zeros_like(r)
    nk = k_ref.shape[1] // tk
    tq = q_ref.shape[1]
    bound = scal_ref[1]

    def body(j, carry):
        start = pl.multiple_of(j * tk, tk)
        k = k_ref[0, pl.ds(start, tk), :]
        vt = vt_ref[0, :, pl.ds(start, tk)]
        for a in range(2):
            p = jnp.exp2(_dot_nt(k, qm[a]) - bound)
            ls[a][...] += jnp.sum(p.reshape(tk // 8, 8, tq), axis=0)
            accs[a][...] += _dot(vt, p.astype(BF16))
        return carry

    lax.fori_loop(0, nk, body, 0, unroll=2)
    lam = scal_ref[0]
    l0 = jnp.sum(l0_ref[...], axis=0, keepdims=True)
    l1 = jnp.sum(l1_ref[...], axis=0, keepdims=True)
    ot = acc0_ref[...] / l0 - lam * (acc1_ref[...] / l1)
    ot = ot * lax.rsqrt(jnp.mean(ot * ot, axis=0, keepdims=True) + EPS)
    o_ref[0] = (ot.T * g_ref[...] * out_scale).astype(BF16)


DIFF_TQ = 1024
DIFF_TK_BOUNDED = 2048
DIFF_TK_ONLINE = 1024


def _diff_attn_call(proj, vt, scal, g_sub, out_scale, bounded):
    b, s, _ = proj.shape
    tq = min(DIFF_TQ, s)
    q_spec = pl.BlockSpec((1, tq, LANES), lambda bi, h, i: (bi, i, h))
    k_spec = pl.BlockSpec((1, s, LANES), lambda bi, h, i: (bi, 0, DIFF_HEADS + h))
    if bounded:
        kern = functools.partial(_diff_attn_t_kernel, tk=min(DIFF_TK_BOUNDED, s), out_scale=out_scale)
        v, v_spec = vt, pl.BlockSpec((1, LANES, s), lambda bi, h, i: (bi, h, 0))
        scratch = [pltpu.VMEM((LANES, tq), F32)] * 2 + [pltpu.VMEM((8, tq), F32)] * 2
    else:
        kern = functools.partial(_diff_attn_online_kernel, tk=min(DIFF_TK_ONLINE, s), out_scale=out_scale)
        v, v_spec = proj, pl.BlockSpec((1, s, LANES), lambda bi, h, i: (bi, 0, 2 * DIFF_HEADS + h))
        scratch = [pltpu.VMEM((tq, LANES), F32)] * 4
    return pl.pallas_call(
        kern,
        grid=(b, DIFF_HEADS, s // tq),
        in_specs=[pl.BlockSpec(memory_space=pltpu.SMEM), q_spec, k_spec, v_spec,
                  pl.BlockSpec((1, LANES), lambda bi, h, i: (0, 0))],
        out_specs=pl.BlockSpec((1, tq, LANES), lambda bi, h, i: (bi, i, h)),
        out_shape=jax.ShapeDtypeStruct((b, s, HALF_WIDTH), BF16),
        scratch_shapes=scratch,
        compiler_params=_cparams(("parallel", "parallel", "arbitrary")),
        name="diff_attn_t" if bounded else "diff_attn_online",
    )(scal, proj, proj, v, g_sub)


def _diff_attn(proj, vt, scal, g_sub, out_scale):
    return lax.cond(
        scal[1] <= MAX_SAFE_SCORE_BOUND,
        lambda: _diff_attn_call(proj, vt, scal, g_sub, out_scale, True),
        lambda: _diff_attn_call(proj, vt, scal, g_sub, out_scale, False))


NA_ROWS_PER_STEP = 8
NA_BLOCK = NA_ROWS_PER_STEP * GRID_W


def _na_attn_kernel(q_ref, kp_ref, kc_ref, kn_ref, vp_ref, vc_ref, vn_ref, bias_ref, g_ref, o_ref,
                    kbuf, vbuf, *, rows):
    i = pl.program_id(1)
    for n, (kr, vr) in enumerate(((kp_ref, vp_ref), (kc_ref, vc_ref), (kn_ref, vn_ref))):
        kbuf[n * NA_BLOCK:(n + 1) * NA_BLOCK, :] = kr[0]
        vbuf[n * NA_BLOCK:(n + 1) * NA_BLOCK, :] = vr[0]
    lane = lax.broadcasted_iota(jnp.int32, (GRID_W, LANES), 1)
    halves = (lane < NA_DH, lane >= NA_DH)

    def row_body(qr, carry):
        r = i * NA_ROWS_PER_STEP + qr
        r0 = jnp.clip(r - NA_KH // 2, 0, rows - NA_KH)
        start = pl.multiple_of((r0 - (i - 1) * NA_ROWS_PER_STEP) * GRID_W, GRID_W)
        rp = r - r0
        qrow = pl.multiple_of(qr * GRID_W, GRID_W)
        npairs = NA_HEADS // 2
        col = [slice(p * LANES, (p + 1) * LANES) for p in range(npairs)]
        scores = []
        for p in range(npairs):
            qp = q_ref[0, pl.ds(qrow, GRID_W), col[p]]
            kp = kbuf[pl.ds(start, NA_KH * GRID_W), col[p]]
            q2 = jnp.concatenate([jnp.where(h, qp, jnp.zeros_like(qp)) for h in halves], axis=0)
            scores.append(_dot_nt(q2, kp) + bias_ref[p, rp])
        probs = []
        for s in scores:
            m = jnp.max(s, axis=-1, keepdims=True)
            pe = jnp.exp2(s - m)
            probs.append((pe.astype(BF16), jnp.sum(pe, axis=-1, keepdims=True)))
        for p in range(npairs):
            vp = vbuf[pl.ds(start, NA_KH * GRID_W), col[p]]
            pe, l = probs[p]
            o_both = _dot(pe, vp) / l
            o = jnp.where(halves[0], o_both[:GRID_W], o_both[GRID_W:])
            o2 = o * o
            ss = [jnp.sum(jnp.where(h, o2, 0.0), axis=-1, keepdims=True) for h in halves]
            inv = jnp.where(halves[0], lax.rsqrt(ss[0] * (1.0 / NA_DH) + EPS),
                            lax.rsqrt(ss[1] * (1.0 / NA_DH) + EPS))
            o_ref[0, pl.ds(qrow, GRID_W), col[p]] = (o * inv * g_ref[...]).astype(BF16)
        return carry

    lax.fori_loop(0, NA_ROWS_PER_STEP, row_body, 0, unroll=True)


def _na_attn(proj, bias, g_out):
    b, s, _ = proj.shape
    rows = s // GRID_W
    nb = rows // NA_ROWS_PER_STEP
    qcol, kcol, vcol = SEC_NQ, SEC_NK, SEC_NV
    blk = (1, NA_BLOCK, HALF_WIDTH)

    def prev(bi, i, c):
        return (bi, jnp.maximum(i - 1, 0), c)

    def nxt(bi, i, c):
        return (bi, jnp.minimum(i + 1, nb - 1), c)

    return pl.pallas_call(
        functools.partial(_na_attn_kernel, rows=rows),
        grid=(b, nb),
        in_specs=[
            pl.BlockSpec(blk, lambda bi, i: (bi, i, qcol)),
            pl.BlockSpec(blk, lambda bi, i: prev(bi, i, kcol)),
            pl.BlockSpec(blk, lambda bi, i: (bi, i, kcol)),
            pl.BlockSpec(blk, lambda bi, i: nxt(bi, i, kcol)),
            pl.BlockSpec(blk, lambda bi, i: prev(bi, i, vcol)),
            pl.BlockSpec(blk, lambda bi, i: (bi, i, vcol)),
            pl.BlockSpec(blk, lambda bi, i: nxt(bi, i, vcol)),
            pl.BlockSpec((NA_HEADS // 2, NA_KH, 2 * GRID_W, NA_KH * GRID_W), lambda bi, i: (0, 0, 0, 0)),
            pl.BlockSpec((1, LANES), lambda bi, i: (0, 0)),
        ],
        out_specs=pl.BlockSpec(blk, lambda bi, i: (bi, i, 0)),
        out_shape=jax.ShapeDtypeStruct((b, s, HALF_WIDTH), BF16),
        scratch_shapes=[pltpu.VMEM((3 * NA_BLOCK, HALF_WIDTH), BF16),
                        pltpu.VMEM((3 * NA_BLOCK, HALF_WIDTH), BF16)],
        compiler_params=_cparams(("parallel", "arbitrary")),
        name="na_attn",
    )(proj, proj, proj, proj, proj, proj, proj, bias, g_out)


def _na_bias_table(rpb):
    qc = np.arange(GRID_W)
    kc = np.arange(GRID_W)
    win = np.clip(qc - NA_KW // 2, 0, GRID_W - NA_KW)
    valid = (kc[None, :] >= win[:, None]) & (kc[None, :] < win[:, None] + NA_KW)
    padded = jnp.pad(rpb.astype(F32) * math.log2(math.e), ((0, 0), (0, 0), (GRID_W, GRID_W)))
    by_col = jnp.stack([padded[:, :, GRID_W + NA_KW - 1 - c:2 * GRID_W + NA_KW - 1 - c] for c in range(GRID_W)],
                       axis=2)
    by_col = jnp.where(jnp.asarray(valid)[None, None], by_col, MASK_VALUE)
    t = jnp.stack([by_col[:, NA_KH - 1 - r:2 * NA_KH - 1 - r] for r in range(NA_KH)], axis=1)
    t = t.reshape(NA_HEADS // 2, 2, NA_KH, NA_KH, GRID_W, GRID_W)
    t = t.transpose(0, 2, 1, 4, 3, 5)
    return t.reshape(NA_HEADS // 2, NA_KH, 2 * GRID_W, NA_KH * GRID_W)


def _out_proj_kernel(x_ref, a_ref, b_ref, w_ref, g_ref, xo_ref, h_ref):
    x = x_ref[...] + _dot(a_ref[...], w_ref[:HALF_WIDTH, :]) + _dot(b_ref[...], w_ref[HALF_WIDTH:, :])
    xo_ref[...] = x
    h_ref[...] = _rms(x, g_ref[...]).astype(BF16)


def _out_proj_router_kernel(x_ref, a_ref, b_ref, w_ref, g_ref, wr_ref, xo_ref, h_ref, c_ref):
    x = x_ref[...] + _dot(a_ref[...], w_ref[:HALF_WIDTH, :]) + _dot(b_ref[...], w_ref[HALF_WIDTH:, :])
    xo_ref[...] = x
    h = _rms(x, g_ref[...])
    h_hi = h.astype(BF16)
    h_ref[...] = h_hi
    h_lo = (h - h_hi.astype(F32)).astype(BF16)
    wr = wr_ref[...]
    both = _dot(h_hi, wr)
    lane = lax.broadcasted_iota(jnp.int32, both.shape, 1).astype(F32)
    lo_part = pltpu.roll(both, LANES - N_EXPERTS, 1)
    logits = both + lo_part + _dot(h_lo, wr)
    logits = jnp.where(lane < N_EXPERTS, logits, -jnp.inf)
    m1 = jnp.max(logits, axis=-1, keepdims=True)
    i1 = jnp.min(jnp.where(logits == m1, lane, float(LANES)), axis=-1, keepdims=True)
    rest = jnp.where(lane == i1, -jnp.inf, logits)
    m2 = jnp.max(rest, axis=-1, keepdims=True)
    i2 = jnp.min(jnp.where(rest == m2, lane, float(LANES)), axis=-1, keepdims=True)
    e2 = jnp.exp(m2 - m1)
    g1 = 1.0 / (1.0 + e2)
    g2 = e2 * g1
    c_ref[...] = jnp.where(lane == i1, g1, jnp.where(lane == i2, g2, 0.0))


def _out_proj(x, a, b, w, g, wr=None, tm=512):
    t = x.shape[0]
    tm = min(tm, t)
    row = lambda i: (i, 0)
    const = lambda i: (0, 0)
    in_specs = [
        pl.BlockSpec((tm, D_MODEL), row),
        pl.BlockSpec((tm, HALF_WIDTH), row),
        pl.BlockSpec((tm, HALF_WIDTH), row),
        pl.BlockSpec((D_MODEL, D_MODEL), const),
        pl.BlockSpec((1, D_MODEL), const),
    ]
    out_specs = [pl.BlockSpec((tm, D_MODEL), row), pl.BlockSpec((tm, D_MODEL), row)]
    out_shape = [jax.ShapeDtypeStruct((t, D_MODEL), F32), jax.ShapeDtypeStruct((t, D_MODEL), BF16)]
    args = [x, a, b, w, g]
    kern = _out_proj_kernel
    if wr is not None:
        in_specs.append(pl.BlockSpec((D_MODEL, LANES), const))
        out_specs.append(pl.BlockSpec((tm, LANES), row))
        out_shape.append(jax.ShapeDtypeStruct((t, LANES), F32))
        args.append(wr)
        kern = _out_proj_router_kernel
    return pl.pallas_call(
        kern,
        grid=(t // tm,),
        in_specs=in_specs,
        out_specs=out_specs,
        out_shape=out_shape,
        compiler_params=_cparams(("parallel",)),
        name="out_proj",
    )(*args)


def _gate_up(h, wgu):
    r = _dot(h, wgu)
    g, u = r[:, :FF_BLOCK], r[:, FF_BLOCK:]
    return (g * jax.nn.sigmoid(g) * u).astype(BF16)


def _swiglu_kernel(x_ref, h_ref, wgu_ref, wd_ref, o_ref):
    @pl.when(pl.program_id(1) == 0)
    def _():
        o_ref[...] = x_ref[...]

    o_ref[...] += _dot(_gate_up(h_ref[...], wgu_ref[...]), wd_ref[...])


def _swiglu(x, h, wgu, wd, tm=1024):
    t = x.shape[0]
    tf = FF_BLOCK
    tm = min(tm, t)
    return pl.pallas_call(
        _swiglu_kernel,
        grid=(t // tm, D_FF // tf),
        in_specs=[
            pl.BlockSpec((tm, D_MODEL), lambda i, f: (i, 0)),
            pl.BlockSpec((tm, D_MODEL), lambda i, f: (i, 0)),
            pl.BlockSpec((D_MODEL, 2 * tf), lambda i, f: (0, f)),
            pl.BlockSpec((tf, D_MODEL), lambda i, f: (f, 0)),
        ],
        out_specs=pl.BlockSpec((tm, D_MODEL), lambda i, f: (i, 0)),
        out_shape=jax.ShapeDtypeStruct((t, D_MODEL), F32),
        compiler_params=_cparams(("parallel", "arbitrary")),
        name="swiglu",
    )(x, h, wgu, wd)


MOE_TILE = 1024
MOE_CHUNK = 256
MOE_TAIL = 128
TRI_BLOCK = 512


def _route_kernel(c_ref, ltri_ref, utri_ref, posm_ref, posmt_ref):
    tm = c_ref.shape[0]
    carry = jnp.zeros((1, LANES), F32)
    carry_t = jnp.zeros((LANES, 1), F32)
    for b in range(tm // TRI_BLOCK):
        rows = slice(b * TRI_BLOCK, (b + 1) * TRI_BLOCK)
        sel = c_ref[rows, :] > 0.0
        self = jnp.where(sel, 1.0, 0.0)
        pos = _dot(ltri_ref[...], self.astype(BF16)) + carry
        posm_ref[rows, :] = jnp.where(sel, pos, -1.0)
        sel_t = self.T
        pos_t = _dot(sel_t.astype(BF16), utri_ref[...]) + carry_t
        posmt_ref[0, :, rows] = jnp.where(sel_t > 0.0, pos_t, -1.0)
        carry = carry + jnp.sum(self, axis=0, keepdims=True)
        carry_t = carry_t + jnp.sum(sel_t, axis=1, keepdims=True)


def _route(c, tm):
    t = c.shape[0]
    idx = np.arange(TRI_BLOCK)
    ltri = jnp.asarray(idx[None, :] < idx[:, None], dtype=BF16)
    utri = jnp.asarray(idx[:, None] < idx[None, :], dtype=BF16)
    return pl.pallas_call(
        _route_kernel,
        grid=(t // tm,),
        in_specs=[
            pl.BlockSpec((tm, LANES), lambda i: (i, 0)),
            pl.BlockSpec((TRI_BLOCK, TRI_BLOCK), lambda i: (0, 0)),
            pl.BlockSpec((TRI_BLOCK, TRI_BLOCK), lambda i: (0, 0)),
        ],
        out_specs=[pl.BlockSpec((tm, LANES), lambda i: (i, 0)),
                   pl.BlockSpec((1, LANES, tm), lambda i: (i, 0, 0))],
        out_shape=[jax.ShapeDtypeStruct((t, LANES), F32),
                   jax.ShapeDtypeStruct((t // tm, LANES, tm), F32)],
        compiler_params=_cparams(("parallel",)),
        name="moe_route",
    )(c, ltri, utri)


def _moe_sparse_kernel(x_ref, h_ref, c_ref, posm_ref, posmt_ref, wgu_ref, wd_ref, o_ref, hc_ref, y_ref):
    e = pl.program_id(1)
    f = pl.program_id(2)
    last_f = pl.num_programs(2) - 1
    tm = x_ref.shape[0]

    @pl.when((e == 0) & (f == 0))
    def _():
        o_ref[...] = x_ref[...]

    pos_row = posmt_ref[0, pl.ds(e, 1), :]
    load = (jnp.max(pos_row) + 1.0).astype(jnp.int32)
    rem = load % MOE_CHUNK
    nfull = load // MOE_CHUNK + (rem > MOE_TAIL).astype(jnp.int32)
    has_tail = (rem > 0) & (rem <= MOE_TAIL)
    tail_off = nfull * MOE_CHUNK

    def rows_of(off, size):
        return pl.ds(pl.multiple_of(off, MOE_TAIL), size)

    def for_each_chunk(fn):
        def full(k, carry):
            fn(k * MOE_CHUNK, MOE_CHUNK)
            return carry

        lax.fori_loop(0, nfull, full, 0)

        @pl.when(has_tail)
        def _():
            fn(tail_off, MOE_TAIL)

    def gather(off, size):
        slot = (lax.broadcasted_iota(jnp.int32, (size, tm), 0) + off).astype(F32)
        onehot = jnp.where(pos_row == slot, 1.0, 0.0).astype(BF16)
        hc_ref[rows_of(off, size), :] = _dot(onehot, h_ref[...]).astype(BF16)

    def expert(off, size):
        y = _dot(_gate_up(hc_ref[rows_of(off, size), :], wgu_ref[0]), wd_ref[0])

        @pl.when(f == 0)
        def _():
            y_ref[rows_of(off, size), :] = y

        @pl.when(f > 0)
        def _():
            y_ref[rows_of(off, size), :] += y

    @pl.when(f == 0)
    def _():
        for_each_chunk(gather)

    for_each_chunk(expert)

    @pl.when(f == last_f)
    def _():
        lane = lax.broadcasted_iota(jnp.int32, (tm, LANES), 1)
        pos = jnp.sum(jnp.where(lane == e, posm_ref[...], 0.0), axis=-1, keepdims=True)
        gate = jnp.sum(jnp.where(lane == e, c_ref[...], 0.0), axis=-1, keepdims=True)

        def scatter(off, size):
            slot = (lax.broadcasted_iota(jnp.int32, (tm, size), 1) + off).astype(F32)
            onehot = jnp.where(pos == slot, 1.0, 0.0).astype(BF16)
            o_ref[...] += gate * _dot(onehot, y_ref[rows_of(off, size), :].astype(BF16))

        for_each_chunk(scatter)


def _moe_sparse(x, h, c, wgu, wd, tm=MOE_TILE):
    t = x.shape[0]
    tf = FF_BLOCK
    tm = min(tm, t)
    posm, posmt = _route(c, tm)
    return pl.pallas_call(
        _moe_sparse_kernel,
        grid=(t // tm, N_EXPERTS, D_FF // tf),
        in_specs=[
            pl.BlockSpec((tm, D_MODEL), lambda i, e, f: (i, 0)),
            pl.BlockSpec((tm, D_MODEL), lambda i, e, f: (i, 0)),
            pl.BlockSpec((tm, LANES), lambda i, e, f: (i, 0)),
            pl.BlockSpec((tm, LANES), lambda i, e, f: (i, 0)),
            pl.BlockSpec((1, LANES, tm), lambda i, e, f: (i, 0, 0)),
            pl.BlockSpec((1, D_MODEL, 2 * tf), lambda i, e, f: (e, 0, f)),
            pl.BlockSpec((1, tf, D_MODEL), lambda i, e, f: (e, f, 0)),
        ],
        out_specs=pl.BlockSpec((tm, D_MODEL), lambda i, e, f: (i, 0)),
        out_shape=jax.ShapeDtypeStruct((t, D_MODEL), F32),
        scratch_shapes=[pltpu.VMEM((tm, D_MODEL), BF16), pltpu.VMEM((tm, D_MODEL), F32)],
        compiler_params=_cparams(("parallel", "arbitrary", "arbitrary")),
        name="moe_sparse",
    )(x, h, c, posm, posmt, wgu, wd)


def _rope_tables(seq):
    inv = 1.0 / (ROPE_THETA ** (jnp.arange(0, DIFF_DH, 2, dtype=F32) / DIFF_DH))
    ang = jnp.arange(seq, dtype=F32)[:, None] * inv[None, :]
    ang = jnp.concatenate([ang, ang], axis=-1)
    cos, sin = jnp.cos(ang), jnp.sin(ang)
    sign = jnp.where(jnp.arange(DIFF_DH) < DIFF_DH // 2, -1.0, 1.0).astype(F32)
    return jnp.tile(cos, (1, 2)), jnp.tile(sin * sign, (1, 2))


def _block_diag_ones():
    idx = np.arange(MXU_TILE) // DIFF_DH
    return jnp.asarray(idx[:, None] == idx[None, :], dtype=BF16)


def _gate_up_blocks(w_gate, w_up):
    pieces = []
    for f in range(D_FF // FF_BLOCK):
        cols = slice(f * FF_BLOCK, (f + 1) * FF_BLOCK)
        pieces += [w_gate[..., cols], w_up[..., cols]]
    return jnp.concatenate(pieces, axis=-1).astype(BF16)


def _prepare_layer(l, p):
    lam_init = 0.8 - 0.6 * math.exp(-0.3 * l)
    lq = p["lambda_q"][l].astype(F32)
    lk = p["lambda_k"][l].astype(F32)
    lam = jnp.exp(jnp.sum(lq[0] * lk[0])) - jnp.exp(jnp.sum(lq[1] * lk[1])) + lam_init
    scale = DIFF_DH ** -0.5 * math.log2(math.e)
    tile8 = lambda g: jnp.tile(g.astype(F32), HALF_WIDTH // g.shape[0])
    zeros = jnp.zeros((HALF_WIDTH,), F32)
    dq_gain = p["diff_q_norm"][l].astype(F32) * scale
    dk_gain = p["diff_k_norm"][l].astype(F32)
    gains = jnp.stack([
        tile8(dq_gain), tile8(dk_gain), zeros,
        tile8(p["na_q_norm"][l]) * scale, tile8(p["na_k_norm"][l]), zeros, zeros, zeros])
    score_bound = SCORE_BOUND_MARGIN * DIFF_DH * jnp.max(jnp.abs(dq_gain)) * jnp.max(jnp.abs(dk_gain))
    lay = dict(
        scal=jnp.stack([lam, score_bound]).astype(F32),
        out_scale=1.0 - lam_init,
        attn_norm=p["attn_norm"][l].reshape(1, D_MODEL).astype(F32),
        w_in=p["w_in"][l].astype(BF16),
        gains=gains,
        g_sub=p["diff_sub_norm"][l].reshape(1, LANES).astype(F32),
        na_bias=_na_bias_table(p["na_rpb"][l]),
        g_na_out=jnp.tile(p["na_out_norm"][l].astype(F32), 2).reshape(1, LANES),
        w_out=p["w_out"][l].astype(BF16),
        ffn_norm=p["ffn_norm"][l].reshape(1, D_MODEL).astype(F32),
    )
    i = l // 2
    if l % 2 == 0:
        lay.update(wgu=_gate_up_blocks(p["dense_w_gate"][i], p["dense_w_up"][i]),
                   wd=p["dense_w_down"][i].astype(BF16))
    else:
        wr = p["moe_router"][i].astype(F32)
        wr_hi = wr.astype(BF16)
        wr_lo = (wr - wr_hi.astype(F32)).astype(BF16)
        wr2 = jnp.zeros((D_MODEL, LANES), BF16)
        wr2 = wr2.at[:, :N_EXPERTS].set(wr_hi).at[:, N_EXPERTS:2 * N_EXPERTS].set(wr_lo)
        lay.update(wr=wr2, wgu=_gate_up_blocks(p["moe_w_gate"][i], p["moe_w_up"][i]),
                   wd=p["moe_w_down"][i].astype(BF16))
    return lay


def _run_trunk(x, layers, bd):
    b, s, _ = x.shape
    t = b * s
    cos, sin = _rope_tables(s)
    x = x.reshape(t, D_MODEL)
    for l, lay in enumerate(layers):
        proj, vt = _in_proj(x, lay["attn_norm"], lay["w_in"], lay["gains"], cos, sin, bd, s)
        proj3 = proj.reshape(b, s, IN_WIDTH)
        a_out = _diff_attn(proj3, vt, lay["scal"], lay["g_sub"], lay["out_scale"])
        b_out = _na_attn(proj3, lay["na_bias"], lay["g_na_out"])
        a_out = a_out.reshape(t, HALF_WIDTH)
        b_out = b_out.reshape(t, HALF_WIDTH)
        if l % 2 == 0:
            x, h = _out_proj(x, a_out, b_out, lay["w_out"], lay["ffn_norm"])
            x = _swiglu(x, h, lay["wgu"], lay["wd"])
        else:
            x, h, c = _out_proj(x, a_out, b_out, lay["w_out"], lay["ffn_norm"], lay["wr"])
            x = _moe_sparse(x, h, c, lay["wgu"], lay["wd"])
    return x.reshape(b, s, D_MODEL)


def kernel(x_prompt, x_sample, attn_norm, w_in, w_out, diff_q_norm, diff_k_norm, lambda_q, lambda_k, diff_sub_norm, na_q_norm, na_k_norm, na_rpb, na_out_norm, ffn_norm, dense_w_gate, dense_w_up, dense_w_down, moe_router, moe_w_gate, moe_w_up, moe_w_down):
    p = dict(attn_norm=attn_norm, w_in=w_in, w_out=w_out, diff_q_norm=diff_q_norm, diff_k_norm=diff_k_norm,
             lambda_q=lambda_q, lambda_k=lambda_k, diff_sub_norm=diff_sub_norm, na_q_norm=na_q_norm,
             na_k_norm=na_k_norm, na_rpb=na_rpb, na_out_norm=na_out_norm, ffn_norm=ffn_norm,
             dense_w_gate=dense_w_gate, dense_w_up=dense_w_up, dense_w_down=dense_w_down,
             moe_router=moe_router, moe_w_gate=moe_w_gate, moe_w_up=moe_w_up, moe_w_down=moe_w_down)
    layers = [_prepare_layer(l, p) for l in range(DEPTH)]
    bd = _block_diag_ones()
    return (_run_trunk(x_prompt, layers, bd), _run_trunk(x_sample, layers, bd))
```

```python
import functools
import math

import jax
import jax.numpy as jnp
import numpy as np
from jax import lax
from jax.experimental import pallas as pl
from jax.experimental.pallas import tpu as pltpu

D_MODEL = 1024
DEPTH = 4
GRID_W = 64
DIFF_HEADS = 4
DIFF_DH = 64
NA_HEADS = 8
NA_DH = 64
NA_KH = 8
NA_KW = 16
HALF_WIDTH = 512
SEC_DQ, SEC_DK, SEC_DV, SEC_NQ, SEC_NK, SEC_NV = range(6)
IN_WIDTH = 6 * HALF_WIDTH
D_FF = 2816
FF_BLOCK = 1408
N_EXPERTS = 8
ROPE_THETA = 10000.0
EPS = 1e-6
MASK_VALUE = -1e30
SCORE_BOUND_MARGIN = 1.02
MAX_SAFE_SCORE_BOUND = 60.0

LANES = 128
MXU_TILE = 256
VMEM_LIMIT = 56 * 1024 * 1024

BF16 = jnp.bfloat16
F32 = jnp.float32


def _cparams(sem):
    return pltpu.CompilerParams(dimension_semantics=sem, vmem_limit_bytes=VMEM_LIMIT)


def _dot(a, b):
    return jnp.dot(a, b, preferred_element_type=F32)


def _dot_nt(a, b):
    return lax.dot_general(a, b, (((1,), (1,)), ((), ())), preferred_element_type=F32)


def _rms(x, g):
    return x * lax.rsqrt(jnp.mean(x * x, axis=-1, keepdims=True) + EPS) * g


def _group_norm64(y, gain, bd_ref):
    parts = []
    for c in range(HALF_WIDTH // MXU_TILE):
        yc = y[:, c * MXU_TILE:(c + 1) * MXU_TILE]
        ss = _dot((yc * yc).astype(BF16), bd_ref[...])
        parts.append(yc * lax.rsqrt(ss * (1.0 / 64) + EPS))
    return jnp.concatenate(parts, axis=1) * gain


def _rope64(y, cos, sin_signed):
    n = y.shape[1]
    lane = lax.broadcasted_iota(jnp.int32, y.shape, 1)
    first_half = (lane % 64) < 32
    rot = jnp.where(first_half, pltpu.roll(y, n - 32, 1), pltpu.roll(y, 32, 1))
    return y * cos + rot * sin_signed


def _in_proj_kernel(x_ref, g_ref, w_ref, gains_ref, cos_ref, sin_ref, bd_ref, o_ref, vt_ref):
    x = x_ref[...]
    h = _rms(x, g_ref[...]).astype(BF16)
    cos = jnp.concatenate([cos_ref[...]] * (HALF_WIDTH // LANES), axis=1)
    sin = jnp.concatenate([sin_ref[...]] * (HALF_WIDTH // LANES), axis=1)
    for sec in range(6):
        y = _dot(h, w_ref[:, sec * HALF_WIDTH:(sec + 1) * HALF_WIDTH])
        if sec in (SEC_DQ, SEC_DK, SEC_NQ, SEC_NK):
            y = _group_norm64(y, gains_ref[sec:sec + 1, :], bd_ref)
        if sec in (SEC_DQ, SEC_DK):
            y = _rope64(y, cos, sin)
        o_ref[:, sec * HALF_WIDTH:(sec + 1) * HALF_WIDTH] = y.astype(BF16)
        if sec == SEC_DV:
            vt_ref[0] = y.T.astype(BF16)


def _in_proj(x, g, w, gains, cos, sin, bd, seq, tm=512):
    t = x.shape[0]
    tm = min(tm, seq)
    nseq = seq // tm
    return pl.pallas_call(
        _in_proj_kernel,
        grid=(t // tm,),
        in_specs=[
            pl.BlockSpec((tm, D_MODEL), lambda i: (i, 0)),
            pl.BlockSpec((1, D_MODEL), lambda i: (0, 0)),
            pl.BlockSpec((D_MODEL, IN_WIDTH), lambda i: (0, 0)),
            pl.BlockSpec((8, HALF_WIDTH), lambda i: (0, 0)),
            pl.BlockSpec((tm, LANES), lambda i: (i % nseq, 0)),
            pl.BlockSpec((tm, LANES), lambda i: (i % nseq, 0)),
            pl.BlockSpec((MXU_TILE, MXU_TILE), lambda i: (0, 0)),
        ],
        out_specs=[pl.BlockSpec((tm, IN_WIDTH), lambda i: (i, 0)),
                   pl.BlockSpec((1, HALF_WIDTH, tm), lambda i: (i // nseq, 0, i % nseq))],
        out_shape=[jax.ShapeDtypeStruct((t, IN_WIDTH), BF16),
                   jax.ShapeDtypeStruct((t // seq, HALF_WIDTH, seq), BF16)],
        compiler_params=_cparams(("parallel",)),
        name="in_proj",
    )(x, g, w, gains, cos, sin, bd)


def _split_maps(q):
    lane = lax.broadcasted_iota(jnp.int32, q.shape, 1)
    zero = jnp.zeros_like(q)
    return jnp.where(lane < DIFF_DH, q, zero), jnp.where(lane >= DIFF_DH, q, zero)


def _diff_attn_online_kernel(scal_ref, q_ref, k_ref, v_ref, g_ref, o_ref, acc0_ref, acc1_ref, l0_ref, l1_ref, *,
                             tk, out_scale):
    qm = _split_maps(q_ref[0])
    tq = q_ref.shape[1]
    accs = (acc0_ref, acc1_ref)
    ls = (l0_ref, l1_ref)
    for r in accs + ls:
        r[...] = jnp.zeros_like(r)

    def lane_fold(p):
        out = p[:, :LANES]
        for c in range(1, tk // LANES):
            out = out + p[:, c * LANES:(c + 1) * LANES]
        return out

    def body(j, ms):
        start = pl.multiple_of(j * tk, tk)
        k = k_ref[0, pl.ds(start, tk), :]
        v = v_ref[0, pl.ds(start, tk), :]
        new_ms = []
        for a in range(2):
            s = _dot_nt(qm[a], k)
            m_new = jnp.maximum(ms[a], jnp.max(s, axis=-1, keepdims=True))
            alpha = jnp.exp2(ms[a] - m_new)
            p = jnp.exp2(s - m_new)
            ls[a][...] = alpha * ls[a][...] + lane_fold(p)
            accs[a][...] = alpha * accs[a][...] + _dot(p.astype(BF16), v)
            new_ms.append(m_new)
        return tuple(new_ms)

    neg = jnp.full((tq, 1), -jnp.inf, F32)
    lax.fori_loop(0, k_ref.shape[1] // tk, body, (neg, neg))
    lam = scal_ref[0]
    l0 = jnp.sum(l0_ref[...], axis=-1, keepdims=True)
    l1 = jnp.sum(l1_ref[...], axis=-1, keepdims=True)
    o = acc0_ref[...] / l0 - lam * (acc1_ref[...] / l1)
    o_ref[0] = (_rms(o, g_ref[...]) * out_scale).astype(BF16)


def _diff_attn_t_kernel(scal_ref, q_ref, k_ref, vt_ref, g_ref, o_ref, acc0_ref, acc1_ref, l0_ref, l1_ref, *,
                        tk, out_scale):
    qm = _split_maps(q_ref[0])
    accs = (acc0_ref, acc1_ref)
    ls = (l0_ref, l1_ref)
    for r in accs + ls:
        r[...] = jnp.zeros_like(r)
    nk = k_ref.shape[1] // tk
    tq = q_ref.shape[1]
    bound = scal_ref[1]

    def body(j, carry):
        start = pl.multiple_of(j * tk, tk)
        k = k_ref[0, pl.ds(start, tk), :]
        vt = vt_ref[0, :, pl.ds(start, tk)]
        for a in range(2):
            p = jnp.exp2(_dot_nt(k, qm[a]) - bound)
            ls[a][...] += jnp.sum(p.reshape(tk // 8, 8, tq), axis=0)
            accs[a][...] += _dot(vt, p.astype(BF16))
        return carry

    lax.fori_loop(0, nk, body, 0, unroll=2)
    lam = scal_ref[0]
    l0 = jnp.sum(l0_ref[...], axis=0, keepdims=True)
    l1 = jnp.sum(l1_ref[...], axis=0, keepdims=True)
    ot = acc0_ref[...] / l0 - lam * (acc1_ref[...] / l1)
    ot = ot * lax.rsqrt(jnp.mean(ot * ot, axis=0, keepdims=True) + EPS)
    o_ref[0] = (ot.T * g_ref[...] * out_scale).astype(BF16)


DIFF_TQ = 1024
DIFF_TK_BOUNDED = 2048
DIFF_TK_ONLINE = 1024


def _diff_attn_call(proj, vt, scal, g_sub, out_scale, bounded):
    b, s, _ = proj.shape
    tq = min(DIFF_TQ, s)
    q_spec = pl.BlockSpec((1, tq, LANES), lambda bi, h, i: (bi, i, h))
    k_spec = pl.BlockSpec((1, s, LANES), lambda bi, h, i: (bi, 0, DIFF_HEADS + h))
    if bounded:
        kern = functools.partial(_diff_attn_t_kernel, tk=min(DIFF_TK_BOUNDED, s), out_scale=out_scale)
        v, v_spec = vt, pl.BlockSpec((1, LANES, s), lambda bi, h, i: (bi, h, 0))
        scratch = [pltpu.VMEM((LANES, tq), F32)] * 2 + [pltpu.VMEM((8, tq), F32)] * 2
    else:
        kern = functools.partial(_diff_attn_online_kernel, tk=min(DIFF_TK_ONLINE, s), out_scale=out_scale)
        v, v_spec = proj, pl.BlockSpec((1, s, LANES), lambda bi, h, i: (bi, 0, 2 * DIFF_HEADS + h))
        scratch = [pltpu.VMEM((tq, LANES), F32)] * 4
    return pl.pallas_call(
        kern,
        grid=(b, DIFF_HEADS, s // tq),
        in_specs=[pl.BlockSpec(memory_space=pltpu.SMEM), q_spec, k_spec, v_spec,
                  pl.BlockSpec((1, LANES), lambda bi, h, i: (0, 0))],
        out_specs=pl.BlockSpec((1, tq, LANES), lambda bi, h, i: (bi, i, h)),
        out_shape=jax.ShapeDtypeStruct((b, s, HALF_WIDTH), BF16),
        scratch_shapes=scratch,
        compiler_params=_cparams(("parallel", "parallel", "arbitrary")),
        name="diff_attn_t" if bounded else "diff_attn_online",
    )(scal, proj, proj, v, g_sub)


def _diff_attn(proj, vt, scal, g_sub, out_scale):
    return lax.cond(
        scal[1] <= MAX_SAFE_SCORE_BOUND,
        lambda: _diff_attn_call(proj, vt, scal, g_sub, out_scale, True),
        lambda: _diff_attn_call(proj, vt, scal, g_sub, out_scale, False))


NA_ROWS_PER_STEP = 8
NA_BLOCK = NA_ROWS_PER_STEP * GRID_W


def _na_attn_kernel(q_ref, kp_ref, kc_ref, kn_ref, vp_ref, vc_ref, vn_ref, bias_ref, g_ref, o_ref,
                    kbuf, vbuf, *, rows):
    i = pl.program_id(1)
    for n, (kr, vr) in enumerate(((kp_ref, vp_ref), (kc_ref, vc_ref), (kn_ref, vn_ref))):
        kbuf[n * NA_BLOCK:(n + 1) * NA_BLOCK, :] = kr[0]
        vbuf[n * NA_BLOCK:(n + 1) * NA_BLOCK, :] = vr[0]
    lane = lax.broadcasted_iota(jnp.int32, (GRID_W, LANES), 1)
    halves = (lane < NA_DH, lane >= NA_DH)

    def row_body(qr, carry):
        r = i * NA_ROWS_PER_STEP + qr
        r0 = jnp.clip(r - NA_KH // 2, 0, rows - NA_KH)
        start = pl.multiple_of((r0 - (i - 1) * NA_ROWS_PER_STEP) * GRID_W, GRID_W)
        shift = (NA_KH - 1) - (r - r0)
        odd = shift % 2
        bias_at = pl.ds(pl.multiple_of((shift - odd) * GRID_W, LANES), NA_KH * GRID_W)
        qrow = pl.multiple_of(qr * GRID_W, GRID_W)
        npairs = NA_HEADS // 2
        col = [slice(p * LANES, (p + 1) * LANES) for p in range(npairs)]
        scores = []
        for p in range(npairs):
            qp = q_ref[0, pl.ds(qrow, GRID_W), col[p]]
            kp = kbuf[pl.ds(start, NA_KH * GRID_W), col[p]]
            q2 = jnp.concatenate([jnp.where(h, qp, jnp.zeros_like(qp)) for h in halves], axis=0)
            scores.append(_dot_nt(q2, kp) + bias_ref[p, odd, :, bias_at])
        probs = []
        for s in scores:
            m = jnp.max(s, axis=-1, keepdims=True)
            pe = jnp.exp2(s - m)
            probs.append((pe.astype(BF16), jnp.sum(pe, axis=-1, keepdims=True)))
        for p in range(npairs):
            vp = vbuf[pl.ds(start, NA_KH * GRID_W), col[p]]
            pe, l = probs[p]
            o_both = _dot(pe, vp) / l
            o = jnp.where(halves[0], o_both[:GRID_W], o_both[GRID_W:])
            o2 = o * o
            ss = [jnp.sum(jnp.where(h, o2, 0.0), axis=-1, keepdims=True) for h in halves]
            inv = jnp.where(halves[0], lax.rsqrt(ss[0] * (1.0 / NA_DH) + EPS),
                            lax.rsqrt(ss[1] * (1.0 / NA_DH) + EPS))
            o_ref[0, pl.ds(qrow, GRID_W), col[p]] = (o * inv * g_ref[...]).astype(BF16)
        return carry

    lax.fori_loop(0, NA_ROWS_PER_STEP, row_body, 0, unroll=True)


def _na_attn(proj, bias, g_out):
    b, s, _ = proj.shape
    rows = s // GRID_W
    nb = rows // NA_ROWS_PER_STEP
    qcol, kcol, vcol = SEC_NQ, SEC_NK, SEC_NV
    blk = (1, NA_BLOCK, HALF_WIDTH)

    def prev(bi, i, c):
        return (bi, jnp.maximum(i - 1, 0), c)

    def nxt(bi, i, c):
        return (bi, jnp.minimum(i + 1, nb - 1), c)

    return pl.pallas_call(
        functools.partial(_na_attn_kernel, rows=rows),
        grid=(b, nb),
        in_specs=[
            pl.BlockSpec(blk, lambda bi, i: (bi, i, qcol)),
            pl.BlockSpec(blk, lambda bi, i: prev(bi, i, kcol)),
            pl.BlockSpec(blk, lambda bi, i: (bi, i, kcol)),
            pl.BlockSpec(blk, lambda bi, i: nxt(bi, i, kcol)),
            pl.BlockSpec(blk, lambda bi, i: prev(bi, i, vcol)),
            pl.BlockSpec(blk, lambda bi, i: (bi, i, vcol)),
            pl.BlockSpec(blk, lambda bi, i: nxt(bi, i, vcol)),
            pl.BlockSpec((NA_HEADS // 2, 2, 2 * GRID_W, NA_BIAS_LANES), lambda bi, i: (0, 0, 0, 0)),
            pl.BlockSpec((1, LANES), lambda bi, i: (0, 0)),
        ],
        out_specs=pl.BlockSpec(blk, lambda bi, i: (bi, i, 0)),
        out_shape=jax.ShapeDtypeStruct((b, s, HALF_WIDTH), BF16),
        scratch_shapes=[pltpu.VMEM((3 * NA_BLOCK, HALF_WIDTH), BF16),
                        pltpu.VMEM((3 * NA_BLOCK, HALF_WIDTH), BF16)],
        compiler_params=_cparams(("parallel", "arbitrary")),
        name="na_attn",
    )(proj, proj, proj, proj, proj, proj, proj, bias, g_out)


NA_BIAS_LANES = 1024


def _na_bias_table(rpb):
    qc = np.arange(GRID_W)
    kc = np.arange(GRID_W)
    win = np.clip(qc - NA_KW // 2, 0, GRID_W - NA_KW)
    valid = (kc[None, :] >= win[:, None]) & (kc[None, :] < win[:, None] + NA_KW)
    padded = jnp.pad(rpb.astype(F32) * math.log2(math.e), ((0, 0), (0, 0), (GRID_W, GRID_W)))
    by_col = jnp.stack([padded[:, :, GRID_W + NA_KW - 1 - c:2 * GRID_W + NA_KW - 1 - c] for c in range(GRID_W)],
                       axis=1)
    by_col = jnp.where(jnp.asarray(valid)[None, :, None, :], by_col, MASK_VALUE)
    flat = by_col.reshape(NA_HEADS, GRID_W, (2 * NA_KH - 1) * GRID_W)
    flat = jnp.pad(flat, ((0, 0), (0, 0), (0, NA_BIAS_LANES + GRID_W - flat.shape[-1])))
    both = jnp.stack([flat[..., :NA_BIAS_LANES], flat[..., GRID_W:]], axis=1)
    both = both.reshape(NA_HEADS // 2, 2, 2, GRID_W, NA_BIAS_LANES).transpose(0, 2, 1, 3, 4)
    return both.reshape(NA_HEADS // 2, 2, 2 * GRID_W, NA_BIAS_LANES)


def _out_proj_kernel(x_ref, a_ref, b_ref, w_ref, g_ref, xo_ref, h_ref):
    x = x_ref[...] + _dot(a_ref[...], w_ref[:HALF_WIDTH, :]) + _dot(b_ref[...], w_ref[HALF_WIDTH:, :])
    xo_ref[...] = x
    h_ref[...] = _rms(x, g_ref[...]).astype(BF16)


def _out_proj_router_kernel(x_ref, a_ref, b_ref, w_ref, g_ref, wr_ref, xo_ref, h_ref, c_ref):
    x = x_ref[...] + _dot(a_ref[...], w_ref[:HALF_WIDTH, :]) + _dot(b_ref[...], w_ref[HALF_WIDTH:, :])
    xo_ref[...] = x
    h = _rms(x, g_ref[...])
    h_hi = h.astype(BF16)
    h_ref[...] = h_hi
    h_lo = (h - h_hi.astype(F32)).astype(BF16)
    wr = wr_ref[...]
    both = _dot(h_hi, wr)
    lane = lax.broadcasted_iota(jnp.int32, both.shape, 1).astype(F32)
    lo_part = pltpu.roll(both, LANES - N_EXPERTS, 1)
    logits = both + lo_part + _dot(h_lo, wr)
    logits = jnp.where(lane < N_EXPERTS, logits, -jnp.inf)
    m1 = jnp.max(logits, axis=-1, keepdims=True)
    i1 = jnp.min(jnp.where(logits == m1, lane, float(LANES)), axis=-1, keepdims=True)
    rest = jnp.where(lane == i1, -jnp.inf, logits)
    m2 = jnp.max(rest, axis=-1, keepdims=True)
    i2 = jnp.min(jnp.where(rest == m2, lane, float(LANES)), axis=-1, keepdims=True)
    e2 = jnp.exp(m2 - m1)
    g1 = 1.0 / (1.0 + e2)
    g2 = e2 * g1
    c_ref[...] = jnp.where(lane == i1, g1, jnp.where(lane == i2, g2, 0.0))


def _out_proj(x, a, b, w, g, wr=None, tm=512):
    t = x.shape[0]
    tm = min(tm, t)
    row = lambda i: (i, 0)
    const = lambda i: (0, 0)
    in_specs = [
        pl.BlockSpec((tm, D_MODEL), row),
        pl.BlockSpec((tm, HALF_WIDTH), row),
        pl.BlockSpec((tm, HALF_WIDTH), row),
        pl.BlockSpec((D_MODEL, D_MODEL), const),
        pl.BlockSpec((1, D_MODEL), const),
    ]
    out_specs = [pl.BlockSpec((tm, D_MODEL), row), pl.BlockSpec((tm, D_MODEL), row)]
    out_shape = [jax.ShapeDtypeStruct((t, D_MODEL), F32), jax.ShapeDtypeStruct((t, D_MODEL), BF16)]
    args = [x, a, b, w, g]
    kern = _out_proj_kernel
    if wr is not None:
        in_specs.append(pl.BlockSpec((D_MODEL, LANES), const))
        out_specs.append(pl.BlockSpec((tm, LANES), row))
        out_shape.append(jax.ShapeDtypeStruct((t, LANES), F32))
        args.append(wr)
        kern = _out_proj_router_kernel
    return pl.pallas_call(
        kern,
        grid=(t // tm,),
        in_specs=in_specs,
        out_specs=out_specs,
        out_shape=out_shape,
        compiler_params=_cparams(("parallel",)),
        name="out_proj",
    )(*args)


def _gate_up(h, wgu):
    r = _dot(h, wgu)
    g, u = r[:, :FF_BLOCK], r[:, FF_BLOCK:]
    return (g * jax.nn.sigmoid(g) * u).astype(BF16)


def _swiglu_kernel(x_ref, h_ref, wgu_ref, wd_ref, o_ref):
    @pl.when(pl.program_id(1) == 0)
    def _():
        o_ref[...] = x_ref[...]

    o_ref[...] += _dot(_gate_up(h_ref[...], wgu_ref[...]), wd_ref[...])


def _swiglu(x, h, wgu, wd, tm=1024):
    t = x.shape[0]
    tf = FF_BLOCK
    tm = min(tm, t)
    return pl.pallas_call(
        _swiglu_kernel,
        grid=(t // tm, D_FF // tf),
        in_specs=[
            pl.BlockSpec((tm, D_MODEL), lambda i, f: (i, 0)),
            pl.BlockSpec((tm, D_MODEL), lambda i, f: (i, 0)),
            pl.BlockSpec((D_MODEL, 2 * tf), lambda i, f: (0, f)),
            pl.BlockSpec((tf, D_MODEL), lambda i, f: (f, 0)),
        ],
        out_specs=pl.BlockSpec((tm, D_MODEL), lambda i, f: (i, 0)),
        out_shape=jax.ShapeDtypeStruct((t, D_MODEL), F32),
        compiler_params=_cparams(("parallel", "arbitrary")),
        name="swiglu",
    )(x, h, wgu, wd)


MOE_TILE = 1024
MOE_CHUNK = 256
MOE_TAIL = 128
TRI_BLOCK = 512


def _route_kernel(c_ref, ltri_ref, utri_ref, posm_ref, posmt_ref):
    tm = c_ref.shape[0]
    carry = jnp.zeros((1, LANES), F32)
    carry_t = jnp.zeros((LANES, 1), F32)
    for b in range(tm // TRI_BLOCK):
        rows = slice(b * TRI_BLOCK, (b + 1) * TRI_BLOCK)
        sel = c_ref[rows, :] > 0.0
        self = jnp.where(sel, 1.0, 0.0)
        pos = _dot(ltri_ref[...], self.astype(BF16)) + carry
        posm_ref[rows, :] = jnp.where(sel, pos, -1.0)
        sel_t = self.T
        pos_t = _dot(sel_t.astype(BF16), utri_ref[...]) + carry_t
        posmt_ref[0, :, rows] = jnp.where(sel_t > 0.0, pos_t, -1.0)
        carry = carry + jnp.sum(self, axis=0, keepdims=True)
        carry_t = carry_t + jnp.sum(sel_t, axis=1, keepdims=True)


def _route(c, tm):
    t = c.shape[0]
    idx = np.arange(TRI_BLOCK)
    ltri = jnp.asarray(idx[None, :] < idx[:, None], dtype=BF16)
    utri = jnp.asarray(idx[:, None] < idx[None, :], dtype=BF16)
    return pl.pallas_call(
        _route_kernel,
        grid=(t // tm,),
        in_specs=[
            pl.BlockSpec((tm, LANES), lambda i: (i, 0)),
            pl.BlockSpec((TRI_BLOCK, TRI_BLOCK), lambda i: (0, 0)),
            pl.BlockSpec((TRI_BLOCK, TRI_BLOCK), lambda i: (0, 0)),
        ],
        out_specs=[pl.BlockSpec((tm, LANES), lambda i: (i, 0)),
                   pl.BlockSpec((1, LANES, tm), lambda i: (i, 0, 0))],
        out_shape=[jax.ShapeDtypeStruct((t, LANES), F32),
                   jax.ShapeDtypeStruct((t // tm, LANES, tm), F32)],
        compiler_params=_cparams(("parallel",)),
        name="moe_route",
    )(c, ltri, utri)


def _moe_sparse_kernel(x_ref, h_ref, c_ref, posm_ref, posmt_ref, wgu_ref, wd_ref, o_ref, hc_ref, y_ref):
    e = pl.program_id(1)
    f = pl.program_id(2)
    last_f = pl.num_programs(2) - 1
    tm = x_ref.shape[0]

    @pl.when((e == 0) & (f == 0))
    def _():
        o_ref[...] = x_ref[...]

    pos_row = posmt_ref[0, pl.ds(e, 1), :]
    load = (jnp.max(pos_row) + 1.0).astype(jnp.int32)
    rem = load % MOE_CHUNK
    nfull = load // MOE_CHUNK + (rem > MOE_TAIL).astype(jnp.int32)
    has_tail = (rem > 0) & (rem <= MOE_TAIL)
    tail_off = nfull * MOE_CHUNK

    def rows_of(off, size):
        return pl.ds(pl.multiple_of(off, MOE_TAIL), size)

    def for_each_chunk(fn):
        def full(k, carry):
            fn(k * MOE_CHUNK, MOE_CHUNK)
            return carry

        lax.fori_loop(0, nfull, full, 0)

        @pl.when(has_tail)
        def _():
            fn(tail_off, MOE_TAIL)

    def gather(off, size):
        slot = (lax.broadcasted_iota(jnp.int32, (size, tm), 0) + off).astype(F32)
        onehot = jnp.where(pos_row == slot, 1.0, 0.0).astype(BF16)
        hc_ref[rows_of(off, size), :] = _dot(onehot, h_ref[...]).astype(BF16)

    def expert(off, size):
        y = _dot(_gate_up(hc_ref[rows_of(off, size), :], wgu_ref[0]), wd_ref[0])

        @pl.when(f == 0)
        def _():
            y_ref[rows_of(off, size), :] = y

        @pl.when(f > 0)
        def _():
            y_ref[rows_of(off, size), :] += y

    @pl.when(f == 0)
    def _():
        for_each_chunk(gather)

    for_each_chunk(expert)

    @pl.when(f == last_f)
    def _():
        lane = lax.broadcasted_iota(jnp.int32, (tm, LANES), 1)
        pos = jnp.sum(jnp.where(lane == e, posm_ref[...], 0.0), axis=-1, keepdims=True)
        gate = jnp.sum(jnp.where(lane == e, c_ref[...], 0.0), axis=-1, keepdims=True)

        def scatter(off, size):
            slot = (lax.broadcasted_iota(jnp.int32, (tm, size), 1) + off).astype(F32)
            onehot = jnp.where(pos == slot, 1.0, 0.0).astype(BF16)
            o_ref[...] += gate * _dot(onehot, y_ref[rows_of(off, size), :].astype(BF16))

        for_each_chunk(scatter)


def _moe_sparse(x, h, c, wgu, wd, tm=MOE_TILE):
    t = x.shape[0]
    tf = FF_BLOCK
    tm = min(tm, t)
    posm, posmt = _route(c, tm)
    return pl.pallas_call(
        _moe_sparse_kernel,
        grid=(t // tm, N_EXPERTS, D_FF // tf),
        in_specs=[
            pl.BlockSpec((tm, D_MODEL), lambda i, e, f: (i, 0)),
            pl.BlockSpec((tm, D_MODEL), lambda i, e, f: (i, 0)),
            pl.BlockSpec((tm, LANES), lambda i, e, f: (i, 0)),
            pl.BlockSpec((tm, LANES), lambda i, e, f: (i, 0)),
            pl.BlockSpec((1, LANES, tm), lambda i, e, f: (i, 0, 0)),
            pl.BlockSpec((1, D_MODEL, 2 * tf), lambda i, e, f: (e, 0, f)),
            pl.BlockSpec((1, tf, D_MODEL), lambda i, e, f: (e, f, 0)),
        ],
        out_specs=pl.BlockSpec((tm, D_MODEL), lambda i, e, f: (i, 0)),
        out_shape=jax.ShapeDtypeStruct((t, D_MODEL), F32),
        scratch_shapes=[pltpu.VMEM((tm, D_MODEL), BF16), pltpu.VMEM((tm, D_MODEL), F32)],
        compiler_params=_cparams(("parallel", "arbitrary", "arbitrary")),
        name="moe_sparse",
    )(x, h, c, posm, posmt, wgu, wd)


def _rope_tables(seq):
    inv = 1.0 / (ROPE_THETA ** (jnp.arange(0, DIFF_DH, 2, dtype=F32) / DIFF_DH))
    ang = jnp.arange(seq, dtype=F32)[:, None] * inv[None, :]
    ang = jnp.concatenate([ang, ang], axis=-1)
    cos, sin = jnp.cos(ang), jnp.sin(ang)
    sign = jnp.where(jnp.arange(DIFF_DH) < DIFF_DH // 2, -1.0, 1.0).astype(F32)
    return jnp.tile(cos, (1, 2)), jnp.tile(sin * sign, (1, 2))


def _block_diag_ones():
    idx = np.arange(MXU_TILE) // DIFF_DH
    return jnp.asarray(idx[:, None] == idx[None, :], dtype=BF16)


def _gate_up_blocks(w_gate, w_up):
    pieces = []
    for f in range(D_FF // FF_BLOCK):
        cols = slice(f * FF_BLOCK, (f + 1) * FF_BLOCK)
        pieces += [w_gate[..., cols], w_up[..., cols]]
    return jnp.concatenate(pieces, axis=-1).astype(BF16)


def _prepare_layer(l, p):
    lam_init = 0.8 - 0.6 * math.exp(-0.3 * l)
    lq = p["lambda_q"][l].astype(F32)
    lk = p["lambda_k"][l].astype(F32)
    lam = jnp.exp(jnp.sum(lq[0] * lk[0])) - jnp.exp(jnp.sum(lq[1] * lk[1])) + lam_init
    scale = DIFF_DH ** -0.5 * math.log2(math.e)
    tile8 = lambda g: jnp.tile(g.astype(F32), HALF_WIDTH // g.shape[0])
    zeros = jnp.zeros((HALF_WIDTH,), F32)
    dq_gain = p["diff_q_norm"][l].astype(F32) * scale
    dk_gain = p["diff_k_norm"][l].astype(F32)
    gains = jnp.stack([
        tile8(dq_gain), tile8(dk_gain), zeros,
        tile8(p["na_q_norm"][l]) * scale, tile8(p["na_k_norm"][l]), zeros, zeros, zeros])
    score_bound = SCORE_BOUND_MARGIN * DIFF_DH * jnp.max(jnp.abs(dq_gain)) * jnp.max(jnp.abs(dk_gain))
    lay = dict(
        scal=jnp.stack([lam, score_bound]).astype(F32),
        out_scale=1.0 - lam_init,
        attn_norm=p["attn_norm"][l].reshape(1, D_MODEL).astype(F32),
        w_in=p["w_in"][l].astype(BF16),
        gains=gains,
        g_sub=p["diff_sub_norm"][l].reshape(1, LANES).astype(F32),
        na_bias=_na_bias_table(p["na_rpb"][l]),
        g_na_out=jnp.tile(p["na_out_norm"][l].astype(F32), 2).reshape(1, LANES),
        w_out=p["w_out"][l].astype(BF16),
        ffn_norm=p["ffn_norm"][l].reshape(1, D_MODEL).astype(F32),
    )
    i = l // 2
    if l % 2 == 0:
        lay.update(wgu=_gate_up_blocks(p["dense_w_gate"][i], p["dense_w_up"][i]),
                   wd=p["dense_w_down"][i].astype(BF16))
    else:
        wr = p["moe_router"][i].astype(F32)
        wr_hi = wr.astype(BF16)
        wr_lo = (wr - wr_hi.astype(F32)).astype(BF16)
        wr2 = jnp.zeros((D_MODEL, LANES), BF16)
        wr2 = wr2.at[:, :N_EXPERTS].set(wr_hi).at[:, N_EXPERTS:2 * N_EXPERTS].set(wr_lo)
        lay.update(wr=wr2, wgu=_gate_up_blocks(p["moe_w_gate"][i], p["moe_w_up"][i]),
                   wd=p["moe_w_down"][i].astype(BF16))
    return lay


def _run_trunk(x, layers, bd):
    b, s, _ = x.shape
    t = b * s
    cos, sin = _rope_tables(s)
    x = x.reshape(t, D_MODEL)
    for l, lay in enumerate(layers):
        proj, vt = _in_proj(x, lay["attn_norm"], lay["w_in"], lay["gains"], cos, sin, bd, s)
        proj3 = proj.reshape(b, s, IN_WIDTH)
        a_out = _diff_attn(proj3, vt, lay["scal"], lay["g_sub"], lay["out_scale"])
        b_out = _na_attn(proj3, lay["na_bias"], lay["g_na_out"])
        a_out = a_out.reshape(t, HALF_WIDTH)
        b_out = b_out.reshape(t, HALF_WIDTH)
        if l % 2 == 0:
            x, h = _out_proj(x, a_out, b_out, lay["w_out"], lay["ffn_norm"])
            x = _swiglu(x, h, lay["wgu"], lay["wd"])
        else:
            x, h, c = _out_proj(x, a_out, b_out, lay["w_out"], lay["ffn_norm"], lay["wr"])
            x = _moe_sparse(x, h, c, lay["wgu"], lay["wd"])
    return x.reshape(b, s, D_MODEL)


def kernel(x_prompt, x_sample, attn_norm, w_in, w_out, diff_q_norm, diff_k_norm, lambda_q, lambda_k, diff_sub_norm, na_q_norm, na_k_norm, na_rpb, na_out_norm, ffn_norm, dense_w_gate, dense_w_up, dense_w_down, moe_router, moe_w_gate, moe_w_up, moe_w_down):
    p = dict(attn_norm=attn_norm, w_in=w_in, w_out=w_out, diff_q_norm=diff_q_norm, diff_k_norm=diff_k_norm,
             lambda_q=lambda_q, lambda_k=lambda_k, diff_sub_norm=diff_sub_norm, na_q_norm=na_q_norm,
             na_k_norm=na_k_norm, na_rpb=na_rpb, na_out_norm=na_out_norm, ffn_norm=ffn_norm,
             dense_w_gate=dense_w_gate, dense_w_up=dense_w_up, dense_w_down=dense_w_down,
             moe_router=moe_router, moe_w_gate=moe_w_gate, moe_w_up=moe_w_up, moe_w_down=moe_w_down)
    layers = [_prepare_layer(l, p) for l in range(DEPTH)]
    bd = _block_diag_ones()
    return (_run_trunk(x_prompt, layers, bd), _run_trunk(x_sample, layers, bd))
```

```python
import functools
import math

import jax
import jax.numpy as jnp
import numpy as np
from jax import lax
from jax.experimental import pallas as pl
from jax.experimental.pallas import tpu as pltpu

D_MODEL = 1024
DEPTH = 4
GRID_W = 64
DIFF_HEADS = 4
DIFF_DH = 64
NA_HEADS = 8
NA_DH = 64
NA_KH = 8
NA_KW = 16
HALF_WIDTH = 512
SEC_DQ, SEC_DK, SEC_DV, SEC_NQ, SEC_NK, SEC_NV = range(6)
IN_WIDTH = 6 * HALF_WIDTH
D_FF = 2816
FF_BLOCK = 1408
N_EXPERTS = 8
ROPE_THETA = 10000.0
EPS = 1e-6
MASK_VALUE = -1e30
SCORE_BOUND_MARGIN = 1.02
MAX_SAFE_SCORE_BOUND = 60.0

LANES = 128
MXU_TILE = 256
VMEM_LIMIT = 56 * 1024 * 1024

BF16 = jnp.bfloat16
F32 = jnp.float32


def _cparams(sem):
    return pltpu.CompilerParams(dimension_semantics=sem, vmem_limit_bytes=VMEM_LIMIT)


def _dot(a, b):
    return jnp.dot(a, b, preferred_element_type=F32)


def _dot_nt(a, b):
    return lax.dot_general(a, b, (((1,), (1,)), ((), ())), preferred_element_type=F32)


def _rms(x, g):
    return x * lax.rsqrt(jnp.mean(x * x, axis=-1, keepdims=True) + EPS) * g


def _group_norm64(y, gain, bd_ref):
    parts = []
    for c in range(HALF_WIDTH // MXU_TILE):
        yc = y[:, c * MXU_TILE:(c + 1) * MXU_TILE]
        ss = _dot((yc * yc).astype(BF16), bd_ref[...])
        parts.append(yc * lax.rsqrt(ss * (1.0 / 64) + EPS))
    return jnp.concatenate(parts, axis=1) * gain


def _rope64(y, cos, sin_signed):
    n = y.shape[1]
    lane = lax.broadcasted_iota(jnp.int32, y.shape, 1)
    first_half = (lane % 64) < 32
    rot = jnp.where(first_half, pltpu.roll(y, n - 32, 1), pltpu.roll(y, 32, 1))
    return y * cos + rot * sin_signed


def _in_proj_kernel(x_ref, g_ref, w_ref, gains_ref, cos_ref, sin_ref, bd_ref, o_ref, vt_ref):
    x = x_ref[...]
    h = _rms(x, g_ref[...]).astype(BF16)
    cos = jnp.concatenate([cos_ref[...]] * (HALF_WIDTH // LANES), axis=1)
    sin = jnp.concatenate([sin_ref[...]] * (HALF_WIDTH // LANES), axis=1)
    for sec in range(6):
        y = _dot(h, w_ref[:, sec * HALF_WIDTH:(sec + 1) * HALF_WIDTH])
        if sec in (SEC_DQ, SEC_DK, SEC_NQ, SEC_NK):
            y = _group_norm64(y, gains_ref[sec:sec + 1, :], bd_ref)
        if sec in (SEC_DQ, SEC_DK):
            y = _rope64(y, cos, sin)
        o_ref[:, sec * HALF_WIDTH:(sec + 1) * HALF_WIDTH] = y.astype(BF16)
        if sec == SEC_DV:
            vt_ref[0] = y.T.astype(BF16)


def _in_proj(x, g, w, gains, cos, sin, bd, seq, tm=512):
    t = x.shape[0]
    tm = min(tm, seq)
    nseq = seq // tm
    return pl.pallas_call(
        _in_proj_kernel,
        grid=(t // tm,),
        in_specs=[
            pl.BlockSpec((tm, D_MODEL), lambda i: (i, 0)),
            pl.BlockSpec((1, D_MODEL), lambda i: (0, 0)),
            pl.BlockSpec((D_MODEL, IN_WIDTH), lambda i: (0, 0)),
            pl.BlockSpec((8, HALF_WIDTH), lambda i: (0, 0)),
            pl.BlockSpec((tm, LANES), lambda i: (i % nseq, 0)),
            pl.BlockSpec((tm, LANES), lambda i: (i % nseq, 0)),
            pl.BlockSpec((MXU_TILE, MXU_TILE), lambda i: (0, 0)),
        ],
        out_specs=[pl.BlockSpec((tm, IN_WIDTH), lambda i: (i, 0)),
                   pl.BlockSpec((1, HALF_WIDTH, tm), lambda i: (i // nseq, 0, i % nseq))],
        out_shape=[jax.ShapeDtypeStruct((t, IN_WIDTH), BF16),
                   jax.ShapeDtypeStruct((t // seq, HALF_WIDTH, seq), BF16)],
        compiler_params=_cparams(("parallel",)),
        name="in_proj",
    )(x, g, w, gains, cos, sin, bd)


def _split_maps(q):
    lane = lax.broadcasted_iota(jnp.int32, q.shape, 1)
    zero = jnp.zeros_like(q)
    return jnp.where(lane < DIFF_DH, q, zero), jnp.where(lane >= DIFF_DH, q, zero)


def _diff_attn_online_kernel(scal_ref, q_ref, k_ref, v_ref, g_ref, o_ref, acc0_ref, acc1_ref, l0_ref, l1_ref, *,
                             tk, out_scale):
    qm = _split_maps(q_ref[0])
    tq = q_ref.shape[1]
    accs = (acc0_ref, acc1_ref)
    ls = (l0_ref, l1_ref)
    for r in accs + ls:
        r[...] = jnp.zeros_like(r)

    def lane_fold(p):
        out = p[:, :LANES]
        for c in range(1, tk // LANES):
            out = out + p[:, c * LANES:(c + 1) * LANES]
        return out

    def body(j, ms):
        start = pl.multiple_of(j * tk, tk)
        k = k_ref[0, pl.ds(start, tk), :]
        v = v_ref[0, pl.ds(start, tk), :]
        new_ms = []
        for a in range(2):
            s = _dot_nt(qm[a], k)
            m_new = jnp.maximum(ms[a], jnp.max(s, axis=-1, keepdims=True))
            alpha = jnp.exp2(ms[a] - m_new)
            p = jnp.exp2(s - m_new)
            ls[a][...] = alpha * ls[a][...] + lane_fold(p)
            accs[a][...] = alpha * accs[a][...] + _dot(p.astype(BF16), v)
            new_ms.append(m_new)
        return tuple(new_ms)

    neg = jnp.full((tq, 1), -jnp.inf, F32)
    lax.fori_loop(0, k_ref.shape[1] // tk, body, (neg, neg))
    lam = scal_ref[0]
    l0 = jnp.sum(l0_ref[...], axis=-1, keepdims=True)
    l1 = jnp.sum(l1_ref[...], axis=-1, keepdims=True)
    o = acc0_ref[...] / l0 - lam * (acc1_ref[...] / l1)
    o_ref[0] = (_rms(o, g_ref[...]) * out_scale).astype(BF16)


def _diff_attn_t_kernel(scal_ref, q_ref, k_ref, vt_ref, g_ref, o_ref, acc0_ref, acc1_ref, l0_ref, l1_ref, *,
                        tk, out_scale):
    qm = _split_maps(q_ref[0])
    accs = (acc0_ref, acc1_ref)
    ls = (l0_ref, l1_ref)
    for r in accs + ls:
        r[...] = jnp.zeros_like(r)
    nk = k_ref.shape[1] // tk
    tq = q_ref.shape[1]
    bound = scal_ref[1]

    def body(j, carry):
        start = pl.multiple_of(j * tk, tk)
        k = k_ref[0, pl.ds(start, tk), :]
        vt = vt_ref[0, :, pl.ds(start, tk)]
        for a in range(2):
            p = jnp.exp2(_dot_nt(k, qm[a]) - bound)
            ls[a][...] += jnp.sum(p.reshape(tk // 8, 8, tq), axis=0)
            accs[a][...] += _dot(vt, p.astype(BF16))
        return carry

    lax.fori_loop(0, nk, body, 0, unroll=2)
    lam = scal_ref[0]
    l0 = jnp.sum(l0_ref[...], axis=0, keepdims=True)
    l1 = jnp.sum(l1_ref[...], axis=0, keepdims=True)
    ot = acc0_ref[...] / l0 - lam * (acc1_ref[...] / l1)
    ot = ot * lax.rsqrt(jnp.mean(ot * ot, axis=0, keepdims=True) + EPS)
    o_ref[0] = (ot.T * g_ref[...] * out_scale).astype(BF16)


DIFF_TQ = 1024
DIFF_TK_BOUNDED = 2048
DIFF_TK_ONLINE = 1024


def _diff_attn_call(proj, vt, scal, g_sub, out_scale, bounded):
    b, s, _ = proj.shape
    tq = min(DIFF_TQ, s)
    q_spec = pl.BlockSpec((1, tq, LANES), lambda bi, h, i: (bi, i, h))
    k_spec = pl.BlockSpec((1, s, LANES), lambda bi, h, i: (bi, 0, DIFF_HEADS + h))
    if bounded:
        kern = functools.partial(_diff_attn_t_kernel, tk=min(DIFF_TK_BOUNDED, s), out_scale=out_scale)
        v, v_spec = vt, pl.BlockSpec((1, LANES, s), lambda bi, h, i: (bi, h, 0))
        scratch = [pltpu.VMEM((LANES, tq), F32)] * 2 + [pltpu.VMEM((8, tq), F32)] * 2
    else:
        kern = functools.partial(_diff_attn_online_kernel, tk=min(DIFF_TK_ONLINE, s), out_scale=out_scale)
        v, v_spec = proj, pl.BlockSpec((1, s, LANES), lambda bi, h, i: (bi, 0, 2 * DIFF_HEADS + h))
        scratch = [pltpu.VMEM((tq, LANES), F32)] * 4
    return pl.pallas_call(
        kern,
        grid=(b, DIFF_HEADS, s // tq),
        in_specs=[pl.BlockSpec(memory_space=pltpu.SMEM), q_spec, k_spec, v_spec,
                  pl.BlockSpec((1, LANES), lambda bi, h, i: (0, 0))],
        out_specs=pl.BlockSpec((1, tq, LANES), lambda bi, h, i: (bi, i, h)),
        out_shape=jax.ShapeDtypeStruct((b, s, HALF_WIDTH), BF16),
        scratch_shapes=scratch,
        compiler_params=_cparams(("parallel", "parallel", "arbitrary")),
        name="diff_attn_t" if bounded else "diff_attn_online",
    )(scal, proj, proj, v, g_sub)


def _diff_attn(proj, vt, scal, g_sub, out_scale):
    return lax.cond(
        scal[1] <= MAX_SAFE_SCORE_BOUND,
        lambda: _diff_attn_call(proj, vt, scal, g_sub, out_scale, True),
        lambda: _diff_attn_call(proj, vt, scal, g_sub, out_scale, False))


NA_ROWS_PER_STEP = 8
NA_BLOCK = NA_ROWS_PER_STEP * GRID_W


def _na_attn_kernel(q_ref, kp_ref, kc_ref, kn_ref, vp_ref, vc_ref, vn_ref, bias_ref, g_ref, o_ref,
                    kbuf, vbuf, *, rows):
    i = pl.program_id(1)
    for n, (kr, vr) in enumerate(((kp_ref, vp_ref), (kc_ref, vc_ref), (kn_ref, vn_ref))):
        kbuf[n * NA_BLOCK:(n + 1) * NA_BLOCK, :] = kr[0]
        vbuf[n * NA_BLOCK:(n + 1) * NA_BLOCK, :] = vr[0]
    lane = lax.broadcasted_iota(jnp.int32, (GRID_W, LANES), 1)
    halves = (lane < NA_DH, lane >= NA_DH)

    def row_body(qr, carry):
        r = i * NA_ROWS_PER_STEP + qr
        r0 = jnp.clip(r - NA_KH // 2, 0, rows - NA_KH)
        start = pl.multiple_of((r0 - (i - 1) * NA_ROWS_PER_STEP) * GRID_W, GRID_W)
        rp = r - r0
        qrow = pl.multiple_of(qr * GRID_W, GRID_W)
        npairs = NA_HEADS // 2
        col = [slice(p * LANES, (p + 1) * LANES) for p in range(npairs)]
        scores = []
        for p in range(npairs):
            qp = q_ref[0, pl.ds(qrow, GRID_W), col[p]]
            kp = kbuf[pl.ds(start, NA_KH * GRID_W), col[p]]
            q2 = jnp.concatenate([jnp.where(h, qp, jnp.zeros_like(qp)) for h in halves], axis=0)
            scores.append(_dot_nt(q2, kp) + bias_ref[p, rp])
        probs = []
        for s in scores:
            m = jnp.max(s, axis=-1, keepdims=True)
            pe = jnp.exp2(s - m)
            probs.append((pe.astype(BF16), jnp.sum(pe, axis=-1, keepdims=True)))
        for p in range(npairs):
            vp = vbuf[pl.ds(start, NA_KH * GRID_W), col[p]]
            pe, l = probs[p]
            o_both = _dot(pe, vp) / l
            o = jnp.where(halves[0], o_both[:GRID_W], o_both[GRID_W:])
            o2 = o * o
            ss = [jnp.sum(jnp.where(h, o2, 0.0), axis=-1, keepdims=True) for h in halves]
            inv = jnp.where(halves[0], lax.rsqrt(ss[0] * (1.0 / NA_DH) + EPS),
                            lax.rsqrt(ss[1] * (1.0 / NA_DH) + EPS))
            o_ref[0, pl.ds(qrow, GRID_W), col[p]] = (o * inv * g_ref[...]).astype(BF16)
        return carry

    lax.fori_loop(0, NA_ROWS_PER_STEP, row_body, 0, unroll=True)


def _na_attn(proj, bias, g_out):
    b, s, _ = proj.shape
    rows = s // GRID_W
    nb = rows // NA_ROWS_PER_STEP
    qcol, kcol, vcol = SEC_NQ, SEC_NK, SEC_NV
    blk = (1, NA_BLOCK, HALF_WIDTH)

    def prev(bi, i, c):
        return (bi, jnp.maximum(i - 1, 0), c)

    def nxt(bi, i, c):
        return (bi, jnp.minimum(i + 1, nb - 1), c)

    return pl.pallas_call(
        functools.partial(_na_attn_kernel, rows=rows),
        grid=(b, nb),
        in_specs=[
            pl.BlockSpec(blk, lambda bi, i: (bi, i, qcol)),
            pl.BlockSpec(blk, lambda bi, i: prev(bi, i, kcol)),
            pl.BlockSpec(blk, lambda bi, i: (bi, i, kcol)),
            pl.BlockSpec(blk, lambda bi, i: nxt(bi, i, kcol)),
            pl.BlockSpec(blk, lambda bi, i: prev(bi, i, vcol)),
            pl.BlockSpec(blk, lambda bi, i: (bi, i, vcol)),
            pl.BlockSpec(blk, lambda bi, i: nxt(bi, i, vcol)),
            pl.BlockSpec((NA_HEADS // 2, NA_KH, 2 * GRID_W, NA_KH * GRID_W), lambda bi, i: (0, 0, 0, 0)),
            pl.BlockSpec((1, LANES), lambda bi, i: (0, 0)),
        ],
        out_specs=pl.BlockSpec(blk, lambda bi, i: (bi, i, 0)),
        out_shape=jax.ShapeDtypeStruct((b, s, HALF_WIDTH), BF16),
        scratch_shapes=[pltpu.VMEM((3 * NA_BLOCK, HALF_WIDTH), BF16),
                        pltpu.VMEM((3 * NA_BLOCK, HALF_WIDTH), BF16)],
        compiler_params=_cparams(("parallel", "arbitrary")),
        name="na_attn",
    )(proj, proj, proj, proj, proj, proj, proj, bias, g_out)


def _na_bias_table(rpb):
    qc = np.arange(GRID_W)
    kc = np.arange(GRID_W)
    win = np.clip(qc - NA_KW // 2, 0, GRID_W - NA_KW)
    valid = (kc[None, :] >= win[:, None]) & (kc[None, :] < win[:, None] + NA_KW)
    padded = jnp.pad(rpb.astype(F32) * math.log2(math.e), ((0, 0), (0, 0), (GRID_W, GRID_W)))
    by_col = jnp.stack([padded[:, :, GRID_W + NA_KW - 1 - c:2 * GRID_W + NA_KW - 1 - c] for c in range(GRID_W)],
                       axis=2)
    by_col = jnp.where(jnp.asarray(valid)[None, None], by_col, MASK_VALUE)
    t = jnp.stack([by_col[:, NA_KH - 1 - r:2 * NA_KH - 1 - r] for r in range(NA_KH)], axis=1)
    t = t.reshape(NA_HEADS // 2, 2, NA_KH, NA_KH, GRID_W, GRID_W)
    t = t.transpose(0, 2, 1, 4, 3, 5)
    return t.reshape(NA_HEADS // 2, NA_KH, 2 * GRID_W, NA_KH * GRID_W)


def _out_proj_kernel(x_ref, a_ref, b_ref, w_ref, g_ref, xo_ref, h_ref):
    x = x_ref[...] + _dot(a_ref[...], w_ref[:HALF_WIDTH, :]) + _dot(b_ref[...], w_ref[HALF_WIDTH:, :])
    xo_ref[...] = x
    h_ref[...] = _rms(x, g_ref[...]).astype(BF16)


def _out_proj_router_kernel(x_ref, a_ref, b_ref, w_ref, g_ref, wr_ref, xo_ref, h_ref, c_ref):
    x = x_ref[...] + _dot(a_ref[...], w_ref[:HALF_WIDTH, :]) + _dot(b_ref[...], w_ref[HALF_WIDTH:, :])
    xo_ref[...] = x
    h = _rms(x, g_ref[...])
    h_hi = h.astype(BF16)
    h_ref[...] = h_hi
    h_lo = (h - h_hi.astype(F32)).astype(BF16)
    wr = wr_ref[...]
    both = _dot(h_hi, wr)
    lane = lax.broadcasted_iota(jnp.int32, both.shape, 1).astype(F32)
    lo_part = pltpu.roll(both, LANES - N_EXPERTS, 1)
    logits = both + lo_part + _dot(h_lo, wr)
    logits = jnp.where(lane < N_EXPERTS, logits, -jnp.inf)
    m1 = jnp.max(logits, axis=-1, keepdims=True)
    i1 = jnp.min(jnp.where(logits == m1, lane, float(LANES)), axis=-1, keepdims=True)
    rest = jnp.where(lane == i1, -jnp.inf, logits)
    m2 = jnp.max(rest, axis=-1, keepdims=True)
    i2 = jnp.min(jnp.where(rest == m2, lane, float(LANES)), axis=-1, keepdims=True)
    e2 = jnp.exp(m2 - m1)
    g1 = 1.0 / (1.0 + e2)
    g2 = e2 * g1
    c_ref[...] = jnp.where(lane == i1, g1, jnp.where(lane == i2, g2, 0.0))


def _out_proj(x, a, b, w, g, wr=None, tm=512):
    t = x.shape[0]
    tm = min(tm, t)
    row = lambda i: (i, 0)
    const = lambda i: (0, 0)
    in_specs = [
        pl.BlockSpec((tm, D_MODEL), row),
        pl.BlockSpec((tm, HALF_WIDTH), row),
        pl.BlockSpec((tm, HALF_WIDTH), row),
        pl.BlockSpec((D_MODEL, D_MODEL), const),
        pl.BlockSpec((1, D_MODEL), const),
    ]
    out_specs = [pl.BlockSpec((tm, D_MODEL), row), pl.BlockSpec((tm, D_MODEL), row)]
    out_shape = [jax.ShapeDtypeStruct((t, D_MODEL), F32), jax.ShapeDtypeStruct((t, D_MODEL), BF16)]
    args = [x, a, b, w, g]
    kern = _out_proj_kernel
    if wr is not None:
        in_specs.append(pl.BlockSpec((D_MODEL, LANES), const))
        out_specs.append(pl.BlockSpec((tm, LANES), row))
        out_shape.append(jax.ShapeDtypeStruct((t, LANES), F32))
        args.append(wr)
        kern = _out_proj_router_kernel
    return pl.pallas_call(
        kern,
        grid=(t // tm,),
        in_specs=in_specs,
        out_specs=out_specs,
        out_shape=out_shape,
        compiler_params=_cparams(("parallel",)),
        name="out_proj",
    )(*args)


def _gate_up(h, wgu):
    r = _dot(h, wgu)
    g, u = r[:, :FF_BLOCK], r[:, FF_BLOCK:]
    return (g * jax.nn.sigmoid(g) * u).astype(BF16)


def _swiglu_kernel(x_ref, h_ref, wgu_ref, wd_ref, o_ref):
    @pl.when(pl.program_id(1) == 0)
    def _():
        o_ref[...] = x_ref[...]

    o_ref[...] += _dot(_gate_up(h_ref[...], wgu_ref[...]), wd_ref[...])


def _swiglu(x, h, wgu, wd, tm=1024):
    t = x.shape[0]
    tf = FF_BLOCK
    tm = min(tm, t)
    return pl.pallas_call(
        _swiglu_kernel,
        grid=(t // tm, D_FF // tf),
        in_specs=[
            pl.BlockSpec((tm, D_MODEL), lambda i, f: (i, 0)),
            pl.BlockSpec((tm, D_MODEL), lambda i, f: (i, 0)),
            pl.BlockSpec((D_MODEL, 2 * tf), lambda i, f: (0, f)),
            pl.BlockSpec((tf, D_MODEL), lambda i, f: (f, 0)),
        ],
        out_specs=pl.BlockSpec((tm, D_MODEL), lambda i, f: (i, 0)),
        out_shape=jax.ShapeDtypeStruct((t, D_MODEL), F32),
        compiler_params=_cparams(("parallel", "arbitrary")),
        name="swiglu",
    )(x, h, wgu, wd)


MOE_TILE = 1024
MOE_CHUNK = 256
MOE_TAIL = 128
TRI_BLOCK = 512


def _route_kernel(c_ref, utri_ref, pos_ref, gate_ref):
    tm = c_ref.shape[0]
    carry = jnp.zeros((LANES, 1), F32)
    for b in range(tm // TRI_BLOCK):
        rows = slice(b * TRI_BLOCK, (b + 1) * TRI_BLOCK)
        gate_t = c_ref[rows, :].T
        sel_t = jnp.where(gate_t > 0.0, 1.0, 0.0)
        pos_t = _dot(sel_t.astype(BF16), utri_ref[...]) + carry
        pos_ref[0, :, rows] = jnp.where(gate_t > 0.0, pos_t, -1.0)
        gate_ref[0, :, rows] = gate_t
        carry = carry + jnp.sum(sel_t, axis=1, keepdims=True)


def _route(c, tm):
    t = c.shape[0]
    idx = np.arange(TRI_BLOCK)
    utri = jnp.asarray(idx[:, None] < idx[None, :], dtype=BF16)
    expert_major = jax.ShapeDtypeStruct((t // tm, LANES, tm), F32)
    return pl.pallas_call(
        _route_kernel,
        grid=(t // tm,),
        in_specs=[
            pl.BlockSpec((tm, LANES), lambda i: (i, 0)),
            pl.BlockSpec((TRI_BLOCK, TRI_BLOCK), lambda i: (0, 0)),
        ],
        out_specs=[pl.BlockSpec((1, LANES, tm), lambda i: (i, 0, 0))] * 2,
        out_shape=[expert_major, expert_major],
        compiler_params=_cparams(("parallel",)),
        name="moe_route",
    )(c, utri)


def _moe_sparse_kernel(x_ref, h_ref, posmt_ref, gatet_ref, wgu_ref, wd_ref, o_ref, hc_ref, y_ref):
    e = pl.program_id(1)
    f = pl.program_id(2)
    last_f = pl.num_programs(2) - 1
    tm = x_ref.shape[0]

    @pl.when((e == 0) & (f == 0))
    def _():
        o_ref[...] = x_ref[...]

    pos_row = posmt_ref[0, pl.ds(e, 1), :]
    load = (jnp.max(pos_row) + 1.0).astype(jnp.int32)
    rem = load % MOE_CHUNK
    nfull = load // MOE_CHUNK + (rem > MOE_TAIL).astype(jnp.int32)
    has_tail = (rem > 0) & (rem <= MOE_TAIL)
    tail_off = nfull * MOE_CHUNK

    def rows_of(off, size):
        return pl.ds(pl.multiple_of(off, MOE_TAIL), size)

    def for_each_chunk(fn):
        def full(k, carry):
            fn(k * MOE_CHUNK, MOE_CHUNK)
            return carry

        lax.fori_loop(0, nfull, full, 0)

        @pl.when(has_tail)
        def _():
            fn(tail_off, MOE_TAIL)

    def gather(off, size):
        slot = (lax.broadcasted_iota(jnp.int32, (size, tm), 0) + off).astype(F32)
        onehot = jnp.where(pos_row == slot, 1.0, 0.0).astype(BF16)
        hc_ref[rows_of(off, size), :] = _dot(onehot, h_ref[...]).astype(BF16)

    def expert(off, size):
        y = _dot(_gate_up(hc_ref[rows_of(off, size), :], wgu_ref[0]), wd_ref[0])

        @pl.when(f == 0)
        def _():
            y_ref[rows_of(off, size), :] = y

        @pl.when(f > 0)
        def _():
            y_ref[rows_of(off, size), :] += y

    @pl.when(f == 0)
    def _():
        for_each_chunk(gather)

    for_each_chunk(expert)

    @pl.when(f == last_f)
    def _():
        gate_row = gatet_ref[0, pl.ds(e, 1), :]

        def scatter(off, size):
            slot = (lax.broadcasted_iota(jnp.int32, (size, tm), 0) + off).astype(F32)
            hit = pos_row == slot
            gate = jnp.sum(jnp.where(hit, gate_row, 0.0), axis=-1, keepdims=True)
            ys = (y_ref[rows_of(off, size), :] * gate).astype(BF16)
            onehot = jnp.where(hit, 1.0, 0.0).astype(BF16)
            o_ref[...] += lax.dot_general(onehot, ys, (((0,), (0,)), ((), ())), preferred_element_type=F32)

        for_each_chunk(scatter)


def _moe_sparse(x, h, c, wgu, wd, tm=MOE_TILE):
    t = x.shape[0]
    tf = FF_BLOCK
    tm = min(tm, t)
    posmt, gatet = _route(c, tm)
    return pl.pallas_call(
        _moe_sparse_kernel,
        grid=(t // tm, N_EXPERTS, D_FF // tf),
        in_specs=[
            pl.BlockSpec((tm, D_MODEL), lambda i, e, f: (i, 0)),
            pl.BlockSpec((tm, D_MODEL), lambda i, e, f: (i, 0)),
            pl.BlockSpec((1, LANES, tm), lambda i, e, f: (i, 0, 0)),
            pl.BlockSpec((1, LANES, tm), lambda i, e, f: (i, 0, 0)),
            pl.BlockSpec((1, D_MODEL, 2 * tf), lambda i, e, f: (e, 0, f)),
            pl.BlockSpec((1, tf, D_MODEL), lambda i, e, f: (e, f, 0)),
        ],
        out_specs=pl.BlockSpec((tm, D_MODEL), lambda i, e, f: (i, 0)),
        out_shape=jax.ShapeDtypeStruct((t, D_MODEL), F32),
        scratch_shapes=[pltpu.VMEM((tm, D_MODEL), BF16), pltpu.VMEM((tm, D_MODEL), F32)],
        compiler_params=_cparams(("parallel", "arbitrary", "arbitrary")),
        name="moe_sparse",
    )(x, h, posmt, gatet, wgu, wd)


def _rope_tables(seq):
    inv = 1.0 / (ROPE_THETA ** (jnp.arange(0, DIFF_DH, 2, dtype=F32) / DIFF_DH))
    ang = jnp.arange(seq, dtype=F32)[:, None] * inv[None, :]
    ang = jnp.concatenate([ang, ang], axis=-1)
    cos, sin = jnp.cos(ang), jnp.sin(ang)
    sign = jnp.where(jnp.arange(DIFF_DH) < DIFF_DH // 2, -1.0, 1.0).astype(F32)
    return jnp.tile(cos, (1, 2)), jnp.tile(sin * sign, (1, 2))


def _block_diag_ones():
    idx = np.arange(MXU_TILE) // DIFF_DH
    return jnp.asarray(idx[:, None] == idx[None, :], dtype=BF16)


def _gate_up_blocks(w_gate, w_up):
    pieces = []
    for f in range(D_FF // FF_BLOCK):
        cols = slice(f * FF_BLOCK, (f + 1) * FF_BLOCK)
        pieces += [w_gate[..., cols], w_up[..., cols]]
    return jnp.concatenate(pieces, axis=-1).astype(BF16)


def _prepare_layer(l, p):
    lam_init = 0.8 - 0.6 * math.exp(-0.3 * l)
    lq = p["lambda_q"][l].astype(F32)
    lk = p["lambda_k"][l].astype(F32)
    lam = jnp.exp(jnp.sum(lq[0] * lk[0])) - jnp.exp(jnp.sum(lq[1] * lk[1])) + lam_init
    scale = DIFF_DH ** -0.5 * math.log2(math.e)
    tile8 = lambda g: jnp.tile(g.astype(F32), HALF_WIDTH // g.shape[0])
    zeros = jnp.zeros((HALF_WIDTH,), F32)
    dq_gain = p["diff_q_norm"][l].astype(F32) * scale
    dk_gain = p["diff_k_norm"][l].astype(F32)
    gains = jnp.stack([
        tile8(dq_gain), tile8(dk_gain), zeros,
        tile8(p["na_q_norm"][l]) * scale, tile8(p["na_k_norm"][l]), zeros, zeros, zeros])
    score_bound = SCORE_BOUND_MARGIN * DIFF_DH * jnp.max(jnp.abs(dq_gain)) * jnp.max(jnp.abs(dk_gain))
    lay = dict(
        scal=jnp.stack([lam, score_bound]).astype(F32),
        out_scale=1.0 - lam_init,
        attn_norm=p["attn_norm"][l].reshape(1, D_MODEL).astype(F32),
        w_in=p["w_in"][l].astype(BF16),
        gains=gains,
        g_sub=p["diff_sub_norm"][l].reshape(1, LANES).astype(F32),
        na_bias=_na_bias_table(p["na_rpb"][l]),
        g_na_out=jnp.tile(p["na_out_norm"][l].astype(F32), 2).reshape(1, LANES),
        w_out=p["w_out"][l].astype(BF16),
        ffn_norm=p["ffn_norm"][l].reshape(1, D_MODEL).astype(F32),
    )
    i = l // 2
    if l % 2 == 0:
        lay.update(wgu=_gate_up_blocks(p["dense_w_gate"][i], p["dense_w_up"][i]),
                   wd=p["dense_w_down"][i].astype(BF16))
    else:
        wr = p["moe_router"][i].astype(F32)
        wr_hi = wr.astype(BF16)
        wr_lo = (wr - wr_hi.astype(F32)).astype(BF16)
        wr2 = jnp.zeros((D_MODEL, LANES), BF16)
        wr2 = wr2.at[:, :N_EXPERTS].set(wr_hi).at[:, N_EXPERTS:2 * N_EXPERTS].set(wr_lo)
        lay.update(wr=wr2, wgu=_gate_up_blocks(p["moe_w_gate"][i], p["moe_w_up"][i]),
                   wd=p["moe_w_down"][i].astype(BF16))
    return lay


def _run_trunk(x, layers, bd):
    b, s, _ = x.shape
    t = b * s
    cos, sin = _rope_tables(s)
    x = x.reshape(t, D_MODEL)
    for l, lay in enumerate(layers):
        proj, vt = _in_proj(x, lay["attn_norm"], lay["w_in"], lay["gains"], cos, sin, bd, s)
        proj3 = proj.reshape(b, s, IN_WIDTH)
        a_out = _diff_attn(proj3, vt, lay["scal"], lay["g_sub"], lay["out_scale"])
        b_out = _na_attn(proj3, lay["na_bias"], lay["g_na_out"])
        a_out = a_out.reshape(t, HALF_WIDTH)
        b_out = b_out.reshape(t, HALF_WIDTH)
        if l % 2 == 0:
            x, h = _out_proj(x, a_out, b_out, lay["w_out"], lay["ffn_norm"])
            x = _swiglu(x, h, lay["wgu"], lay["wd"])
        else:
            x, h, c = _out_proj(x, a_out, b_out, lay["w_out"], lay["ffn_norm"], lay["wr"])
            x = _moe_sparse(x, h, c, lay["wgu"], lay["wd"])
    return x.reshape(b, s, D_MODEL)


def kernel(x_prompt, x_sample, attn_norm, w_in, w_out, diff_q_norm, diff_k_norm, lambda_q, lambda_k, diff_sub_norm, na_q_norm, na_k_norm, na_rpb, na_out_norm, ffn_norm, dense_w_gate, dense_w_up, dense_w_down, moe_router, moe_w_gate, moe_w_up, moe_w_down):
    p = dict(attn_norm=attn_norm, w_in=w_in, w_out=w_out, diff_q_norm=diff_q_norm, diff_k_norm=diff_k_norm,
             lambda_q=lambda_q, lambda_k=lambda_k, diff_sub_norm=diff_sub_norm, na_q_norm=na_q_norm,
             na_k_norm=na_k_norm, na_rpb=na_rpb, na_out_norm=na_out_norm, ffn_norm=ffn_norm,
             dense_w_gate=dense_w_gate, dense_w_up=dense_w_up, dense_w_down=dense_w_down,
             moe_router=moe_router, moe_w_gate=moe_w_gate, moe_w_up=moe_w_up, moe_w_down=moe_w_down)
    layers = [_prepare_layer(l, p) for l in range(DEPTH)]
    bd = _block_diag_ones()
    return (_run_trunk(x_prompt, layers, bd), _run_trunk(x_sample, layers, bd))
```

```python
import functools
import math

import jax
import jax.numpy as jnp
import numpy as np
from jax import lax
from jax.experimental import pallas as pl
from jax.experimental.pallas import tpu as pltpu

D_MODEL = 1024
DEPTH = 4
GRID_W = 64
DIFF_HEADS = 4
DIFF_DH = 64
NA_HEADS = 8
NA_DH = 64
NA_KH = 8
NA_KW = 16
HALF_WIDTH = 512
SEC_DQ, SEC_DK, SEC_DV, SEC_NQ, SEC_NK, SEC_NV = range(6)
IN_WIDTH = 6 * HALF_WIDTH
D_FF = 2816
FF_BLOCK = 1408
N_EXPERTS = 8
ROPE_THETA = 10000.0
EPS = 1e-6
MASK_VALUE = -1e30
SCORE_BOUND_MARGIN = 1.02
MAX_SAFE_SCORE_BOUND = 60.0

LANES = 128
MXU_TILE = 256
VMEM_LIMIT = 56 * 1024 * 1024

BF16 = jnp.bfloat16
F32 = jnp.float32


def _cparams(sem):
    return pltpu.CompilerParams(dimension_semantics=sem, vmem_limit_bytes=VMEM_LIMIT)


def _dot(a, b):
    return jnp.dot(a, b, preferred_element_type=F32)


def _dot_nt(a, b):
    return lax.dot_general(a, b, (((1,), (1,)), ((), ())), preferred_element_type=F32)


def _rms(x, g):
    return x * lax.rsqrt(jnp.mean(x * x, axis=-1, keepdims=True) + EPS) * g


def _group_norm64(y, gain, bd_ref):
    parts = []
    for c in range(HALF_WIDTH // MXU_TILE):
        yc = y[:, c * MXU_TILE:(c + 1) * MXU_TILE]
        ss = _dot((yc * yc).astype(BF16), bd_ref[...])
        parts.append(yc * lax.rsqrt(ss * (1.0 / 64) + EPS))
    return jnp.concatenate(parts, axis=1) * gain


def _rope64(y, cos, sin_signed):
    n = y.shape[1]
    lane = lax.broadcasted_iota(jnp.int32, y.shape, 1)
    first_half = (lane % 64) < 32
    rot = jnp.where(first_half, pltpu.roll(y, n - 32, 1), pltpu.roll(y, 32, 1))
    return y * cos + rot * sin_signed


def _in_proj_kernel(x_ref, g_ref, w_ref, gains_ref, cos_ref, sin_ref, bd_ref, o_ref, vt_ref):
    x = x_ref[...]
    h = _rms(x, g_ref[...]).astype(BF16)
    cos = jnp.concatenate([cos_ref[...]] * (HALF_WIDTH // LANES), axis=1)
    sin = jnp.concatenate([sin_ref[...]] * (HALF_WIDTH // LANES), axis=1)
    for sec in range(6):
        y = _dot(h, w_ref[:, sec * HALF_WIDTH:(sec + 1) * HALF_WIDTH])
        if sec in (SEC_DQ, SEC_DK, SEC_NQ, SEC_NK):
            y = _group_norm64(y, gains_ref[sec:sec + 1, :], bd_ref)
        if sec in (SEC_DQ, SEC_DK):
            y = _rope64(y, cos, sin)
        o_ref[:, sec * HALF_WIDTH:(sec + 1) * HALF_WIDTH] = y.astype(BF16)
        if sec == SEC_DV:
            vt_ref[0] = y.T.astype(BF16)


def _in_proj(x, g, w, gains, cos, sin, bd, seq, tm=512):
    t = x.shape[0]
    tm = min(tm, seq)
    nseq = seq // tm
    return pl.pallas_call(
        _in_proj_kernel,
        grid=(t // tm,),
        in_specs=[
            pl.BlockSpec((tm, D_MODEL), lambda i: (i, 0)),
            pl.BlockSpec((1, D_MODEL), lambda i: (0, 0)),
            pl.BlockSpec((D_MODEL, IN_WIDTH), lambda i: (0, 0)),
            pl.BlockSpec((8, HALF_WIDTH), lambda i: (0, 0)),
            pl.BlockSpec((tm, LANES), lambda i: (i % nseq, 0)),
            pl.BlockSpec((tm, LANES), lambda i: (i % nseq, 0)),
            pl.BlockSpec((MXU_TILE, MXU_TILE), lambda i: (0, 0)),
        ],
        out_specs=[pl.BlockSpec((tm, IN_WIDTH), lambda i: (i, 0)),
                   pl.BlockSpec((1, HALF_WIDTH, tm), lambda i: (i // nseq, 0, i % nseq))],
        out_shape=[jax.ShapeDtypeStruct((t, IN_WIDTH), BF16),
                   jax.ShapeDtypeStruct((t // seq, HALF_WIDTH, seq), BF16)],
        compiler_params=_cparams(("parallel",)),
        name="in_proj",
    )(x, g, w, gains, cos, sin, bd)


def _split_maps(q):
    lane = lax.broadcasted_iota(jnp.int32, q.shape, 1)
    zero = jnp.zeros_like(q)
    return jnp.where(lane < DIFF_DH, q, zero), jnp.where(lane >= DIFF_DH, q, zero)


def _diff_attn_online_kernel(scal_ref, q_ref, k_ref, v_ref, g_ref, o_ref, acc0_ref, acc1_ref, l0_ref, l1_ref, *,
                             tk, out_scale):
    qm = _split_maps(q_ref[0])
    tq = q_ref.shape[1]
    accs = (acc0_ref, acc1_ref)
    ls = (l0_ref, l1_ref)
    for r in accs + ls:
        r[...] = jnp.zeros_like(r)

    def lane_fold(p):
        out = p[:, :LANES]
        for c in range(1, tk // LANES):
            out = out + p[:, c * LANES:(c + 1) * LANES]
        return out

    def body(j, ms):
        start = pl.multiple_of(j * tk, tk)
        k = k_ref[0, pl.ds(start, tk), :]
        v = v_ref[0, pl.ds(start, tk), :]
        new_ms = []
        for a in range(2):
            s = _dot_nt(qm[a], k)
            m_new = jnp.maximum(ms[a], jnp.max(s, axis=-1, keepdims=True))
            alpha = jnp.exp2(ms[a] - m_new)
            p = jnp.exp2(s - m_new)
            ls[a][...] = alpha * ls[a][...] + lane_fold(p)
            accs[a][...] = alpha * accs[a][...] + _dot(p.astype(BF16), v)
            new_ms.append(m_new)
        return tuple(new_ms)

    neg = jnp.full((tq, 1), -jnp.inf, F32)
    lax.fori_loop(0, k_ref.shape[1] // tk, body, (neg, neg))
    lam = scal_ref[0]
    l0 = jnp.sum(l0_ref[...], axis=-1, keepdims=True)
    l1 = jnp.sum(l1_ref[...], axis=-1, keepdims=True)
    o = acc0_ref[...] / l0 - lam * (acc1_ref[...] / l1)
    o_ref[0] = (_rms(o, g_ref[...]) * out_scale).astype(BF16)


def _diff_attn_t_kernel(scal_ref, q_ref, k_ref, vt_ref, g_ref, o_ref, acc0_ref, acc1_ref, l0_ref, l1_ref, *,
                        tk, out_scale):
    qm = _split_maps(q_ref[0])
    accs = (acc0_ref, acc1_ref)
    ls = (l0_ref, l1_ref)
    for r in accs + ls:
        r[...] = jnp.zeros_like(r)
    nk = k_ref.shape[1] // tk
    tq = q_ref.shape[1]
    bound = scal_ref[1]

    def body(j, carry):
        start = pl.multiple_of(j * tk, tk)
        k = k_ref[0, pl.ds(start, tk), :]
        vt = vt_ref[0, :, pl.ds(start, tk)]
        for a in range(2):
            p = jnp.exp2(_dot_nt(k, qm[a]) - bound)
            ls[a][...] += jnp.sum(p.reshape(tk // 8, 8, tq), axis=0)
            accs[a][...] += _dot(vt, p.astype(BF16))
        return carry

    lax.fori_loop(0, nk, body, 0, unroll=2)
    lam = scal_ref[0]
    l0 = jnp.sum(l0_ref[...], axis=0, keepdims=True)
    l1 = jnp.sum(l1_ref[...], axis=0, keepdims=True)
    ot = acc0_ref[...] / l0 - lam * (acc1_ref[...] / l1)
    ot = ot * lax.rsqrt(jnp.mean(ot * ot, axis=0, keepdims=True) + EPS)
    o_ref[0] = (ot.T * g_ref[...] * out_scale).astype(BF16)


DIFF_TQ = 1024
DIFF_TK_BOUNDED = 2048
DIFF_TK_ONLINE = 1024


def _diff_attn_call(proj, vt, scal, g_sub, out_scale, bounded):
    b, s, _ = proj.shape
    tq = min(DIFF_TQ, s)
    q_spec = pl.BlockSpec((1, tq, LANES), lambda bi, h, i: (bi, i, h))
    k_spec = pl.BlockSpec((1, s, LANES), lambda bi, h, i: (bi, 0, DIFF_HEADS + h))
    if bounded:
        kern = functools.partial(_diff_attn_t_kernel, tk=min(DIFF_TK_BOUNDED, s), out_scale=out_scale)
        v, v_spec = vt, pl.BlockSpec((1, LANES, s), lambda bi, h, i: (bi, h, 0))
        scratch = [pltpu.VMEM((LANES, tq), F32)] * 2 + [pltpu.VMEM((8, tq), F32)] * 2
    else:
        kern = functools.partial(_diff_attn_online_kernel, tk=min(DIFF_TK_ONLINE, s), out_scale=out_scale)
        v, v_spec = proj, pl.BlockSpec((1, s, LANES), lambda bi, h, i: (bi, 0, 2 * DIFF_HEADS + h))
        scratch = [pltpu.VMEM((tq, LANES), F32)] * 4
    return pl.pallas_call(
        kern,
        grid=(b, DIFF_HEADS, s // tq),
        in_specs=[pl.BlockSpec(memory_space=pltpu.SMEM), q_spec, k_spec, v_spec,
                  pl.BlockSpec((1, LANES), lambda bi, h, i: (0, 0))],
        out_specs=pl.BlockSpec((1, tq, LANES), lambda bi, h, i: (bi, i, h)),
        out_shape=jax.ShapeDtypeStruct((b, s, HALF_WIDTH), BF16),
        scratch_shapes=scratch,
        compiler_params=_cparams(("parallel", "parallel", "arbitrary")),
        name="diff_attn_t" if bounded else "diff_attn_online",
    )(scal, proj, proj, v, g_sub)


def _diff_attn(proj, vt, scal, g_sub, out_scale):
    return lax.cond(
        scal[1] <= MAX_SAFE_SCORE_BOUND,
        lambda: _diff_attn_call(proj, vt, scal, g_sub, out_scale, True),
        lambda: _diff_attn_call(proj, vt, scal, g_sub, out_scale, False))


NA_ROWS_PER_STEP = 8
NA_BLOCK = NA_ROWS_PER_STEP * GRID_W


def _na_attn_kernel(q_ref, kp_ref, kc_ref, kn_ref, vp_ref, vc_ref, vn_ref, bias_ref, g_ref, o_ref,
                    kbuf, vbuf, *, rows):
    i = pl.program_id(1)
    for n, (kr, vr) in enumerate(((kp_ref, vp_ref), (kc_ref, vc_ref), (kn_ref, vn_ref))):
        kbuf[n * NA_BLOCK:(n + 1) * NA_BLOCK, :] = kr[0]
        vbuf[n * NA_BLOCK:(n + 1) * NA_BLOCK, :] = vr[0]
    lane = lax.broadcasted_iota(jnp.int32, (GRID_W, LANES), 1)
    halves = (lane < NA_DH, lane >= NA_DH)

    def row_body(qr, carry):
        r = i * NA_ROWS_PER_STEP + qr
        r0 = jnp.clip(r - NA_KH // 2, 0, rows - NA_KH)
        start = pl.multiple_of((r0 - (i - 1) * NA_ROWS_PER_STEP) * GRID_W, GRID_W)
        rp = r - r0
        qrow = pl.multiple_of(qr * GRID_W, GRID_W)
        npairs = NA_HEADS // 2
        col = [slice(p * LANES, (p + 1) * LANES) for p in range(npairs)]
        scores = []
        for p in range(npairs):
            qp = q_ref[0, pl.ds(qrow, GRID_W), col[p]]
            kp = kbuf[pl.ds(start, NA_KH * GRID_W), col[p]]
            q2 = jnp.concatenate([jnp.where(h, qp, jnp.zeros_like(qp)) for h in halves], axis=0)
            scores.append(_dot_nt(q2, kp) + bias_ref[p, rp])
        probs = []
        for s in scores:
            m = jnp.max(s, axis=-1, keepdims=True)
            pe = jnp.exp2(s - m)
            probs.append((pe.astype(BF16), jnp.sum(pe, axis=-1, keepdims=True)))
        for p in range(npairs):
            vp = vbuf[pl.ds(start, NA_KH * GRID_W), col[p]]
            pe, l = probs[p]
            o_both = _dot(pe, vp) / l
            o = jnp.where(halves[0], o_both[:GRID_W], o_both[GRID_W:])
            o2 = o * o
            ss = [jnp.sum(jnp.where(h, o2, 0.0), axis=-1, keepdims=True) for h in halves]
            inv = jnp.where(halves[0], lax.rsqrt(ss[0] * (1.0 / NA_DH) + EPS),
                            lax.rsqrt(ss[1] * (1.0 / NA_DH) + EPS))
            o_ref[0, pl.ds(qrow, GRID_W), col[p]] = (o * inv * g_ref[...]).astype(BF16)
        return carry

    lax.fori_loop(0, NA_ROWS_PER_STEP, row_body, 0, unroll=True)


def _na_attn(proj, bias, g_out):
    b, s, _ = proj.shape
    rows = s // GRID_W
    nb = rows // NA_ROWS_PER_STEP
    qcol, kcol, vcol = SEC_NQ, SEC_NK, SEC_NV
    blk = (1, NA_BLOCK, HALF_WIDTH)

    def prev(bi, i, c):
        return (bi, jnp.maximum(i - 1, 0), c)

    def nxt(bi, i, c):
        return (bi, jnp.minimum(i + 1, nb - 1), c)

    return pl.pallas_call(
        functools.partial(_na_attn_kernel, rows=rows),
        grid=(b, nb),
        in_specs=[
            pl.BlockSpec(blk, lambda bi, i: (bi, i, qcol)),
            pl.BlockSpec(blk, lambda bi, i: prev(bi, i, kcol)),
            pl.BlockSpec(blk, lambda bi, i: (bi, i, kcol)),
            pl.BlockSpec(blk, lambda bi, i: nxt(bi, i, kcol)),
            pl.BlockSpec(blk, lambda bi, i: prev(bi, i, vcol)),
            pl.BlockSpec(blk, lambda bi, i: (bi, i, vcol)),
            pl.BlockSpec(blk, lambda bi, i: nxt(bi, i, vcol)),
            pl.BlockSpec((NA_HEADS // 2, NA_KH, 2 * GRID_W, NA_KH * GRID_W), lambda bi, i: (0, 0, 0, 0)),
            pl.BlockSpec((1, LANES), lambda bi, i: (0, 0)),
        ],
        out_specs=pl.BlockSpec(blk, lambda bi, i: (bi, i, 0)),
        out_shape=jax.ShapeDtypeStruct((b, s, HALF_WIDTH), BF16),
        scratch_shapes=[pltpu.VMEM((3 * NA_BLOCK, HALF_WIDTH), BF16),
                        pltpu.VMEM((3 * NA_BLOCK, HALF_WIDTH), BF16)],
        compiler_params=_cparams(("parallel", "arbitrary")),
        name="na_attn",
    )(proj, proj, proj, proj, proj, proj, proj, bias, g_out)


def _na_bias_table(rpb):
    qc = np.arange(GRID_W)
    kc = np.arange(GRID_W)
    win = np.clip(qc - NA_KW // 2, 0, GRID_W - NA_KW)
    valid = (kc[None, :] >= win[:, None]) & (kc[None, :] < win[:, None] + NA_KW)
    padded = jnp.pad(rpb.astype(F32) * math.log2(math.e), ((0, 0), (0, 0), (GRID_W, GRID_W)))
    by_col = jnp.stack([padded[:, :, GRID_W + NA_KW - 1 - c:2 * GRID_W + NA_KW - 1 - c] for c in range(GRID_W)],
                       axis=2)
    by_col = jnp.where(jnp.asarray(valid)[None, None], by_col, MASK_VALUE)
    t = jnp.stack([by_col[:, NA_KH - 1 - r:2 * NA_KH - 1 - r] for r in range(NA_KH)], axis=1)
    t = t.reshape(NA_HEADS // 2, 2, NA_KH, NA_KH, GRID_W, GRID_W)
    t = t.transpose(0, 2, 1, 4, 3, 5)
    return t.reshape(NA_HEADS // 2, NA_KH, 2 * GRID_W, NA_KH * GRID_W)


def _out_proj_kernel(x_ref, a_ref, b_ref, w_ref, g_ref, xo_ref, h_ref):
    x = x_ref[...] + _dot(a_ref[...], w_ref[:HALF_WIDTH, :]) + _dot(b_ref[...], w_ref[HALF_WIDTH:, :])
    xo_ref[...] = x
    h_ref[...] = _rms(x, g_ref[...]).astype(BF16)


def _out_proj_router_kernel(x_ref, a_ref, b_ref, w_ref, g_ref, wr_ref, xo_ref, h_ref, c_ref):
    x = x_ref[...] + _dot(a_ref[...], w_ref[:HALF_WIDTH, :]) + _dot(b_ref[...], w_ref[HALF_WIDTH:, :])
    xo_ref[...] = x
    h = _rms(x, g_ref[...])
    h_hi = h.astype(BF16)
    h_ref[...] = h_hi
    h_lo = (h - h_hi.astype(F32)).astype(BF16)
    wr = wr_ref[...]
    both = _dot(h_hi, wr)
    lane = lax.broadcasted_iota(jnp.int32, both.shape, 1).astype(F32)
    lo_part = pltpu.roll(both, LANES - N_EXPERTS, 1)
    logits = both + lo_part + _dot(h_lo, wr)
    logits = jnp.where(lane < N_EXPERTS, logits, -jnp.inf)
    m1 = jnp.max(logits, axis=-1, keepdims=True)
    i1 = jnp.min(jnp.where(logits == m1, lane, float(LANES)), axis=-1, keepdims=True)
    rest = jnp.where(lane == i1, -jnp.inf, logits)
    m2 = jnp.max(rest, axis=-1, keepdims=True)
    i2 = jnp.min(jnp.where(rest == m2, lane, float(LANES)), axis=-1, keepdims=True)
    e2 = jnp.exp(m2 - m1)
    g1 = 1.0 / (1.0 + e2)
    g2 = e2 * g1
    c_ref[...] = jnp.where(lane == i1, g1, jnp.where(lane == i2, g2, 0.0))


def _out_proj(x, a, b, w, g, wr=None, tm=512):
    t = x.shape[0]
    tm = min(tm, t)
    row = lambda i: (i, 0)
    const = lambda i: (0, 0)
    in_specs = [
        pl.BlockSpec((tm, D_MODEL), row),
        pl.BlockSpec((tm, HALF_WIDTH), row),
        pl.BlockSpec((tm, HALF_WIDTH), row),
        pl.BlockSpec((D_MODEL, D_MODEL), const),
        pl.BlockSpec((1, D_MODEL), const),
    ]
    out_specs = [pl.BlockSpec((tm, D_MODEL), row), pl.BlockSpec((tm, D_MODEL), row)]
    out_shape = [jax.ShapeDtypeStruct((t, D_MODEL), F32), jax.ShapeDtypeStruct((t, D_MODEL), BF16)]
    args = [x, a, b, w, g]
    kern = _out_proj_kernel
    if wr is not None:
        in_specs.append(pl.BlockSpec((D_MODEL, LANES), const))
        out_specs.append(pl.BlockSpec((tm, LANES), row))
        out_shape.append(jax.ShapeDtypeStruct((t, LANES), F32))
        args.append(wr)
        kern = _out_proj_router_kernel
    return pl.pallas_call(
        kern,
        grid=(t // tm,),
        in_specs=in_specs,
        out_specs=out_specs,
        out_shape=out_shape,
        compiler_params=_cparams(("parallel",)),
        name="out_proj",
    )(*args)


def _gate_up(h, wgu):
    r = _dot(h, wgu)
    g, u = r[:, :FF_BLOCK], r[:, FF_BLOCK:]
    return (g * jax.nn.sigmoid(g) * u).astype(BF16)


def _swiglu_kernel(x_ref, h_ref, wgu_ref, wd_ref, o_ref):
    @pl.when(pl.program_id(1) == 0)
    def _():
        o_ref[...] = x_ref[...]

    o_ref[...] += _dot(_gate_up(h_ref[...], wgu_ref[...]), wd_ref[...])


def _swiglu(x, h, wgu, wd, tm=1024):
    t = x.shape[0]
    tf = FF_BLOCK
    tm = min(tm, t)
    return pl.pallas_call(
        _swiglu_kernel,
        grid=(t // tm, D_FF // tf),
        in_specs=[
            pl.BlockSpec((tm, D_MODEL), lambda i, f: (i, 0)),
            pl.BlockSpec((tm, D_MODEL), lambda i, f: (i, 0)),
            pl.BlockSpec((D_MODEL, 2 * tf), lambda i, f: (0, f)),
            pl.BlockSpec((tf, D_MODEL), lambda i, f: (f, 0)),
        ],
        out_specs=pl.BlockSpec((tm, D_MODEL), lambda i, f: (i, 0)),
        out_shape=jax.ShapeDtypeStruct((t, D_MODEL), F32),
        compiler_params=_cparams(("parallel", "arbitrary")),
        name="swiglu",
    )(x, h, wgu, wd)


MOE_TILE = 1024
MOE_CHUNK = 256
MOE_TAIL = 128
TRI_BLOCK = 512


def _route_kernel(c_ref, utri_ref, pos_ref, gate_ref):
    tm = c_ref.shape[0]
    carry = jnp.zeros((LANES, 1), F32)
    for b in range(tm // TRI_BLOCK):
        rows = slice(b * TRI_BLOCK, (b + 1) * TRI_BLOCK)
        gate_t = c_ref[rows, :].T
        sel_t = jnp.where(gate_t > 0.0, 1.0, 0.0)
        pos_t = _dot(sel_t.astype(BF16), utri_ref[...]) + carry
        pos_ref[0, :, rows] = jnp.where(gate_t > 0.0, pos_t, -1.0)
        gate_ref[0, :, rows] = gate_t
        carry = carry + jnp.sum(sel_t, axis=1, keepdims=True)


def _route(c, tm):
    t = c.shape[0]
    idx = np.arange(TRI_BLOCK)
    utri = jnp.asarray(idx[:, None] < idx[None, :], dtype=BF16)
    expert_major = jax.ShapeDtypeStruct((t // tm, LANES, tm), F32)
    return pl.pallas_call(
        _route_kernel,
        grid=(t // tm,),
        in_specs=[
            pl.BlockSpec((tm, LANES), lambda i: (i, 0)),
            pl.BlockSpec((TRI_BLOCK, TRI_BLOCK), lambda i: (0, 0)),
        ],
        out_specs=[pl.BlockSpec((1, LANES, tm), lambda i: (i, 0, 0))] * 2,
        out_shape=[expert_major, expert_major],
        compiler_params=_cparams(("parallel",)),
        name="moe_route",
    )(c, utri)


def _moe_sparse_kernel(x_ref, h_ref, posmt_ref, gatet_ref, wgu_ref, wd_ref, o_ref, hc_ref, y_ref):
    e = pl.program_id(1)
    f = pl.program_id(2)
    last_f = pl.num_programs(2) - 1
    tm = x_ref.shape[0]

    @pl.when((e == 0) & (f == 0))
    def _():
        o_ref[...] = x_ref[...]

    pos_row = posmt_ref[0, pl.ds(e, 1), :]
    load = (jnp.max(pos_row) + 1.0).astype(jnp.int32)
    rem = load % MOE_CHUNK
    nfull = load // MOE_CHUNK + (rem > MOE_TAIL).astype(jnp.int32)
    has_tail = (rem > 0) & (rem <= MOE_TAIL)
    tail_off = nfull * MOE_CHUNK

    def rows_of(off, size):
        return pl.ds(pl.multiple_of(off, MOE_TAIL), size)

    def for_each_chunk(fn):
        def full(k, carry):
            fn(k * MOE_CHUNK, MOE_CHUNK)
            return carry

        lax.fori_loop(0, nfull, full, 0)

        @pl.when(has_tail)
        def _():
            fn(tail_off, MOE_TAIL)

    def hits(off, size):
        slot = (lax.broadcasted_iota(jnp.int32, (size, tm), 0) + off).astype(F32)
        return pos_row == slot

    def expert(off, size):
        return _dot(_gate_up(hc_ref[rows_of(off, size), :], wgu_ref[0]), wd_ref[0])

    def first(off, size):
        onehot = jnp.where(hits(off, size), 1.0, 0.0).astype(BF16)
        hc_ref[rows_of(off, size), :] = _dot(onehot, h_ref[...]).astype(BF16)
        y_ref[rows_of(off, size), :] = expert(off, size)

    def middle(off, size):
        y_ref[rows_of(off, size), :] += expert(off, size)

    def last(off, size):
        hit = hits(off, size)
        gate = jnp.sum(jnp.where(hit, gatet_ref[0, pl.ds(e, 1), :], 0.0), axis=-1, keepdims=True)
        ys = ((y_ref[rows_of(off, size), :] + expert(off, size)) * gate).astype(BF16)
        onehot = jnp.where(hit, 1.0, 0.0).astype(BF16)
        o_ref[...] += lax.dot_general(onehot, ys, (((0,), (0,)), ((), ())), preferred_element_type=F32)

    @pl.when(f == 0)
    def _():
        for_each_chunk(first)

    if D_FF // FF_BLOCK > 2:
        @pl.when((f > 0) & (f < last_f))
        def _():
            for_each_chunk(middle)

    @pl.when(f == last_f)
    def _():
        for_each_chunk(last)


def _moe_sparse(x, h, c, wgu, wd, tm=MOE_TILE):
    t = x.shape[0]
    tf = FF_BLOCK
    tm = min(tm, t)
    posmt, gatet = _route(c, tm)
    return pl.pallas_call(
        _moe_sparse_kernel,
        grid=(t // tm, N_EXPERTS, D_FF // tf),
        in_specs=[
            pl.BlockSpec((tm, D_MODEL), lambda i, e, f: (i, 0)),
            pl.BlockSpec((tm, D_MODEL), lambda i, e, f: (i, 0)),
            pl.BlockSpec((1, LANES, tm), lambda i, e, f: (i, 0, 0)),
            pl.BlockSpec((1, LANES, tm), lambda i, e, f: (i, 0, 0)),
            pl.BlockSpec((1, D_MODEL, 2 * tf), lambda i, e, f: (e, 0, f)),
            pl.BlockSpec((1, tf, D_MODEL), lambda i, e, f: (e, f, 0)),
        ],
        out_specs=pl.BlockSpec((tm, D_MODEL), lambda i, e, f: (i, 0)),
        out_shape=jax.ShapeDtypeStruct((t, D_MODEL), F32),
        scratch_shapes=[pltpu.VMEM((tm, D_MODEL), BF16), pltpu.VMEM((tm, D_MODEL), F32)],
        compiler_params=_cparams(("parallel", "arbitrary", "arbitrary")),
        name="moe_sparse",
    )(x, h, posmt, gatet, wgu, wd)


def _rope_tables(seq):
    inv = 1.0 / (ROPE_THETA ** (jnp.arange(0, DIFF_DH, 2, dtype=F32) / DIFF_DH))
    ang = jnp.arange(seq, dtype=F32)[:, None] * inv[None, :]
    ang = jnp.concatenate([ang, ang], axis=-1)
    cos, sin = jnp.cos(ang), jnp.sin(ang)
    sign = jnp.where(jnp.arange(DIFF_DH) < DIFF_DH // 2, -1.0, 1.0).astype(F32)
    return jnp.tile(cos, (1, 2)), jnp.tile(sin * sign, (1, 2))


def _block_diag_ones():
    idx = np.arange(MXU_TILE) // DIFF_DH
    return jnp.asarray(idx[:, None] == idx[None, :], dtype=BF16)


def _gate_up_blocks(w_gate, w_up):
    pieces = []
    for f in range(D_FF // FF_BLOCK):
        cols = slice(f * FF_BLOCK, (f + 1) * FF_BLOCK)
        pieces += [w_gate[..., cols], w_up[..., cols]]
    return jnp.concatenate(pieces, axis=-1).astype(BF16)


def _prepare_layer(l, p):
    lam_init = 0.8 - 0.6 * math.exp(-0.3 * l)
    lq = p["lambda_q"][l].astype(F32)
    lk = p["lambda_k"][l].astype(F32)
    lam = jnp.exp(jnp.sum(lq[0] * lk[0])) - jnp.exp(jnp.sum(lq[1] * lk[1])) + lam_init
    scale = DIFF_DH ** -0.5 * math.log2(math.e)
    tile8 = lambda g: jnp.tile(g.astype(F32), HALF_WIDTH // g.shape[0])
    zeros = jnp.zeros((HALF_WIDTH,), F32)
    dq_gain = p["diff_q_norm"][l].astype(F32) * scale
    dk_gain = p["diff_k_norm"][l].astype(F32)
    gains = jnp.stack([
        tile8(dq_gain), tile8(dk_gain), zeros,
        tile8(p["na_q_norm"][l]) * scale, tile8(p["na_k_norm"][l]), zeros, zeros, zeros])
    score_bound = SCORE_BOUND_MARGIN * DIFF_DH * jnp.max(jnp.abs(dq_gain)) * jnp.max(jnp.abs(dk_gain))
    lay = dict(
        scal=jnp.stack([lam, score_bound]).astype(F32),
        out_scale=1.0 - lam_init,
        attn_norm=p["attn_norm"][l].reshape(1, D_MODEL).astype(F32),
        w_in=p["w_in"][l].astype(BF16),
        gains=gains,
        g_sub=p["diff_sub_norm"][l].reshape(1, LANES).astype(F32),
        na_bias=_na_bias_table(p["na_rpb"][l]),
        g_na_out=jnp.tile(p["na_out_norm"][l].astype(F32), 2).reshape(1, LANES),
        w_out=p["w_out"][l].astype(BF16),
        ffn_norm=p["ffn_norm"][l].reshape(1, D_MODEL).astype(F32),
    )
    i = l // 2
    if l % 2 == 0:
        lay.update(wgu=_gate_up_blocks(p["dense_w_gate"][i], p["dense_w_up"][i]),
                   wd=p["dense_w_down"][i].astype(BF16))
    else:
        wr = p["moe_router"][i].astype(F32)
        wr_hi = wr.astype(BF16)
        wr_lo = (wr - wr_hi.astype(F32)).astype(BF16)
        wr2 = jnp.zeros((D_MODEL, LANES), BF16)
        wr2 = wr2.at[:, :N_EXPERTS].set(wr_hi).at[:, N_EXPERTS:2 * N_EXPERTS].set(wr_lo)
        lay.update(wr=wr2, wgu=_gate_up_blocks(p["moe_w_gate"][i], p["moe_w_up"][i]),
                   wd=p["moe_w_down"][i].astype(BF16))
    return lay


def _run_trunk(x, layers, bd):
    b, s, _ = x.shape
    t = b * s
    cos, sin = _rope_tables(s)
    x = x.reshape(t, D_MODEL)
    for l, lay in enumerate(layers):
        proj, vt = _in_proj(x, lay["attn_norm"], lay["w_in"], lay["gains"], cos, sin, bd, s)
        proj3 = proj.reshape(b, s, IN_WIDTH)
        a_out = _diff_attn(proj3, vt, lay["scal"], lay["g_sub"], lay["out_scale"])
        b_out = _na_attn(proj3, lay["na_bias"], lay["g_na_out"])
        a_out = a_out.reshape(t, HALF_WIDTH)
        b_out = b_out.reshape(t, HALF_WIDTH)
        if l % 2 == 0:
            x, h = _out_proj(x, a_out, b_out, lay["w_out"], lay["ffn_norm"])
            x = _swiglu(x, h, lay["wgu"], lay["wd"])
        else:
            x, h, c = _out_proj(x, a_out, b_out, lay["w_out"], lay["ffn_norm"], lay["wr"])
            x = _moe_sparse(x, h, c, lay["wgu"], lay["wd"])
    return x.reshape(b, s, D_MODEL)


def kernel(x_prompt, x_sample, attn_norm, w_in, w_out, diff_q_norm, diff_k_norm, lambda_q, lambda_k, diff_sub_norm, na_q_norm, na_k_norm, na_rpb, na_out_norm, ffn_norm, dense_w_gate, dense_w_up, dense_w_down, moe_router, moe_w_gate, moe_w_up, moe_w_down):
    p = dict(attn_norm=attn_norm, w_in=w_in, w_out=w_out, diff_q_norm=diff_q_norm, diff_k_norm=diff_k_norm,
             lambda_q=lambda_q, lambda_k=lambda_k, diff_sub_norm=diff_sub_norm, na_q_norm=na_q_norm,
             na_k_norm=na_k_norm, na_rpb=na_rpb, na_out_norm=na_out_norm, ffn_norm=ffn_norm,
             dense_w_gate=dense_w_gate, dense_w_up=dense_w_up, dense_w_down=dense_w_down,
             moe_router=moe_router, moe_w_gate=moe_w_gate, moe_w_up=moe_w_up, moe_w_down=moe_w_down)
    layers = [_prepare_layer(l, p) for l in range(DEPTH)]
    bd = _block_diag_ones()
    return (_run_trunk(x_prompt, layers, bd), _run_trunk(x_sample, layers, bd))
```

```python
import functools
import math

import jax
import jax.numpy as jnp
import numpy as np
from jax import lax
from jax.experimental import pallas as pl
from jax.experimental.pallas import tpu as pltpu

D_MODEL = 1024
DEPTH = 4
GRID_W = 64
DIFF_HEADS = 4
DIFF_DH = 64
NA_HEADS = 8
NA_DH = 64
NA_KH = 8
NA_KW = 16
HALF_WIDTH = 512
SEC_DQ, SEC_DK, SEC_DV, SEC_NQ, SEC_NK, SEC_NV = range(6)
IN_WIDTH = 6 * HALF_WIDTH
D_FF = 2816
FF_BLOCK = 1408
N_EXPERTS = 8
ROPE_THETA = 10000.0
EPS = 1e-6
MASK_VALUE = -1e30
SCORE_BOUND_MARGIN = 1.02
MAX_SAFE_SCORE_BOUND = 60.0

LANES = 128
MXU_TILE = 256
VMEM_LIMIT = 56 * 1024 * 1024

BF16 = jnp.bfloat16
F32 = jnp.float32


def _cparams(sem):
    return pltpu.CompilerParams(dimension_semantics=sem, vmem_limit_bytes=VMEM_LIMIT)


def _dot(a, b):
    return jnp.dot(a, b, preferred_element_type=F32)


def _dot_nt(a, b):
    return lax.dot_general(a, b, (((1,), (1,)), ((), ())), preferred_element_type=F32)


def _rms(x, g):
    return x * lax.rsqrt(jnp.mean(x * x, axis=-1, keepdims=True) + EPS) * g


def _group_norm64(y, gain, bd_ref):
    parts = []
    for c in range(HALF_WIDTH // MXU_TILE):
        yc = y[:, c * MXU_TILE:(c + 1) * MXU_TILE]
        ss = _dot((yc * yc).astype(BF16), bd_ref[...])
        parts.append(yc * lax.rsqrt(ss * (1.0 / 64) + EPS))
    return jnp.concatenate(parts, axis=1) * gain


def _rope64(y, cos, sin_signed):
    n = y.shape[1]
    lane = lax.broadcasted_iota(jnp.int32, y.shape, 1)
    first_half = (lane % 64) < 32
    rot = jnp.where(first_half, pltpu.roll(y, n - 32, 1), pltpu.roll(y, 32, 1))
    return y * cos + rot * sin_signed


def _in_proj_kernel(x_ref, g_ref, w_ref, gains_ref, cos_ref, sin_ref, bd_ref, o_ref, vt_ref):
    x = x_ref[...]
    h = _rms(x, g_ref[...]).astype(BF16)
    cos = jnp.concatenate([cos_ref[...]] * (HALF_WIDTH // LANES), axis=1)
    sin = jnp.concatenate([sin_ref[...]] * (HALF_WIDTH // LANES), axis=1)
    for sec in range(6):
        y = _dot(h, w_ref[:, sec * HALF_WIDTH:(sec + 1) * HALF_WIDTH])
        if sec in (SEC_DQ, SEC_DK, SEC_NQ, SEC_NK):
            y = _group_norm64(y, gains_ref[sec:sec + 1, :], bd_ref)
        if sec in (SEC_DQ, SEC_DK):
            y = _rope64(y, cos, sin)
        o_ref[:, sec * HALF_WIDTH:(sec + 1) * HALF_WIDTH] = y.astype(BF16)
        if sec == SEC_DV:
            vt_ref[0] = y.T.astype(BF16)


def _in_proj(x, g, w, gains, cos, sin, bd, seq, tm=512):
    t = x.shape[0]
    tm = min(tm, seq)
    nseq = seq // tm
    return pl.pallas_call(
        _in_proj_kernel,
        grid=(t // tm,),
        in_specs=[
            pl.BlockSpec((tm, D_MODEL), lambda i: (i, 0)),
            pl.BlockSpec((1, D_MODEL), lambda i: (0, 0)),
            pl.BlockSpec((D_MODEL, IN_WIDTH), lambda i: (0, 0)),
            pl.BlockSpec((8, HALF_WIDTH), lambda i: (0, 0)),
            pl.BlockSpec((tm, LANES), lambda i: (i % nseq, 0)),
            pl.BlockSpec((tm, LANES), lambda i: (i % nseq, 0)),
            pl.BlockSpec((MXU_TILE, MXU_TILE), lambda i: (0, 0)),
        ],
        out_specs=[pl.BlockSpec((tm, IN_WIDTH), lambda i: (i, 0)),
                   pl.BlockSpec((1, HALF_WIDTH, tm), lambda i: (i // nseq, 0, i % nseq))],
        out_shape=[jax.ShapeDtypeStruct((t, IN_WIDTH), BF16),
                   jax.ShapeDtypeStruct((t // seq, HALF_WIDTH, seq), BF16)],
        compiler_params=_cparams(("parallel",)),
        name="in_proj",
    )(x, g, w, gains, cos, sin, bd)


def _split_maps(q):
    lane = lax.broadcasted_iota(jnp.int32, q.shape, 1)
    zero = jnp.zeros_like(q)
    return jnp.where(lane < DIFF_DH, q, zero), jnp.where(lane >= DIFF_DH, q, zero)


def _diff_attn_online_kernel(scal_ref, q_ref, k_ref, v_ref, g_ref, o_ref, acc0_ref, acc1_ref, l0_ref, l1_ref, *,
                             tk, out_scale):
    qm = _split_maps(q_ref[0])
    tq = q_ref.shape[1]
    accs = (acc0_ref, acc1_ref)
    ls = (l0_ref, l1_ref)
    for r in accs + ls:
        r[...] = jnp.zeros_like(r)

    def lane_fold(p):
        out = p[:, :LANES]
        for c in range(1, tk // LANES):
            out = out + p[:, c * LANES:(c + 1) * LANES]
        return out

    def body(j, ms):
        start = pl.multiple_of(j * tk, tk)
        k = k_ref[0, pl.ds(start, tk), :]
        v = v_ref[0, pl.ds(start, tk), :]
        new_ms = []
        for a in range(2):
            s = _dot_nt(qm[a], k)
            m_new = jnp.maximum(ms[a], jnp.max(s, axis=-1, keepdims=True))
            alpha = jnp.exp2(ms[a] - m_new)
            p = jnp.exp2(s - m_new)
            ls[a][...] = alpha * ls[a][...] + lane_fold(p)
            accs[a][...] = alpha * accs[a][...] + _dot(p.astype(BF16), v)
            new_ms.append(m_new)
        return tuple(new_ms)

    neg = jnp.full((tq, 1), -jnp.inf, F32)
    lax.fori_loop(0, k_ref.shape[1] // tk, body, (neg, neg))
    lam = scal_ref[0]
    l0 = jnp.sum(l0_ref[...], axis=-1, keepdims=True)
    l1 = jnp.sum(l1_ref[...], axis=-1, keepdims=True)
    o = acc0_ref[...] / l0 - lam * (acc1_ref[...] / l1)
    o_ref[0] = (_rms(o, g_ref[...]) * out_scale).astype(BF16)


def _diff_attn_t_kernel(scal_ref, q_ref, k_ref, vt_ref, g_ref, o_ref, acc0_ref, acc1_ref, l0_ref, l1_ref, *,
                        tk, out_scale):
    qm = _split_maps(q_ref[0])
    accs = (acc0_ref, acc1_ref)
    ls = (l0_ref, l1_ref)
    for r in accs + ls:
        r[...] = jnp.zeros_like(r)
    nk = k_ref.shape[1] // tk
    tq = q_ref.shape[1]
    bound = scal_ref[1]

    def body(j, carry):
        start = pl.multiple_of(j * tk, tk)
        k = k_ref[0, pl.ds(start, tk), :]
        vt = vt_ref[0, :, pl.ds(start, tk)]
        for a in range(2):
            p = jnp.exp2(_dot_nt(k, qm[a]) - bound)
            ls[a][...] += jnp.sum(p.reshape(tk // 8, 8, tq), axis=0)
            accs[a][...] += _dot(vt, p.astype(BF16))
        return carry

    lax.fori_loop(0, nk, body, 0, unroll=2)
    lam = scal_ref[0]
    l0 = jnp.sum(l0_ref[...], axis=0, keepdims=True)
    l1 = jnp.sum(l1_ref[...], axis=0, keepdims=True)
    ot = acc0_ref[...] / l0 - lam * (acc1_ref[...] / l1)
    ot = ot * lax.rsqrt(jnp.mean(ot * ot, axis=0, keepdims=True) + EPS)
    o_ref[0] = (ot.T * g_ref[...] * out_scale).astype(BF16)


DIFF_TQ = 1024
DIFF_TK_BOUNDED = 2048
DIFF_TK_ONLINE = 1024


def _diff_attn_call(proj, vt, scal, g_sub, out_scale, bounded):
    b, s, _ = proj.shape
    tq = min(DIFF_TQ, s)
    q_spec = pl.BlockSpec((1, tq, LANES), lambda bi, h, i: (bi, i, h))
    k_spec = pl.BlockSpec((1, s, LANES), lambda bi, h, i: (bi, 0, DIFF_HEADS + h))
    if bounded:
        kern = functools.partial(_diff_attn_t_kernel, tk=min(DIFF_TK_BOUNDED, s), out_scale=out_scale)
        v, v_spec = vt, pl.BlockSpec((1, LANES, s), lambda bi, h, i: (bi, h, 0))
        scratch = [pltpu.VMEM((LANES, tq), F32)] * 2 + [pltpu.VMEM((8, tq), F32)] * 2
    else:
        kern = functools.partial(_diff_attn_online_kernel, tk=min(DIFF_TK_ONLINE, s), out_scale=out_scale)
        v, v_spec = proj, pl.BlockSpec((1, s, LANES), lambda bi, h, i: (bi, 0, 2 * DIFF_HEADS + h))
        scratch = [pltpu.VMEM((tq, LANES), F32)] * 4
    return pl.pallas_call(
        kern,
        grid=(b, DIFF_HEADS, s // tq),
        in_specs=[pl.BlockSpec(memory_space=pltpu.SMEM), q_spec, k_spec, v_spec,
                  pl.BlockSpec((1, LANES), lambda bi, h, i: (0, 0))],
        out_specs=pl.BlockSpec((1, tq, LANES), lambda bi, h, i: (bi, i, h)),
        out_shape=jax.ShapeDtypeStruct((b, s, HALF_WIDTH), BF16),
        scratch_shapes=scratch,
        compiler_params=_cparams(("parallel", "parallel", "arbitrary")),
        name="diff_attn_t" if bounded else "diff_attn_online",
    )(scal, proj, proj, v, g_sub)


def _diff_attn(proj, vt, scal, g_sub, out_scale):
    return lax.cond(
        scal[1] <= MAX_SAFE_SCORE_BOUND,
        lambda: _diff_attn_call(proj, vt, scal, g_sub, out_scale, True),
        lambda: _diff_attn_call(proj, vt, scal, g_sub, out_scale, False))


NA_ROWS_PER_STEP = 8
NA_BLOCK = NA_ROWS_PER_STEP * GRID_W


def _na_attn_kernel(q_ref, kp_ref, kc_ref, kn_ref, vp_ref, vc_ref, vn_ref, bias_ref, g_ref, o_ref,
                    kbuf, vbuf, *, rows):
    i = pl.program_id(1)
    for n, (kr, vr) in enumerate(((kp_ref, vp_ref), (kc_ref, vc_ref), (kn_ref, vn_ref))):
        kbuf[n * NA_BLOCK:(n + 1) * NA_BLOCK, :] = kr[0]
        vbuf[n * NA_BLOCK:(n + 1) * NA_BLOCK, :] = vr[0]
    lane = lax.broadcasted_iota(jnp.int32, (GRID_W, LANES), 1)
    halves = (lane < NA_DH, lane >= NA_DH)

    def row_body(qr, carry):
        r = i * NA_ROWS_PER_STEP + qr
        r0 = jnp.clip(r - NA_KH // 2, 0, rows - NA_KH)
        start = pl.multiple_of((r0 - (i - 1) * NA_ROWS_PER_STEP) * GRID_W, GRID_W)
        rp = r - r0
        qrow = pl.multiple_of(qr * GRID_W, GRID_W)
        npairs = NA_HEADS // 2
        col = [slice(p * LANES, (p + 1) * LANES) for p in range(npairs)]
        scores = []
        for p in range(npairs):
            qp = q_ref[0, pl.ds(qrow, GRID_W), col[p]]
            kp = kbuf[pl.ds(start, NA_KH * GRID_W), col[p]]
            q2 = jnp.concatenate([jnp.where(h, qp, jnp.zeros_like(qp)) for h in halves], axis=0)
            scores.append(_dot_nt(q2, kp) + bias_ref[p, rp])
        probs = []
        for s in scores:
            m = jnp.max(s, axis=-1, keepdims=True)
            pe = jnp.exp2(s - m)
            probs.append((pe.astype(BF16), jnp.sum(pe, axis=-1, keepdims=True)))
        for p in range(npairs):
            vp = vbuf[pl.ds(start, NA_KH * GRID_W), col[p]]
            pe, l = probs[p]
            o_both = _dot(pe, vp) / l
            o = jnp.where(halves[0], o_both[:GRID_W], o_both[GRID_W:])
            o2 = o * o
            ss = [jnp.sum(jnp.where(h, o2, 0.0), axis=-1, keepdims=True) for h in halves]
            inv = jnp.where(halves[0], lax.rsqrt(ss[0] * (1.0 / NA_DH) + EPS),
                            lax.rsqrt(ss[1] * (1.0 / NA_DH) + EPS))
            o_ref[0, pl.ds(qrow, GRID_W), col[p]] = (o * inv * g_ref[...]).astype(BF16)
        return carry

    lax.fori_loop(0, NA_ROWS_PER_STEP, row_body, 0, unroll=True)


def _na_attn(proj, bias, g_out):
    b, s, _ = proj.shape
    rows = s // GRID_W
    nb = rows // NA_ROWS_PER_STEP
    qcol, kcol, vcol = SEC_NQ, SEC_NK, SEC_NV
    blk = (1, NA_BLOCK, HALF_WIDTH)

    def prev(bi, i, c):
        return (bi, jnp.maximum(i - 1, 0), c)

    def nxt(bi, i, c):
        return (bi, jnp.minimum(i + 1, nb - 1), c)

    return pl.pallas_call(
        functools.partial(_na_attn_kernel, rows=rows),
        grid=(b, nb),
        in_specs=[
            pl.BlockSpec(blk, lambda bi, i: (bi, i, qcol)),
            pl.BlockSpec(blk, lambda bi, i: prev(bi, i, kcol)),
            pl.BlockSpec(blk, lambda bi, i: (bi, i, kcol)),
            pl.BlockSpec(blk, lambda bi, i: nxt(bi, i, kcol)),
            pl.BlockSpec(blk, lambda bi, i: prev(bi, i, vcol)),
            pl.BlockSpec(blk, lambda bi, i: (bi, i, vcol)),
            pl.BlockSpec(blk, lambda bi, i: nxt(bi, i, vcol)),
            pl.BlockSpec((NA_HEADS // 2, NA_KH, 2 * GRID_W, NA_KH * GRID_W), lambda bi, i: (0, 0, 0, 0)),
            pl.BlockSpec((1, LANES), lambda bi, i: (0, 0)),
        ],
        out_specs=pl.BlockSpec(blk, lambda bi, i: (bi, i, 0)),
        out_shape=jax.ShapeDtypeStruct((b, s, HALF_WIDTH), BF16),
        scratch_shapes=[pltpu.VMEM((3 * NA_BLOCK, HALF_WIDTH), BF16),
                        pltpu.VMEM((3 * NA_BLOCK, HALF_WIDTH), BF16)],
        compiler_params=_cparams(("parallel", "arbitrary")),
        name="na_attn",
    )(proj, proj, proj, proj, proj, proj, proj, bias, g_out)


def _na_bias_table(rpb):
    qc = np.arange(GRID_W)
    kc = np.arange(GRID_W)
    win = np.clip(qc - NA_KW // 2, 0, GRID_W - NA_KW)
    valid = (kc[None, :] >= win[:, None]) & (kc[None, :] < win[:, None] + NA_KW)
    padded = jnp.pad(rpb.astype(F32) * math.log2(math.e), ((0, 0), (0, 0), (GRID_W, GRID_W)))
    by_col = jnp.stack([padded[:, :, GRID_W + NA_KW - 1 - c:2 * GRID_W + NA_KW - 1 - c] for c in range(GRID_W)],
                       axis=2)
    by_col = jnp.where(jnp.asarray(valid)[None, None], by_col, MASK_VALUE)
    t = jnp.stack([by_col[:, NA_KH - 1 - r:2 * NA_KH - 1 - r] for r in range(NA_KH)], axis=1)
    t = t.reshape(NA_HEADS // 2, 2, NA_KH, NA_KH, GRID_W, GRID_W)
    t = t.transpose(0, 2, 1, 4, 3, 5)
    return t.reshape(NA_HEADS // 2, NA_KH, 2 * GRID_W, NA_KH * GRID_W)


def _out_proj_kernel(x_ref, a_ref, b_ref, w_ref, g_ref, xo_ref, h_ref):
    x = x_ref[...] + _dot(a_ref[...], w_ref[:HALF_WIDTH, :]) + _dot(b_ref[...], w_ref[HALF_WIDTH:, :])
    xo_ref[...] = x
    h_ref[...] = _rms(x, g_ref[...]).astype(BF16)


def _out_proj_router_kernel(x_ref, a_ref, b_ref, w_ref, g_ref, wr_ref, xo_ref, h_ref, c_ref):
    x = x_ref[...] + _dot(a_ref[...], w_ref[:HALF_WIDTH, :]) + _dot(b_ref[...], w_ref[HALF_WIDTH:, :])
    xo_ref[...] = x
    h = _rms(x, g_ref[...])
    h_hi = h.astype(BF16)
    h_ref[...] = h_hi
    h_lo = (h - h_hi.astype(F32)).astype(BF16)
    wr = wr_ref[...]
    both = _dot(h_hi, wr)
    lane = lax.broadcasted_iota(jnp.int32, both.shape, 1).astype(F32)
    lo_part = pltpu.roll(both, LANES - N_EXPERTS, 1)
    logits = both + lo_part + _dot(h_lo, wr)
    logits = jnp.where(lane < N_EXPERTS, logits, -jnp.inf)
    m1 = jnp.max(logits, axis=-1, keepdims=True)
    i1 = jnp.min(jnp.where(logits == m1, lane, float(LANES)), axis=-1, keepdims=True)
    rest = jnp.where(lane == i1, -jnp.inf, logits)
    m2 = jnp.max(rest, axis=-1, keepdims=True)
    i2 = jnp.min(jnp.where(rest == m2, lane, float(LANES)), axis=-1, keepdims=True)
    e2 = jnp.exp(m2 - m1)
    g1 = 1.0 / (1.0 + e2)
    g2 = e2 * g1
    c_ref[...] = jnp.where(lane == i1, g1, jnp.where(lane == i2, g2, 0.0))


def _out_proj(x, a, b, w, g, wr=None, tm=512):
    t = x.shape[0]
    tm = min(tm, t)
    row = lambda i: (i, 0)
    const = lambda i: (0, 0)
    in_specs = [
        pl.BlockSpec((tm, D_MODEL), row),
        pl.BlockSpec((tm, HALF_WIDTH), row),
        pl.BlockSpec((tm, HALF_WIDTH), row),
        pl.BlockSpec((D_MODEL, D_MODEL), const),
        pl.BlockSpec((1, D_MODEL), const),
    ]
    out_specs = [pl.BlockSpec((tm, D_MODEL), row), pl.BlockSpec((tm, D_MODEL), row)]
    out_shape = [jax.ShapeDtypeStruct((t, D_MODEL), F32), jax.ShapeDtypeStruct((t, D_MODEL), BF16)]
    args = [x, a, b, w, g]
    kern = _out_proj_kernel
    if wr is not None:
        in_specs.append(pl.BlockSpec((D_MODEL, LANES), const))
        out_specs.append(pl.BlockSpec((tm, LANES), row))
        out_shape.append(jax.ShapeDtypeStruct((t, LANES), F32))
        args.append(wr)
        kern = _out_proj_router_kernel
    return pl.pallas_call(
        kern,
        grid=(t // tm,),
        in_specs=in_specs,
        out_specs=out_specs,
        out_shape=out_shape,
        compiler_params=_cparams(("parallel",)),
        name="out_proj",
    )(*args)


def _gate_up(h, wgu):
    r = _dot(h, wgu)
    g, u = r[:, :FF_BLOCK], r[:, FF_BLOCK:]
    return (g * jax.nn.sigmoid(g) * u).astype(BF16)


def _swiglu_kernel(x_ref, h_ref, wgu_ref, wd_ref, o_ref):
    @pl.when(pl.program_id(1) == 0)
    def _():
        o_ref[...] = x_ref[...]

    o_ref[...] += _dot(_gate_up(h_ref[...], wgu_ref[...]), wd_ref[...])


def _swiglu(x, h, wgu, wd, tm=1024):
    t = x.shape[0]
    tf = FF_BLOCK
    tm = min(tm, t)
    return pl.pallas_call(
        _swiglu_kernel,
        grid=(t // tm, D_FF // tf),
        in_specs=[
            pl.BlockSpec((tm, D_MODEL), lambda i, f: (i, 0)),
            pl.BlockSpec((tm, D_MODEL), lambda i, f: (i, 0)),
            pl.BlockSpec((D_MODEL, 2 * tf), lambda i, f: (0, f)),
            pl.BlockSpec((tf, D_MODEL), lambda i, f: (f, 0)),
        ],
        out_specs=pl.BlockSpec((tm, D_MODEL), lambda i, f: (i, 0)),
        out_shape=jax.ShapeDtypeStruct((t, D_MODEL), F32),
        compiler_params=_cparams(("parallel", "arbitrary")),
        name="swiglu",
    )(x, h, wgu, wd)


MOE_TILE = 1024
MOE_CHUNK = 256
MOE_TAIL = 128
TRI_BLOCK = 512


def _route_kernel(c_ref, utri_ref, pos_ref, gate_ref):
    tm = c_ref.shape[0]
    carry = jnp.zeros((LANES, 1), F32)
    for b in range(tm // TRI_BLOCK):
        rows = slice(b * TRI_BLOCK, (b + 1) * TRI_BLOCK)
        gate_t = c_ref[rows, :].T
        sel_t = jnp.where(gate_t > 0.0, 1.0, 0.0)
        pos_t = _dot(sel_t.astype(BF16), utri_ref[...]) + carry
        pos_ref[0, :, rows] = jnp.where(gate_t > 0.0, pos_t, -1.0)
        gate_ref[0, :, rows] = gate_t
        carry = carry + jnp.sum(sel_t, axis=1, keepdims=True)


def _route(c, tm):
    t = c.shape[0]
    idx = np.arange(TRI_BLOCK)
    utri = jnp.asarray(idx[:, None] < idx[None, :], dtype=BF16)
    expert_major = jax.ShapeDtypeStruct((t // tm, LANES, tm), F32)
    return pl.pallas_call(
        _route_kernel,
        grid=(t // tm,),
        in_specs=[
            pl.BlockSpec((tm, LANES), lambda i: (i, 0)),
            pl.BlockSpec((TRI_BLOCK, TRI_BLOCK), lambda i: (0, 0)),
        ],
        out_specs=[pl.BlockSpec((1, LANES, tm), lambda i: (i, 0, 0))] * 2,
        out_shape=[expert_major, expert_major],
        compiler_params=_cparams(("parallel",)),
        name="moe_route",
    )(c, utri)


def _moe_sparse_kernel(loads_ref, x_ref, h_ref, posmt_ref, gatet_ref, wgu_ref, wd_ref, o_ref, hc_ref, y_ref):
    e = pl.program_id(1)
    f = pl.program_id(2)
    last_f = pl.num_programs(2) - 1
    tm = x_ref.shape[0]

    @pl.when((e == 0) & (f == 0))
    def _():
        o_ref[...] = x_ref[...]

    pos_row = posmt_ref[0, pl.ds(e, 1), :]
    load = loads_ref[pl.program_id(0), e]
    rem = load % MOE_CHUNK
    nfull = load // MOE_CHUNK + (rem > MOE_TAIL).astype(jnp.int32)
    has_tail = (rem > 0) & (rem <= MOE_TAIL)
    tail_off = nfull * MOE_CHUNK

    def rows_of(off, size):
        return pl.ds(pl.multiple_of(off, MOE_TAIL), size)

    def for_each_chunk(fn):
        def full(k, carry):
            fn(k * MOE_CHUNK, MOE_CHUNK)
            return carry

        lax.fori_loop(0, nfull, full, 0)

        @pl.when(has_tail)
        def _():
            fn(tail_off, MOE_TAIL)

    def hits(off, size):
        slot = (lax.broadcasted_iota(jnp.int32, (size, tm), 0) + off).astype(F32)
        return pos_row == slot

    def expert(off, size):
        return _dot(_gate_up(hc_ref[rows_of(off, size), :], wgu_ref[0]), wd_ref[0])

    def first(off, size):
        onehot = jnp.where(hits(off, size), 1.0, 0.0).astype(BF16)
        hc_ref[rows_of(off, size), :] = _dot(onehot, h_ref[...]).astype(BF16)
        y_ref[rows_of(off, size), :] = expert(off, size)

    def middle(off, size):
        y_ref[rows_of(off, size), :] += expert(off, size)

    def last(off, size):
        hit = hits(off, size)
        gate = jnp.sum(jnp.where(hit, gatet_ref[0, pl.ds(e, 1), :], 0.0), axis=-1, keepdims=True)
        ys = ((y_ref[rows_of(off, size), :] + expert(off, size)) * gate).astype(BF16)
        onehot = jnp.where(hit, 1.0, 0.0).astype(BF16)
        o_ref[...] += lax.dot_general(onehot, ys, (((0,), (0,)), ((), ())), preferred_element_type=F32)

    @pl.when(f == 0)
    def _():
        for_each_chunk(first)

    if D_FF // FF_BLOCK > 2:
        @pl.when((f > 0) & (f < last_f))
        def _():
            for_each_chunk(middle)

    @pl.when(f == last_f)
    def _():
        for_each_chunk(last)


def _moe_sparse(x, h, c, wgu, wd, tm=MOE_TILE):
    t = x.shape[0]
    tf = FF_BLOCK
    tm = min(tm, t)
    posmt, gatet = _route(c, tm)
    loads = (jnp.max(posmt[:, :N_EXPERTS, :], axis=-1) + 1.0).astype(jnp.int32)
    return pl.pallas_call(
        _moe_sparse_kernel,
        grid_spec=pltpu.PrefetchScalarGridSpec(
            num_scalar_prefetch=1,
            grid=(t // tm, N_EXPERTS, D_FF // tf),
            in_specs=[
                pl.BlockSpec((tm, D_MODEL), lambda i, e, f, loads: (i, 0)),
                pl.BlockSpec((tm, D_MODEL), lambda i, e, f, loads: (i, 0)),
                pl.BlockSpec((1, LANES, tm), lambda i, e, f, loads: (i, 0, 0)),
                pl.BlockSpec((1, LANES, tm), lambda i, e, f, loads: (i, 0, 0)),
                pl.BlockSpec((1, D_MODEL, 2 * tf), lambda i, e, f, loads: (e, 0, f)),
                pl.BlockSpec((1, tf, D_MODEL), lambda i, e, f, loads: (e, f, 0)),
            ],
            out_specs=pl.BlockSpec((tm, D_MODEL), lambda i, e, f, loads: (i, 0)),
            scratch_shapes=[pltpu.VMEM((tm, D_MODEL), BF16), pltpu.VMEM((tm, D_MODEL), F32)],
        ),
        out_shape=jax.ShapeDtypeStruct((t, D_MODEL), F32),
        compiler_params=_cparams(("parallel", "arbitrary", "arbitrary")),
        name="moe_sparse",
    )(loads, x, h, posmt, gatet, wgu, wd)


def _rope_tables(seq):
    inv = 1.0 / (ROPE_THETA ** (jnp.arange(0, DIFF_DH, 2, dtype=F32) / DIFF_DH))
    ang = jnp.arange(seq, dtype=F32)[:, None] * inv[None, :]
    ang = jnp.concatenate([ang, ang], axis=-1)
    cos, sin = jnp.cos(ang), jnp.sin(ang)
    sign = jnp.where(jnp.arange(DIFF_DH) < DIFF_DH // 2, -1.0, 1.0).astype(F32)
    return jnp.tile(cos, (1, 2)), jnp.tile(sin * sign, (1, 2))


def _block_diag_ones():
    idx = np.arange(MXU_TILE) // DIFF_DH
    return jnp.asarray(idx[:, None] == idx[None, :], dtype=BF16)


def _gate_up_blocks(w_gate, w_up):
    pieces = []
    for f in range(D_FF // FF_BLOCK):
        cols = slice(f * FF_BLOCK, (f + 1) * FF_BLOCK)
        pieces += [w_gate[..., cols], w_up[..., cols]]
    return jnp.concatenate(pieces, axis=-1).astype(BF16)


def _prepare_layer(l, p):
    lam_init = 0.8 - 0.6 * math.exp(-0.3 * l)
    lq = p["lambda_q"][l].astype(F32)
    lk = p["lambda_k"][l].astype(F32)
    lam = jnp.exp(jnp.sum(lq[0] * lk[0])) - jnp.exp(jnp.sum(lq[1] * lk[1])) + lam_init
    scale = DIFF_DH ** -0.5 * math.log2(math.e)
    tile8 = lambda g: jnp.tile(g.astype(F32), HALF_WIDTH // g.shape[0])
    zeros = jnp.zeros((HALF_WIDTH,), F32)
    dq_gain = p["diff_q_norm"][l].astype(F32) * scale
    dk_gain = p["diff_k_norm"][l].astype(F32)
    gains = jnp.stack([
        tile8(dq_gain), tile8(dk_gain), zeros,
        tile8(p["na_q_norm"][l]) * scale, tile8(p["na_k_norm"][l]), zeros, zeros, zeros])
    score_bound = SCORE_BOUND_MARGIN * DIFF_DH * jnp.max(jnp.abs(dq_gain)) * jnp.max(jnp.abs(dk_gain))
    lay = dict(
        scal=jnp.stack([lam, score_bound]).astype(F32),
        out_scale=1.0 - lam_init,
        attn_norm=p["attn_norm"][l].reshape(1, D_MODEL).astype(F32),
        w_in=p["w_in"][l].astype(BF16),
        gains=gains,
        g_sub=p["diff_sub_norm"][l].reshape(1, LANES).astype(F32),
        na_bias=_na_bias_table(p["na_rpb"][l]),
        g_na_out=jnp.tile(p["na_out_norm"][l].astype(F32), 2).reshape(1, LANES),
        w_out=p["w_out"][l].astype(BF16),
        ffn_norm=p["ffn_norm"][l].reshape(1, D_MODEL).astype(F32),
    )
    i = l // 2
    if l % 2 == 0:
        lay.update(wgu=_gate_up_blocks(p["dense_w_gate"][i], p["dense_w_up"][i]),
                   wd=p["dense_w_down"][i].astype(BF16))
    else:
        wr = p["moe_router"][i].astype(F32)
        wr_hi = wr.astype(BF16)
        wr_lo = (wr - wr_hi.astype(F32)).astype(BF16)
        wr2 = jnp.zeros((D_MODEL, LANES), BF16)
        wr2 = wr2.at[:, :N_EXPERTS].set(wr_hi).at[:, N_EXPERTS:2 * N_EXPERTS].set(wr_lo)
        lay.update(wr=wr2, wgu=_gate_up_blocks(p["moe_w_gate"][i], p["moe_w_up"][i]),
                   wd=p["moe_w_down"][i].astype(BF16))
    return lay


def _run_trunk(x, layers, bd):
    b, s, _ = x.shape
    t = b * s
    cos, sin = _rope_tables(s)
    x = x.reshape(t, D_MODEL)
    for l, lay in enumerate(layers):
        proj, vt = _in_proj(x, lay["attn_norm"], lay["w_in"], lay["gains"], cos, sin, bd, s)
        proj3 = proj.reshape(b, s, IN_WIDTH)
        a_out = _diff_attn(proj3, vt, lay["scal"], lay["g_sub"], lay["out_scale"])
        b_out = _na_attn(proj3, lay["na_bias"], lay["g_na_out"])
        a_out = a_out.reshape(t, HALF_WIDTH)
        b_out = b_out.reshape(t, HALF_WIDTH)
        if l % 2 == 0:
            x, h = _out_proj(x, a_out, b_out, lay["w_out"], lay["ffn_norm"])
            x = _swiglu(x, h, lay["wgu"], lay["wd"])
        else:
            x, h, c = _out_proj(x, a_out, b_out, lay["w_out"], lay["ffn_norm"], lay["wr"])
            x = _moe_sparse(x, h, c, lay["wgu"], lay["wd"])
    return x.reshape(b, s, D_MODEL)


def kernel(x_prompt, x_sample, attn_norm, w_in, w_out, diff_q_norm, diff_k_norm, lambda_q, lambda_k, diff_sub_norm, na_q_norm, na_k_norm, na_rpb, na_out_norm, ffn_norm, dense_w_gate, dense_w_up, dense_w_down, moe_router, moe_w_gate, moe_w_up, moe_w_down):
    p = dict(attn_norm=attn_norm, w_in=w_in, w_out=w_out, diff_q_norm=diff_q_norm, diff_k_norm=diff_k_norm,
             lambda_q=lambda_q, lambda_k=lambda_k, diff_sub_norm=diff_sub_norm, na_q_norm=na_q_norm,
             na_k_norm=na_k_norm, na_rpb=na_rpb, na_out_norm=na_out_norm, ffn_norm=ffn_norm,
             dense_w_gate=dense_w_gate, dense_w_up=dense_w_up, dense_w_down=dense_w_down,
             moe_router=moe_router, moe_w_gate=moe_w_gate, moe_w_up=moe_w_up, moe_w_down=moe_w_down)
    layers = [_prepare_layer(l, p) for l in range(DEPTH)]
    bd = _block_diag_ones()
    return (_run_trunk(x_prompt, layers, bd), _run_trunk(x_sample, layers, bd))
```

```python
import functools
import math

import jax
import jax.numpy as jnp
import numpy as np
from jax import lax
from jax.experimental import pallas as pl
from jax.experimental.pallas import tpu as pltpu

D_MODEL = 1024
DEPTH = 4
GRID_W = 64
DIFF_HEADS = 4
DIFF_DH = 64
NA_HEADS = 8
NA_DH = 64
NA_KH = 8
NA_KW = 16
HALF_WIDTH = 512
SEC_DQ, SEC_DK, SEC_DV, SEC_NQ, SEC_NK, SEC_NV = range(6)
IN_WIDTH = 6 * HALF_WIDTH
D_FF = 2816
FF_BLOCK = 1408
N_EXPERTS = 8
ROPE_THETA = 10000.0
EPS = 1e-6
MASK_VALUE = -1e30
SCORE_BOUND_MARGIN = 1.02
MAX_SAFE_SCORE_BOUND = 60.0

LANES = 128
MXU_TILE = 256
VMEM_LIMIT = 56 * 1024 * 1024

BF16 = jnp.bfloat16
F32 = jnp.float32


def _cparams(sem):
    return pltpu.CompilerParams(dimension_semantics=sem, vmem_limit_bytes=VMEM_LIMIT)


def _dot(a, b):
    return jnp.dot(a, b, preferred_element_type=F32)


def _dot_nt(a, b):
    return lax.dot_general(a, b, (((1,), (1,)), ((), ())), preferred_element_type=F32)


def _rms(x, g):
    return x * lax.rsqrt(jnp.mean(x * x, axis=-1, keepdims=True) + EPS) * g


def _group_norm64(y, gain, bd_ref):
    parts = []
    for c in range(HALF_WIDTH // MXU_TILE):
        yc = y[:, c * MXU_TILE:(c + 1) * MXU_TILE]
        ss = _dot((yc * yc).astype(BF16), bd_ref[...])
        parts.append(yc * lax.rsqrt(ss * (1.0 / 64) + EPS))
    return jnp.concatenate(parts, axis=1) * gain


def _rope64(y, cos, sin_signed):
    n = y.shape[1]
    lane = lax.broadcasted_iota(jnp.int32, y.shape, 1)
    first_half = (lane % 64) < 32
    rot = jnp.where(first_half, pltpu.roll(y, n - 32, 1), pltpu.roll(y, 32, 1))
    return y * cos + rot * sin_signed


def _in_proj_kernel(x_ref, g_ref, w_ref, gains_ref, cos_ref, sin_ref, bd_ref, o_ref, vt_ref):
    x = x_ref[...]
    h = _rms(x, g_ref[...]).astype(BF16)
    cos = jnp.concatenate([cos_ref[...]] * (HALF_WIDTH // LANES), axis=1)
    sin = jnp.concatenate([sin_ref[...]] * (HALF_WIDTH // LANES), axis=1)
    for sec in range(6):
        y = _dot(h, w_ref[:, sec * HALF_WIDTH:(sec + 1) * HALF_WIDTH])
        if sec in (SEC_DQ, SEC_DK, SEC_NQ, SEC_NK):
            y = _group_norm64(y, gains_ref[sec:sec + 1, :], bd_ref)
        if sec in (SEC_DQ, SEC_DK):
            y = _rope64(y, cos, sin)
        o_ref[:, sec * HALF_WIDTH:(sec + 1) * HALF_WIDTH] = y.astype(BF16)
        if sec == SEC_DV:
            vt_ref[0] = y.T.astype(BF16)


def _in_proj(x, g, w, gains, cos, sin, bd, seq, tm=512):
    t = x.shape[0]
    tm = min(tm, seq)
    nseq = seq // tm
    return pl.pallas_call(
        _in_proj_kernel,
        grid=(t // tm,),
        in_specs=[
            pl.BlockSpec((tm, D_MODEL), lambda i: (i, 0)),
            pl.BlockSpec((1, D_MODEL), lambda i: (0, 0)),
            pl.BlockSpec((D_MODEL, IN_WIDTH), lambda i: (0, 0)),
            pl.BlockSpec((8, HALF_WIDTH), lambda i: (0, 0)),
            pl.BlockSpec((tm, LANES), lambda i: (i % nseq, 0)),
            pl.BlockSpec((tm, LANES), lambda i: (i % nseq, 0)),
            pl.BlockSpec((MXU_TILE, MXU_TILE), lambda i: (0, 0)),
        ],
        out_specs=[pl.BlockSpec((tm, IN_WIDTH), lambda i: (i, 0)),
                   pl.BlockSpec((1, HALF_WIDTH, tm), lambda i: (i // nseq, 0, i % nseq))],
        out_shape=[jax.ShapeDtypeStruct((t, IN_WIDTH), BF16),
                   jax.ShapeDtypeStruct((t // seq, HALF_WIDTH, seq), BF16)],
        compiler_params=_cparams(("parallel",)),
        name="in_proj",
    )(x, g, w, gains, cos, sin, bd)


def _split_maps(q):
    lane = lax.broadcasted_iota(jnp.int32, q.shape, 1)
    zero = jnp.zeros_like(q)
    return jnp.where(lane < DIFF_DH, q, zero), jnp.where(lane >= DIFF_DH, q, zero)


def _diff_attn_online_kernel(scal_ref, q_ref, k_ref, v_ref, g_ref, o_ref, acc0_ref, acc1_ref, l0_ref, l1_ref, *,
                             tk, out_scale):
    qm = _split_maps(q_ref[0])
    tq = q_ref.shape[1]
    accs = (acc0_ref, acc1_ref)
    ls = (l0_ref, l1_ref)
    for r in accs + ls:
        r[...] = jnp.zeros_like(r)

    def lane_fold(p):
        out = p[:, :LANES]
        for c in range(1, tk // LANES):
            out = out + p[:, c * LANES:(c + 1) * LANES]
        return out

    def body(j, ms):
        start = pl.multiple_of(j * tk, tk)
        k = k_ref[0, pl.ds(start, tk), :]
        v = v_ref[0, pl.ds(start, tk), :]
        new_ms = []
        for a in range(2):
            s = _dot_nt(qm[a], k)
            m_new = jnp.maximum(ms[a], jnp.max(s, axis=-1, keepdims=True))
            alpha = jnp.exp2(ms[a] - m_new)
            p = jnp.exp2(s - m_new)
            ls[a][...] = alpha * ls[a][...] + lane_fold(p)
            accs[a][...] = alpha * accs[a][...] + _dot(p.astype(BF16), v)
            new_ms.append(m_new)
        return tuple(new_ms)

    neg = jnp.full((tq, 1), -jnp.inf, F32)
    lax.fori_loop(0, k_ref.shape[1] // tk, body, (neg, neg))
    lam = scal_ref[0]
    l0 = jnp.sum(l0_ref[...], axis=-1, keepdims=True)
    l1 = jnp.sum(l1_ref[...], axis=-1, keepdims=True)
    o = acc0_ref[...] / l0 - lam * (acc1_ref[...] / l1)
    o_ref[0] = (_rms(o, g_ref[...]) * out_scale).astype(BF16)


def _diff_attn_t_kernel(scal_ref, q_ref, k_ref, vt_ref, g_ref, o_ref, acc0_ref, acc1_ref, l0_ref, l1_ref, *,
                        tk, out_scale):
    qm = _split_maps(q_ref[0])
    accs = (acc0_ref, acc1_ref)
    ls = (l0_ref, l1_ref)
    for r in accs + ls:
        r[...] = jnp.zeros_like(r)
    nk = k_ref.shape[1] // tk
    tq = q_ref.shape[1]
    bound = scal_ref[1]

    def body(j, carry):
        start = pl.multiple_of(j * tk, tk)
        k = k_ref[0, pl.ds(start, tk), :]
        vt = vt_ref[0, :, pl.ds(start, tk)]
        for a in range(2):
            p = jnp.exp2(_dot_nt(k, qm[a]) - bound)
            ls[a][...] += jnp.sum(p.reshape(tk // 8, 8, tq), axis=0)
            accs[a][...] += _dot(vt, p.astype(BF16))
        return carry

    lax.fori_loop(0, nk, body, 0, unroll=2)
    lam = scal_ref[0]
    l0 = jnp.sum(l0_ref[...], axis=0, keepdims=True)
    l1 = jnp.sum(l1_ref[...], axis=0, keepdims=True)
    ot = acc0_ref[...] / l0 - lam * (acc1_ref[...] / l1)
    ot = ot * lax.rsqrt(jnp.mean(ot * ot, axis=0, keepdims=True) + EPS)
    o_ref[0] = (ot.T * g_ref[...] * out_scale).astype(BF16)


DIFF_TQ = 1024
DIFF_TK_BOUNDED = 2048
DIFF_TK_ONLINE = 1024


def _diff_attn_call(proj, vt, scal, g_sub, out_scale, bounded):
    b, s, _ = proj.shape
    tq = min(DIFF_TQ, s)
    q_spec = pl.BlockSpec((1, tq, LANES), lambda bi, h, i: (bi, i, h))
    k_spec = pl.BlockSpec((1, s, LANES), lambda bi, h, i: (bi, 0, DIFF_HEADS + h))
    if bounded:
        kern = functools.partial(_diff_attn_t_kernel, tk=min(DIFF_TK_BOUNDED, s), out_scale=out_scale)
        v, v_spec = vt, pl.BlockSpec((1, LANES, s), lambda bi, h, i: (bi, h, 0))
        scratch = [pltpu.VMEM((LANES, tq), F32)] * 2 + [pltpu.VMEM((8, tq), F32)] * 2
    else:
        kern = functools.partial(_diff_attn_online_kernel, tk=min(DIFF_TK_ONLINE, s), out_scale=out_scale)
        v, v_spec = proj, pl.BlockSpec((1, s, LANES), lambda bi, h, i: (bi, 0, 2 * DIFF_HEADS + h))
        scratch = [pltpu.VMEM((tq, LANES), F32)] * 4
    return pl.pallas_call(
        kern,
        grid=(b, DIFF_HEADS, s // tq),
        in_specs=[pl.BlockSpec(memory_space=pltpu.SMEM), q_spec, k_spec, v_spec,
                  pl.BlockSpec((1, LANES), lambda bi, h, i: (0, 0))],
        out_specs=pl.BlockSpec((1, tq, LANES), lambda bi, h, i: (bi, i, h)),
        out_shape=jax.ShapeDtypeStruct((b, s, HALF_WIDTH), BF16),
        scratch_shapes=scratch,
        compiler_params=_cparams(("parallel", "parallel", "arbitrary")),
        name="diff_attn_t" if bounded else "diff_attn_online",
    )(scal, proj, proj, v, g_sub)


def _diff_attn(proj, vt, scal, g_sub, out_scale):
    return lax.cond(
        scal[1] <= MAX_SAFE_SCORE_BOUND,
        lambda: _diff_attn_call(proj, vt, scal, g_sub, out_scale, True),
        lambda: _diff_attn_call(proj, vt, scal, g_sub, out_scale, False))


NA_ROWS_PER_STEP = 8
NA_BLOCK = NA_ROWS_PER_STEP * GRID_W


def _na_attn_kernel(q_ref, kp_ref, kc_ref, kn_ref, vp_ref, vc_ref, vn_ref, bias_ref, g_ref, o_ref,
                    kbuf, vbuf, *, rows):
    i = pl.program_id(1)
    for n, (kr, vr) in enumerate(((kp_ref, vp_ref), (kc_ref, vc_ref), (kn_ref, vn_ref))):
        kbuf[n * NA_BLOCK:(n + 1) * NA_BLOCK, :] = kr[0]
        vbuf[n * NA_BLOCK:(n + 1) * NA_BLOCK, :] = vr[0]
    lane = lax.broadcasted_iota(jnp.int32, (GRID_W, LANES), 1)
    halves = (lane < NA_DH, lane >= NA_DH)

    def row_body(qr, carry):
        r = i * NA_ROWS_PER_STEP + qr
        r0 = jnp.clip(r - NA_KH // 2, 0, rows - NA_KH)
        start = pl.multiple_of((r0 - (i - 1) * NA_ROWS_PER_STEP) * GRID_W, GRID_W)
        rp = r - r0
        qrow = pl.multiple_of(qr * GRID_W, GRID_W)
        npairs = NA_HEADS // 2
        col = [slice(p * LANES, (p + 1) * LANES) for p in range(npairs)]
        scores = []
        for p in range(npairs):
            qp = q_ref[0, pl.ds(qrow, GRID_W), col[p]]
            kp = kbuf[pl.ds(start, NA_KH * GRID_W), col[p]]
            q2 = jnp.concatenate([jnp.where(h, qp, jnp.zeros_like(qp)) for h in halves], axis=0)
            scores.append(_dot_nt(q2, kp) + bias_ref[p, rp])
        probs = []
        for s in scores:
            m = jnp.max(s, axis=-1, keepdims=True)
            pe = jnp.exp2(s - m)
            probs.append((pe.astype(BF16), jnp.sum(pe, axis=-1, keepdims=True)))
        for p in range(npairs):
            vp = vbuf[pl.ds(start, NA_KH * GRID_W), col[p]]
            pe, l = probs[p]
            o_both = _dot(pe, vp) / l
            o = jnp.where(halves[0], o_both[:GRID_W], o_both[GRID_W:])
            o2 = o * o
            ss = [jnp.sum(jnp.where(h, o2, 0.0), axis=-1, keepdims=True) for h in halves]
            inv = jnp.where(halves[0], lax.rsqrt(ss[0] * (1.0 / NA_DH) + EPS),
                            lax.rsqrt(ss[1] * (1.0 / NA_DH) + EPS))
            o_ref[0, pl.ds(qrow, GRID_W), col[p]] = (o * inv * g_ref[...]).astype(BF16)
        return carry

    lax.fori_loop(0, NA_ROWS_PER_STEP, row_body, 0, unroll=True)


def _na_attn(proj, bias, g_out):
    b, s, _ = proj.shape
    rows = s // GRID_W
    nb = rows // NA_ROWS_PER_STEP
    qcol, kcol, vcol = SEC_NQ, SEC_NK, SEC_NV
    blk = (1, NA_BLOCK, HALF_WIDTH)

    def prev(bi, i, c):
        return (bi, jnp.maximum(i - 1, 0), c)

    def nxt(bi, i, c):
        return (bi, jnp.minimum(i + 1, nb - 1), c)

    return pl.pallas_call(
        functools.partial(_na_attn_kernel, rows=rows),
        grid=(b, nb),
        in_specs=[
            pl.BlockSpec(blk, lambda bi, i: (bi, i, qcol)),
            pl.BlockSpec(blk, lambda bi, i: prev(bi, i, kcol)),
            pl.BlockSpec(blk, lambda bi, i: (bi, i, kcol)),
            pl.BlockSpec(blk, lambda bi, i: nxt(bi, i, kcol)),
            pl.BlockSpec(blk, lambda bi, i: prev(bi, i, vcol)),
            pl.BlockSpec(blk, lambda bi, i: (bi, i, vcol)),
            pl.BlockSpec(blk, lambda bi, i: nxt(bi, i, vcol)),
            pl.BlockSpec((NA_HEADS // 2, NA_KH, 2 * GRID_W, NA_KH * GRID_W), lambda bi, i: (0, 0, 0, 0)),
            pl.BlockSpec((1, LANES), lambda bi, i: (0, 0)),
        ],
        out_specs=pl.BlockSpec(blk, lambda bi, i: (bi, i, 0)),
        out_shape=jax.ShapeDtypeStruct((b, s, HALF_WIDTH), BF16),
        scratch_shapes=[pltpu.VMEM((3 * NA_BLOCK, HALF_WIDTH), BF16),
                        pltpu.VMEM((3 * NA_BLOCK, HALF_WIDTH), BF16)],
        compiler_params=_cparams(("parallel", "arbitrary")),
        name="na_attn",
    )(proj, proj, proj, proj, proj, proj, proj, bias, g_out)


def _na_bias_table(rpb):
    qc = np.arange(GRID_W)
    kc = np.arange(GRID_W)
    win = np.clip(qc - NA_KW // 2, 0, GRID_W - NA_KW)
    valid = (kc[None, :] >= win[:, None]) & (kc[None, :] < win[:, None] + NA_KW)
    padded = jnp.pad(rpb.astype(F32) * math.log2(math.e), ((0, 0), (0, 0), (GRID_W, GRID_W)))
    by_col = jnp.stack([padded[:, :, GRID_W + NA_KW - 1 - c:2 * GRID_W + NA_KW - 1 - c] for c in range(GRID_W)],
                       axis=2)
    by_col = jnp.where(jnp.asarray(valid)[None, None], by_col, MASK_VALUE)
    t = jnp.stack([by_col[:, NA_KH - 1 - r:2 * NA_KH - 1 - r] for r in range(NA_KH)], axis=1)
    t = t.reshape(NA_HEADS // 2, 2, NA_KH, NA_KH, GRID_W, GRID_W)
    t = t.transpose(0, 2, 1, 4, 3, 5)
    return t.reshape(NA_HEADS // 2, NA_KH, 2 * GRID_W, NA_KH * GRID_W)


def _out_proj_router_kernel(x_ref, a_ref, b_ref, w_ref, g_ref, wr_ref, xo_ref, h_ref, c_ref):
    x = x_ref[...] + _dot(a_ref[...], w_ref[:HALF_WIDTH, :]) + _dot(b_ref[...], w_ref[HALF_WIDTH:, :])
    xo_ref[...] = x
    h = _rms(x, g_ref[...])
    h_hi = h.astype(BF16)
    h_ref[...] = h_hi
    h_lo = (h - h_hi.astype(F32)).astype(BF16)
    wr = wr_ref[...]
    both = _dot(h_hi, wr)
    lane = lax.broadcasted_iota(jnp.int32, both.shape, 1).astype(F32)
    lo_part = pltpu.roll(both, LANES - N_EXPERTS, 1)
    logits = both + lo_part + _dot(h_lo, wr)
    logits = jnp.where(lane < N_EXPERTS, logits, -jnp.inf)
    m1 = jnp.max(logits, axis=-1, keepdims=True)
    i1 = jnp.min(jnp.where(logits == m1, lane, float(LANES)), axis=-1, keepdims=True)
    rest = jnp.where(lane == i1, -jnp.inf, logits)
    m2 = jnp.max(rest, axis=-1, keepdims=True)
    i2 = jnp.min(jnp.where(rest == m2, lane, float(LANES)), axis=-1, keepdims=True)
    e2 = jnp.exp(m2 - m1)
    g1 = 1.0 / (1.0 + e2)
    g2 = e2 * g1
    c_ref[...] = jnp.where(lane == i1, g1, jnp.where(lane == i2, g2, 0.0))


def _out_proj(x, a, b, w, g, wr, tm=512):
    t = x.shape[0]
    tm = min(tm, t)
    row = lambda i: (i, 0)
    const = lambda i: (0, 0)
    return pl.pallas_call(
        _out_proj_router_kernel,
        grid=(t // tm,),
        in_specs=[
            pl.BlockSpec((tm, D_MODEL), row),
            pl.BlockSpec((tm, HALF_WIDTH), row),
            pl.BlockSpec((tm, HALF_WIDTH), row),
            pl.BlockSpec((D_MODEL, D_MODEL), const),
            pl.BlockSpec((1, D_MODEL), const),
            pl.BlockSpec((D_MODEL, LANES), const),
        ],
        out_specs=[pl.BlockSpec((tm, D_MODEL), row), pl.BlockSpec((tm, D_MODEL), row),
                   pl.BlockSpec((tm, LANES), row)],
        out_shape=[jax.ShapeDtypeStruct((t, D_MODEL), F32), jax.ShapeDtypeStruct((t, D_MODEL), BF16),
                   jax.ShapeDtypeStruct((t, LANES), F32)],
        compiler_params=_cparams(("parallel",)),
        name="out_proj",
    )(x, a, b, w, g, wr)


def _gate_up(h, wgu):
    r = _dot(h, wgu)
    g, u = r[:, :FF_BLOCK], r[:, FF_BLOCK:]
    return (g * jax.nn.sigmoid(g) * u).astype(BF16)


def _swiglu_kernel(x_ref, a_ref, b_ref, wo_ref, g_ref, wgu_ref, wd_ref, o_ref, h_ref):
    @pl.when(pl.program_id(1) == 0)
    def _():
        x = x_ref[...] + _dot(a_ref[...], wo_ref[:HALF_WIDTH, :]) + _dot(b_ref[...], wo_ref[HALF_WIDTH:, :])
        o_ref[...] = x
        h_ref[...] = _rms(x, g_ref[...]).astype(BF16)

    o_ref[...] += _dot(_gate_up(h_ref[...], wgu_ref[...]), wd_ref[...])


def _swiglu(x, a, b, w_out, g, wgu, wd, tm=1024):
    t = x.shape[0]
    tf = FF_BLOCK
    tm = min(tm, t)
    return pl.pallas_call(
        _swiglu_kernel,
        grid=(t // tm, D_FF // tf),
        in_specs=[
            pl.BlockSpec((tm, D_MODEL), lambda i, f: (i, 0)),
            pl.BlockSpec((tm, HALF_WIDTH), lambda i, f: (i, 0)),
            pl.BlockSpec((tm, HALF_WIDTH), lambda i, f: (i, 0)),
            pl.BlockSpec((D_MODEL, D_MODEL), lambda i, f: (0, 0)),
            pl.BlockSpec((1, D_MODEL), lambda i, f: (0, 0)),
            pl.BlockSpec((D_MODEL, 2 * tf), lambda i, f: (0, f)),
            pl.BlockSpec((tf, D_MODEL), lambda i, f: (f, 0)),
        ],
        out_specs=pl.BlockSpec((tm, D_MODEL), lambda i, f: (i, 0)),
        out_shape=jax.ShapeDtypeStruct((t, D_MODEL), F32),
        scratch_shapes=[pltpu.VMEM((tm, D_MODEL), BF16)],
        compiler_params=_cparams(("parallel", "arbitrary")),
        name="swiglu",
    )(x, a, b, w_out, g, wgu, wd)


MOE_TILE = 1024
MOE_CHUNK = 256
MOE_TAIL = 128
TRI_BLOCK = 512


def _route_kernel(c_ref, utri_ref, pos_ref, gate_ref):
    tm = c_ref.shape[0]
    carry = jnp.zeros((LANES, 1), F32)
    for b in range(tm // TRI_BLOCK):
        rows = slice(b * TRI_BLOCK, (b + 1) * TRI_BLOCK)
        gate_t = c_ref[rows, :].T
        sel_t = jnp.where(gate_t > 0.0, 1.0, 0.0)
        pos_t = _dot(sel_t.astype(BF16), utri_ref[...]) + carry
        pos_ref[0, :, rows] = jnp.where(gate_t > 0.0, pos_t, -1.0)
        gate_ref[0, :, rows] = gate_t
        carry = carry + jnp.sum(sel_t, axis=1, keepdims=True)


def _route(c, tm):
    t = c.shape[0]
    idx = np.arange(TRI_BLOCK)
    utri = jnp.asarray(idx[:, None] < idx[None, :], dtype=BF16)
    expert_major = jax.ShapeDtypeStruct((t // tm, LANES, tm), F32)
    return pl.pallas_call(
        _route_kernel,
        grid=(t // tm,),
        in_specs=[
            pl.BlockSpec((tm, LANES), lambda i: (i, 0)),
            pl.BlockSpec((TRI_BLOCK, TRI_BLOCK), lambda i: (0, 0)),
        ],
        out_specs=[pl.BlockSpec((1, LANES, tm), lambda i: (i, 0, 0))] * 2,
        out_shape=[expert_major, expert_major],
        compiler_params=_cparams(("parallel",)),
        name="moe_route",
    )(c, utri)


def _moe_sparse_kernel(loads_ref, x_ref, h_ref, posmt_ref, gatet_ref, wgu_ref, wd_ref, o_ref, hc_ref, y_ref):
    e = pl.program_id(1)
    f = pl.program_id(2)
    last_f = pl.num_programs(2) - 1
    tm = x_ref.shape[0]

    @pl.when((e == 0) & (f == 0))
    def _():
        o_ref[...] = x_ref[...]

    pos_row = posmt_ref[0, pl.ds(e, 1), :]
    load = loads_ref[pl.program_id(0), e]
    rem = load % MOE_CHUNK
    nfull = load // MOE_CHUNK + (rem > MOE_TAIL).astype(jnp.int32)
    has_tail = (rem > 0) & (rem <= MOE_TAIL)
    tail_off = nfull * MOE_CHUNK

    def rows_of(off, size):
        return pl.ds(pl.multiple_of(off, MOE_TAIL), size)

    def for_each_chunk(fn):
        def full(k, carry):
            fn(k * MOE_CHUNK, MOE_CHUNK)
            return carry

        lax.fori_loop(0, nfull, full, 0)

        @pl.when(has_tail)
        def _():
            fn(tail_off, MOE_TAIL)

    def hits(off, size):
        slot = (lax.broadcasted_iota(jnp.int32, (size, tm), 0) + off).astype(F32)
        return pos_row == slot

    def expert(off, size):
        return _dot(_gate_up(hc_ref[rows_of(off, size), :], wgu_ref[0]), wd_ref[0])

    def first(off, size):
        onehot = jnp.where(hits(off, size), 1.0, 0.0).astype(BF16)
        hc_ref[rows_of(off, size), :] = _dot(onehot, h_ref[...]).astype(BF16)
        y_ref[rows_of(off, size), :] = expert(off, size)

    def middle(off, size):
        y_ref[rows_of(off, size), :] += expert(off, size)

    def last(off, size):
        hit = hits(off, size)
        gate = jnp.sum(jnp.where(hit, gatet_ref[0, pl.ds(e, 1), :], 0.0), axis=-1, keepdims=True)
        ys = ((y_ref[rows_of(off, size), :] + expert(off, size)) * gate).astype(BF16)
        onehot = jnp.where(hit, 1.0, 0.0).astype(BF16)
        o_ref[...] += lax.dot_general(onehot, ys, (((0,), (0,)), ((), ())), preferred_element_type=F32)

    @pl.when(f == 0)
    def _():
        for_each_chunk(first)

    if D_FF // FF_BLOCK > 2:
        @pl.when((f > 0) & (f < last_f))
        def _():
            for_each_chunk(middle)

    @pl.when(f == last_f)
    def _():
        for_each_chunk(last)


def _moe_sparse(x, h, c, wgu, wd, tm=MOE_TILE):
    t = x.shape[0]
    tf = FF_BLOCK
    tm = min(tm, t)
    posmt, gatet = _route(c, tm)
    loads = (jnp.max(posmt[:, :N_EXPERTS, :], axis=-1) + 1.0).astype(jnp.int32)
    return pl.pallas_call(
        _moe_sparse_kernel,
        grid_spec=pltpu.PrefetchScalarGridSpec(
            num_scalar_prefetch=1,
            grid=(t // tm, N_EXPERTS, D_FF // tf),
            in_specs=[
                pl.BlockSpec((tm, D_MODEL), lambda i, e, f, loads: (i, 0)),
                pl.BlockSpec((tm, D_MODEL), lambda i, e, f, loads: (i, 0)),
                pl.BlockSpec((1, LANES, tm), lambda i, e, f, loads: (i, 0, 0)),
                pl.BlockSpec((1, LANES, tm), lambda i, e, f, loads: (i, 0, 0)),
                pl.BlockSpec((1, D_MODEL, 2 * tf), lambda i, e, f, loads: (e, 0, f)),
                pl.BlockSpec((1, tf, D_MODEL), lambda i, e, f, loads: (e, f, 0)),
            ],
            out_specs=pl.BlockSpec((tm, D_MODEL), lambda i, e, f, loads: (i, 0)),
            scratch_shapes=[pltpu.VMEM((tm, D_MODEL), BF16), pltpu.VMEM((tm, D_MODEL), F32)],
        ),
        out_shape=jax.ShapeDtypeStruct((t, D_MODEL), F32),
        compiler_params=_cparams(("parallel", "arbitrary", "arbitrary")),
        name="moe_sparse",
    )(loads, x, h, posmt, gatet, wgu, wd)


def _rope_tables(seq):
    inv = 1.0 / (ROPE_THETA ** (jnp.arange(0, DIFF_DH, 2, dtype=F32) / DIFF_DH))
    ang = jnp.arange(seq, dtype=F32)[:, None] * inv[None, :]
    ang = jnp.concatenate([ang, ang], axis=-1)
    cos, sin = jnp.cos(ang), jnp.sin(ang)
    sign = jnp.where(jnp.arange(DIFF_DH) < DIFF_DH // 2, -1.0, 1.0).astype(F32)
    return jnp.tile(cos, (1, 2)), jnp.tile(sin * sign, (1, 2))


def _block_diag_ones():
    idx = np.arange(MXU_TILE) // DIFF_DH
    return jnp.asarray(idx[:, None] == idx[None, :], dtype=BF16)


def _gate_up_blocks(w_gate, w_up):
    pieces = []
    for f in range(D_FF // FF_BLOCK):
        cols = slice(f * FF_BLOCK, (f + 1) * FF_BLOCK)
        pieces += [w_gate[..., cols], w_up[..., cols]]
    return jnp.concatenate(pieces, axis=-1).astype(BF16)


def _prepare_layer(l, p):
    lam_init = 0.8 - 0.6 * math.exp(-0.3 * l)
    lq = p["lambda_q"][l].astype(F32)
    lk = p["lambda_k"][l].astype(F32)
    lam = jnp.exp(jnp.sum(lq[0] * lk[0])) - jnp.exp(jnp.sum(lq[1] * lk[1])) + lam_init
    scale = DIFF_DH ** -0.5 * math.log2(math.e)
    tile8 = lambda g: jnp.tile(g.astype(F32), HALF_WIDTH // g.shape[0])
    zeros = jnp.zeros((HALF_WIDTH,), F32)
    dq_gain = p["diff_q_norm"][l].astype(F32) * scale
    dk_gain = p["diff_k_norm"][l].astype(F32)
    gains = jnp.stack([
        tile8(dq_gain), tile8(dk_gain), zeros,
        tile8(p["na_q_norm"][l]) * scale, tile8(p["na_k_norm"][l]), zeros, zeros, zeros])
    score_bound = SCORE_BOUND_MARGIN * DIFF_DH * jnp.max(jnp.abs(dq_gain)) * jnp.max(jnp.abs(dk_gain))
    lay = dict(
        scal=jnp.stack([lam, score_bound]).astype(F32),
        out_scale=1.0 - lam_init,
        attn_norm=p["attn_norm"][l].reshape(1, D_MODEL).astype(F32),
        w_in=p["w_in"][l].astype(BF16),
        gains=gains,
        g_sub=p["diff_sub_norm"][l].reshape(1, LANES).astype(F32),
        na_bias=_na_bias_table(p["na_rpb"][l]),
        g_na_out=jnp.tile(p["na_out_norm"][l].astype(F32), 2).reshape(1, LANES),
        w_out=p["w_out"][l].astype(BF16),
        ffn_norm=p["ffn_norm"][l].reshape(1, D_MODEL).astype(F32),
    )
    i = l // 2
    if l % 2 == 0:
        lay.update(wgu=_gate_up_blocks(p["dense_w_gate"][i], p["dense_w_up"][i]),
                   wd=p["dense_w_down"][i].astype(BF16))
    else:
        wr = p["moe_router"][i].astype(F32)
        wr_hi = wr.astype(BF16)
        wr_lo = (wr - wr_hi.astype(F32)).astype(BF16)
        wr2 = jnp.zeros((D_MODEL, LANES), BF16)
        wr2 = wr2.at[:, :N_EXPERTS].set(wr_hi).at[:, N_EXPERTS:2 * N_EXPERTS].set(wr_lo)
        lay.update(wr=wr2, wgu=_gate_up_blocks(p["moe_w_gate"][i], p["moe_w_up"][i]),
                   wd=p["moe_w_down"][i].astype(BF16))
    return lay


def _run_trunk(x, layers, bd):
    b, s, _ = x.shape
    t = b * s
    cos, sin = _rope_tables(s)
    x = x.reshape(t, D_MODEL)
    for l, lay in enumerate(layers):
        proj, vt = _in_proj(x, lay["attn_norm"], lay["w_in"], lay["gains"], cos, sin, bd, s)
        proj3 = proj.reshape(b, s, IN_WIDTH)
        a_out = _diff_attn(proj3, vt, lay["scal"], lay["g_sub"], lay["out_scale"])
        b_out = _na_attn(proj3, lay["na_bias"], lay["g_na_out"])
        a_out = a_out.reshape(t, HALF_WIDTH)
        b_out = b_out.reshape(t, HALF_WIDTH)
        if l % 2 == 0:
            x = _swiglu(x, a_out, b_out, lay["w_out"], lay["ffn_norm"], lay["wgu"], lay["wd"])
        else:
            x, h, c = _out_proj(x, a_out, b_out, lay["w_out"], lay["ffn_norm"], lay["wr"])
            x = _moe_sparse(x, h, c, lay["wgu"], lay["wd"])
    return x.reshape(b, s, D_MODEL)


def kernel(x_prompt, x_sample, attn_norm, w_in, w_out, diff_q_norm, diff_k_norm, lambda_q, lambda_k, diff_sub_norm, na_q_norm, na_k_norm, na_rpb, na_out_norm, ffn_norm, dense_w_gate, dense_w_up, dense_w_down, moe_router, moe_w_gate, moe_w_up, moe_w_down):
    p = dict(attn_norm=attn_norm, w_in=w_in, w_out=w_out, diff_q_norm=diff_q_norm, diff_k_norm=diff_k_norm,
             lambda_q=lambda_q, lambda_k=lambda_k, diff_sub_norm=diff_sub_norm, na_q_norm=na_q_norm,
             na_k_norm=na_k_norm, na_rpb=na_rpb, na_out_norm=na_out_norm, ffn_norm=ffn_norm,
             dense_w_gate=dense_w_gate, dense_w_up=dense_w_up, dense_w_down=dense_w_down,
             moe_router=moe_router, moe_w_gate=moe_w_gate, moe_w_up=moe_w_up, moe_w_down=moe_w_down)
    layers = [_prepare_layer(l, p) for l in range(DEPTH)]
    bd = _block_diag_ones()
    return (_run_trunk(x_prompt, layers, bd), _run_trunk(x_sample, layers, bd))
```

```python
import functools
import math

import jax
import jax.numpy as jnp
import numpy as np
from jax import lax
from jax.experimental import pallas as pl
from jax.experimental.pallas import tpu as pltpu

D_MODEL = 1024
DEPTH = 4
GRID_W = 64
DIFF_HEADS = 4
DIFF_DH = 64
NA_HEADS = 8
NA_DH = 64
NA_KH = 8
NA_KW = 16
HALF_WIDTH = 512
SEC_DQ, SEC_DK, SEC_DV, SEC_NQ, SEC_NK, SEC_NV = range(6)
IN_WIDTH = 6 * HALF_WIDTH
D_FF = 2816
FF_BLOCK = 1408
N_EXPERTS = 8
ROPE_THETA = 10000.0
EPS = 1e-6
MASK_VALUE = -1e30
SCORE_BOUND_MARGIN = 1.02
MAX_SAFE_SCORE_BOUND = 60.0

LANES = 128
MXU_TILE = 256
VMEM_LIMIT = 56 * 1024 * 1024

BF16 = jnp.bfloat16
F32 = jnp.float32


def _cparams(sem):
    return pltpu.CompilerParams(dimension_semantics=sem, vmem_limit_bytes=VMEM_LIMIT)


def _dot(a, b):
    return jnp.dot(a, b, preferred_element_type=F32)


def _dot_nt(a, b):
    return lax.dot_general(a, b, (((1,), (1,)), ((), ())), preferred_element_type=F32)


def _rms(x, g):
    return x * lax.rsqrt(jnp.mean(x * x, axis=-1, keepdims=True) + EPS) * g


def _group_norm64(y, gain, bd_ref):
    parts = []
    for c in range(HALF_WIDTH // MXU_TILE):
        yc = y[:, c * MXU_TILE:(c + 1) * MXU_TILE]
        ss = _dot((yc * yc).astype(BF16), bd_ref[...])
        parts.append(yc * lax.rsqrt(ss * (1.0 / 64) + EPS))
    return jnp.concatenate(parts, axis=1) * gain


def _rope64(y, cos, sin_signed):
    n = y.shape[1]
    lane = lax.broadcasted_iota(jnp.int32, y.shape, 1)
    first_half = (lane % 64) < 32
    rot = jnp.where(first_half, pltpu.roll(y, n - 32, 1), pltpu.roll(y, 32, 1))
    return y * cos + rot * sin_signed


def _in_proj_kernel(x_ref, g_ref, w_ref, gains_ref, cos_ref, sin_ref, bd_ref, o_ref, vt_ref):
    x = x_ref[...]
    h = _rms(x, g_ref[...]).astype(BF16)
    cos = jnp.concatenate([cos_ref[...]] * (HALF_WIDTH // LANES), axis=1)
    sin = jnp.concatenate([sin_ref[...]] * (HALF_WIDTH // LANES), axis=1)
    for sec in range(6):
        y = _dot(h, w_ref[:, sec * HALF_WIDTH:(sec + 1) * HALF_WIDTH])
        if sec in (SEC_DQ, SEC_DK, SEC_NQ, SEC_NK):
            y = _group_norm64(y, gains_ref[sec:sec + 1, :], bd_ref)
        if sec in (SEC_DQ, SEC_DK):
            y = _rope64(y, cos, sin)
        o_ref[:, sec * HALF_WIDTH:(sec + 1) * HALF_WIDTH] = y.astype(BF16)
        if sec == SEC_DV:
            vt_ref[0] = y.T.astype(BF16)


def _in_proj(x, g, w, gains, cos, sin, bd, seq, tm=512):
    t = x.shape[0]
    tm = min(tm, seq)
    nseq = seq // tm
    return pl.pallas_call(
        _in_proj_kernel,
        grid=(t // tm,),
        in_specs=[
            pl.BlockSpec((tm, D_MODEL), lambda i: (i, 0)),
            pl.BlockSpec((1, D_MODEL), lambda i: (0, 0)),
            pl.BlockSpec((D_MODEL, IN_WIDTH), lambda i: (0, 0)),
            pl.BlockSpec((8, HALF_WIDTH), lambda i: (0, 0)),
            pl.BlockSpec((tm, LANES), lambda i: (i % nseq, 0)),
            pl.BlockSpec((tm, LANES), lambda i: (i % nseq, 0)),
            pl.BlockSpec((MXU_TILE, MXU_TILE), lambda i: (0, 0)),
        ],
        out_specs=[pl.BlockSpec((tm, IN_WIDTH), lambda i: (i, 0)),
                   pl.BlockSpec((1, HALF_WIDTH, tm), lambda i: (i // nseq, 0, i % nseq))],
        out_shape=[jax.ShapeDtypeStruct((t, IN_WIDTH), BF16),
                   jax.ShapeDtypeStruct((t // seq, HALF_WIDTH, seq), BF16)],
        compiler_params=_cparams(("parallel",)),
        name="in_proj",
    )(x, g, w, gains, cos, sin, bd)


def _split_maps(q):
    lane = lax.broadcasted_iota(jnp.int32, q.shape, 1)
    zero = jnp.zeros_like(q)
    return jnp.where(lane < DIFF_DH, q, zero), jnp.where(lane >= DIFF_DH, q, zero)


def _diff_attn_online_kernel(scal_ref, q_ref, k_ref, v_ref, g_ref, o_ref, acc0_ref, acc1_ref, l0_ref, l1_ref, *,
                             tk, out_scale):
    qm = _split_maps(q_ref[0])
    tq = q_ref.shape[1]
    accs = (acc0_ref, acc1_ref)
    ls = (l0_ref, l1_ref)
    for r in accs + ls:
        r[...] = jnp.zeros_like(r)

    def lane_fold(p):
        out = p[:, :LANES]
        for c in range(1, tk // LANES):
            out = out + p[:, c * LANES:(c + 1) * LANES]
        return out

    def body(j, ms):
        start = pl.multiple_of(j * tk, tk)
        k = k_ref[0, pl.ds(start, tk), :]
        v = v_ref[0, pl.ds(start, tk), :]
        new_ms = []
        for a in range(2):
            s = _dot_nt(qm[a], k)
            m_new = jnp.maximum(ms[a], jnp.max(s, axis=-1, keepdims=True))
            alpha = jnp.exp2(ms[a] - m_new)
            p = jnp.exp2(s - m_new)
            ls[a][...] = alpha * ls[a][...] + lane_fold(p)
            accs[a][...] = alpha * accs[a][...] + _dot(p.astype(BF16), v)
            new_ms.append(m_new)
        return tuple(new_ms)

    neg = jnp.full((tq, 1), -jnp.inf, F32)
    lax.fori_loop(0, k_ref.shape[1] // tk, body, (neg, neg))
    lam = scal_ref[0]
    l0 = jnp.sum(l0_ref[...], axis=-1, keepdims=True)
    l1 = jnp.sum(l1_ref[...], axis=-1, keepdims=True)
    o = acc0_ref[...] / l0 - lam * (acc1_ref[...] / l1)
    o_ref[0] = (_rms(o, g_ref[...]) * out_scale).astype(BF16)


def _diff_attn_t_kernel(scal_ref, q_ref, k_ref, vt_ref, g_ref, o_ref, acc0_ref, acc1_ref, l0_ref, l1_ref, *,
                        tk, out_scale):
    qm = _split_maps(q_ref[0])
    accs = (acc0_ref, acc1_ref)
    ls = (l0_ref, l1_ref)
    for r in accs + ls:
        r[...] = jnp.zeros_like(r)
    nk = k_ref.shape[1] // tk
    tq = q_ref.shape[1]
    bound = scal_ref[1]

    def body(j, carry):
        start = pl.multiple_of(j * tk, tk)
        k = k_ref[0, pl.ds(start, tk), :]
        vt = vt_ref[0, :, pl.ds(start, tk)]
        for a in range(2):
            p = jnp.exp2(_dot_nt(k, qm[a]) - bound)
            ls[a][...] += jnp.sum(p.reshape(tk // 8, 8, tq), axis=0)
            accs[a][...] += _dot(vt, p.astype(BF16))
        return carry

    lax.fori_loop(0, nk, body, 0, unroll=2)
    lam = scal_ref[0]
    l0 = jnp.sum(l0_ref[...], axis=0, keepdims=True)
    l1 = jnp.sum(l1_ref[...], axis=0, keepdims=True)
    ot = acc0_ref[...] / l0 - lam * (acc1_ref[...] / l1)
    ot = ot * lax.rsqrt(jnp.mean(ot * ot, axis=0, keepdims=True) + EPS)
    o_ref[0] = (ot.T * g_ref[...] * out_scale).astype(BF16)


DIFF_TQ = 1024
DIFF_TK_BOUNDED = 2048
DIFF_TK_ONLINE = 1024


def _diff_attn_call(proj, vt, scal, g_sub, out_scale, bounded):
    b, s, _ = proj.shape
    tq = min(DIFF_TQ, s)
    q_spec = pl.BlockSpec((1, tq, LANES), lambda bi, h, i: (bi, i, h))
    k_spec = pl.BlockSpec((1, s, LANES), lambda bi, h, i: (bi, 0, DIFF_HEADS + h))
    if bounded:
        kern = functools.partial(_diff_attn_t_kernel, tk=min(DIFF_TK_BOUNDED, s), out_scale=out_scale)
        v, v_spec = vt, pl.BlockSpec((1, LANES, s), lambda bi, h, i: (bi, h, 0))
        scratch = [pltpu.VMEM((LANES, tq), F32)] * 2 + [pltpu.VMEM((8, tq), F32)] * 2
    else:
        kern = functools.partial(_diff_attn_online_kernel, tk=min(DIFF_TK_ONLINE, s), out_scale=out_scale)
        v, v_spec = proj, pl.BlockSpec((1, s, LANES), lambda bi, h, i: (bi, 0, 2 * DIFF_HEADS + h))
        scratch = [pltpu.VMEM((tq, LANES), F32)] * 4
    return pl.pallas_call(
        kern,
        grid=(b, DIFF_HEADS, s // tq),
        in_specs=[pl.BlockSpec(memory_space=pltpu.SMEM), q_spec, k_spec, v_spec,
                  pl.BlockSpec((1, LANES), lambda bi, h, i: (0, 0))],
        out_specs=pl.BlockSpec((1, tq, LANES), lambda bi, h, i: (bi, i, h)),
        out_shape=jax.ShapeDtypeStruct((b, s, HALF_WIDTH), BF16),
        scratch_shapes=scratch,
        compiler_params=_cparams(("parallel", "parallel", "arbitrary")),
        name="diff_attn_t" if bounded else "diff_attn_online",
    )(scal, proj, proj, v, g_sub)


def _diff_attn(proj, vt, scal, g_sub, out_scale):
    return lax.cond(
        scal[1] <= MAX_SAFE_SCORE_BOUND,
        lambda: _diff_attn_call(proj, vt, scal, g_sub, out_scale, True),
        lambda: _diff_attn_call(proj, vt, scal, g_sub, out_scale, False))


NA_ROWS_PER_STEP = 8
NA_BLOCK = NA_ROWS_PER_STEP * GRID_W


def _na_attn_kernel(q_ref, kp_ref, kc_ref, kn_ref, vp_ref, vc_ref, vn_ref, bias_ref, g_ref, o_ref,
                    kbuf, vbuf, *, rows):
    i = pl.program_id(1)
    for n, (kr, vr) in enumerate(((kp_ref, vp_ref), (kc_ref, vc_ref), (kn_ref, vn_ref))):
        kbuf[n * NA_BLOCK:(n + 1) * NA_BLOCK, :] = kr[0]
        vbuf[n * NA_BLOCK:(n + 1) * NA_BLOCK, :] = vr[0]
    lane = lax.broadcasted_iota(jnp.int32, (GRID_W, LANES), 1)
    halves = (lane < NA_DH, lane >= NA_DH)

    def row_body(qr, carry):
        r = i * NA_ROWS_PER_STEP + qr
        r0 = jnp.clip(r - NA_KH // 2, 0, rows - NA_KH)
        start = pl.multiple_of((r0 - (i - 1) * NA_ROWS_PER_STEP) * GRID_W, GRID_W)
        rp = r - r0
        qrow = pl.multiple_of(qr * GRID_W, GRID_W)
        npairs = NA_HEADS // 2
        col = [slice(p * LANES, (p + 1) * LANES) for p in range(npairs)]
        scores = []
        for p in range(npairs):
            qp = q_ref[0, pl.ds(qrow, GRID_W), col[p]]
            kp = kbuf[pl.ds(start, NA_KH * GRID_W), col[p]]
            q2 = jnp.concatenate([jnp.where(h, qp, jnp.zeros_like(qp)) for h in halves], axis=0)
            scores.append(_dot_nt(q2, kp) + bias_ref[p, rp])
        probs = []
        for s in scores:
            m = jnp.max(s, axis=-1, keepdims=True)
            pe = jnp.exp2(s - m)
            probs.append((pe.astype(BF16), jnp.sum(pe, axis=-1, keepdims=True)))
        for p in range(npairs):
            vp = vbuf[pl.ds(start, NA_KH * GRID_W), col[p]]
            pe, l = probs[p]
            o_both = _dot(pe, vp) / l
            o = jnp.where(halves[0], o_both[:GRID_W], o_both[GRID_W:])
            o2 = o * o
            ss = [jnp.sum(jnp.where(h, o2, 0.0), axis=-1, keepdims=True) for h in halves]
            inv = jnp.where(halves[0], lax.rsqrt(ss[0] * (1.0 / NA_DH) + EPS),
                            lax.rsqrt(ss[1] * (1.0 / NA_DH) + EPS))
            o_ref[0, pl.ds(qrow, GRID_W), col[p]] = (o * inv * g_ref[...]).astype(BF16)
        return carry

    lax.fori_loop(0, NA_ROWS_PER_STEP, row_body, 0, unroll=True)


def _na_attn(proj, bias, g_out):
    b, s, _ = proj.shape
    rows = s // GRID_W
    nb = rows // NA_ROWS_PER_STEP
    qcol, kcol, vcol = SEC_NQ, SEC_NK, SEC_NV
    blk = (1, NA_BLOCK, HALF_WIDTH)

    def prev(bi, i, c):
        return (bi, jnp.maximum(i - 1, 0), c)

    def nxt(bi, i, c):
        return (bi, jnp.minimum(i + 1, nb - 1), c)

    return pl.pallas_call(
        functools.partial(_na_attn_kernel, rows=rows),
        grid=(b, nb),
        in_specs=[
            pl.BlockSpec(blk, lambda bi, i: (bi, i, qcol)),
            pl.BlockSpec(blk, lambda bi, i: prev(bi, i, kcol)),
            pl.BlockSpec(blk, lambda bi, i: (bi, i, kcol)),
            pl.BlockSpec(blk, lambda bi, i: nxt(bi, i, kcol)),
            pl.BlockSpec(blk, lambda bi, i: prev(bi, i, vcol)),
            pl.BlockSpec(blk, lambda bi, i: (bi, i, vcol)),
            pl.BlockSpec(blk, lambda bi, i: nxt(bi, i, vcol)),
            pl.BlockSpec((NA_HEADS // 2, NA_KH, 2 * GRID_W, NA_KH * GRID_W), lambda bi, i: (0, 0, 0, 0)),
            pl.BlockSpec((1, LANES), lambda bi, i: (0, 0)),
        ],
        out_specs=pl.BlockSpec(blk, lambda bi, i: (bi, i, 0)),
        out_shape=jax.ShapeDtypeStruct((b, s, HALF_WIDTH), BF16),
        scratch_shapes=[pltpu.VMEM((3 * NA_BLOCK, HALF_WIDTH), BF16),
                        pltpu.VMEM((3 * NA_BLOCK, HALF_WIDTH), BF16)],
        compiler_params=_cparams(("parallel", "arbitrary")),
        name="na_attn",
    )(proj, proj, proj, proj, proj, proj, proj, bias, g_out)


def _na_bias_table(rpb):
    qc = np.arange(GRID_W)
    kc = np.arange(GRID_W)
    win = np.clip(qc - NA_KW // 2, 0, GRID_W - NA_KW)
    valid = (kc[None, :] >= win[:, None]) & (kc[None, :] < win[:, None] + NA_KW)
    padded = jnp.pad(rpb.astype(F32) * math.log2(math.e), ((0, 0), (0, 0), (GRID_W, GRID_W)))
    by_col = jnp.stack([padded[:, :, GRID_W + NA_KW - 1 - c:2 * GRID_W + NA_KW - 1 - c] for c in range(GRID_W)],
                       axis=2)
    by_col = jnp.where(jnp.asarray(valid)[None, None], by_col, MASK_VALUE)
    t = jnp.stack([by_col[:, NA_KH - 1 - r:2 * NA_KH - 1 - r] for r in range(NA_KH)], axis=1)
    t = t.reshape(NA_HEADS // 2, 2, NA_KH, NA_KH, GRID_W, GRID_W)
    t = t.transpose(0, 2, 1, 4, 3, 5)
    return t.reshape(NA_HEADS // 2, NA_KH, 2 * GRID_W, NA_KH * GRID_W)


def _out_proj_router_kernel(x_ref, a_ref, b_ref, w_ref, g_ref, wr_ref, utri_ref, xo_ref, h_ref, pos_ref, gate_ref):
    x = x_ref[...] + _dot(a_ref[...], w_ref[:HALF_WIDTH, :]) + _dot(b_ref[...], w_ref[HALF_WIDTH:, :])
    xo_ref[...] = x
    h = _rms(x, g_ref[...])
    h_hi = h.astype(BF16)
    h_ref[...] = h_hi
    h_lo = (h - h_hi.astype(F32)).astype(BF16)
    wr = wr_ref[...]
    both = _dot(h_hi, wr)
    lane = lax.broadcasted_iota(jnp.int32, both.shape, 1).astype(F32)
    lo_part = pltpu.roll(both, LANES - N_EXPERTS, 1)
    logits = both + lo_part + _dot(h_lo, wr)
    logits = jnp.where(lane < N_EXPERTS, logits, -jnp.inf)
    m1 = jnp.max(logits, axis=-1, keepdims=True)
    i1 = jnp.min(jnp.where(logits == m1, lane, float(LANES)), axis=-1, keepdims=True)
    rest = jnp.where(lane == i1, -jnp.inf, logits)
    m2 = jnp.max(rest, axis=-1, keepdims=True)
    i2 = jnp.min(jnp.where(rest == m2, lane, float(LANES)), axis=-1, keepdims=True)
    e2 = jnp.exp(m2 - m1)
    g1 = 1.0 / (1.0 + e2)
    g2 = e2 * g1
    c = jnp.where(lane == i1, g1, jnp.where(lane == i2, g2, 0.0))
    carry = jnp.zeros((LANES, 1), F32)
    for b in range(c.shape[0] // TRI_BLOCK):
        rows = slice(b * TRI_BLOCK, (b + 1) * TRI_BLOCK)
        gate_t = c[rows, :].T
        sel_t = jnp.where(gate_t > 0.0, 1.0, 0.0)
        pos_t = _dot(sel_t.astype(BF16), utri_ref[...]) + carry
        pos_ref[0, :, rows] = jnp.where(gate_t > 0.0, pos_t, -1.0)
        gate_ref[0, :, rows] = gate_t
        carry = carry + jnp.sum(sel_t, axis=1, keepdims=True)


def _out_proj(x, a, b, w, g, wr):
    t = x.shape[0]
    tm = min(MOE_TILE, t)
    row = lambda i: (i, 0)
    const = lambda i: (0, 0)
    idx = np.arange(TRI_BLOCK)
    utri = jnp.asarray(idx[:, None] < idx[None, :], dtype=BF16)
    expert_major = jax.ShapeDtypeStruct((t // tm, LANES, tm), F32)
    return pl.pallas_call(
        _out_proj_router_kernel,
        grid=(t // tm,),
        in_specs=[
            pl.BlockSpec((tm, D_MODEL), row),
            pl.BlockSpec((tm, HALF_WIDTH), row),
            pl.BlockSpec((tm, HALF_WIDTH), row),
            pl.BlockSpec((D_MODEL, D_MODEL), const),
            pl.BlockSpec((1, D_MODEL), const),
            pl.BlockSpec((D_MODEL, LANES), const),
            pl.BlockSpec((TRI_BLOCK, TRI_BLOCK), const),
        ],
        out_specs=[pl.BlockSpec((tm, D_MODEL), row), pl.BlockSpec((tm, D_MODEL), row),
                   pl.BlockSpec((1, LANES, tm), lambda i: (i, 0, 0)),
                   pl.BlockSpec((1, LANES, tm), lambda i: (i, 0, 0))],
        out_shape=[jax.ShapeDtypeStruct((t, D_MODEL), F32), jax.ShapeDtypeStruct((t, D_MODEL), BF16),
                   expert_major, expert_major],
        compiler_params=_cparams(("parallel",)),
        name="out_proj",
    )(x, a, b, w, g, wr, utri)


def _gate_up(h, wgu):
    r = _dot(h, wgu)
    g, u = r[:, :FF_BLOCK], r[:, FF_BLOCK:]
    return (g * jax.nn.sigmoid(g) * u).astype(BF16)


def _swiglu_kernel(x_ref, a_ref, b_ref, wo_ref, g_ref, wgu_ref, wd_ref, o_ref, h_ref):
    @pl.when(pl.program_id(1) == 0)
    def _():
        x = x_ref[...] + _dot(a_ref[...], wo_ref[:HALF_WIDTH, :]) + _dot(b_ref[...], wo_ref[HALF_WIDTH:, :])
        o_ref[...] = x
        h_ref[...] = _rms(x, g_ref[...]).astype(BF16)

    o_ref[...] += _dot(_gate_up(h_ref[...], wgu_ref[...]), wd_ref[...])


def _swiglu(x, a, b, w_out, g, wgu, wd, tm=1024):
    t = x.shape[0]
    tf = FF_BLOCK
    tm = min(tm, t)
    return pl.pallas_call(
        _swiglu_kernel,
        grid=(t // tm, D_FF // tf),
        in_specs=[
            pl.BlockSpec((tm, D_MODEL), lambda i, f: (i, 0)),
            pl.BlockSpec((tm, HALF_WIDTH), lambda i, f: (i, 0)),
            pl.BlockSpec((tm, HALF_WIDTH), lambda i, f: (i, 0)),
            pl.BlockSpec((D_MODEL, D_MODEL), lambda i, f: (0, 0)),
            pl.BlockSpec((1, D_MODEL), lambda i, f: (0, 0)),
            pl.BlockSpec((D_MODEL, 2 * tf), lambda i, f: (0, f)),
            pl.BlockSpec((tf, D_MODEL), lambda i, f: (f, 0)),
        ],
        out_specs=pl.BlockSpec((tm, D_MODEL), lambda i, f: (i, 0)),
        out_shape=jax.ShapeDtypeStruct((t, D_MODEL), F32),
        scratch_shapes=[pltpu.VMEM((tm, D_MODEL), BF16)],
        compiler_params=_cparams(("parallel", "arbitrary")),
        name="swiglu",
    )(x, a, b, w_out, g, wgu, wd)


MOE_TILE = 1024
MOE_CHUNK = 256
MOE_TAIL = 128
TRI_BLOCK = 512


def _moe_sparse_kernel(loads_ref, x_ref, h_ref, posmt_ref, gatet_ref, wgu_ref, wd_ref, o_ref, hc_ref, y_ref):
    e = pl.program_id(1)
    f = pl.program_id(2)
    last_f = pl.num_programs(2) - 1
    tm = x_ref.shape[0]

    @pl.when((e == 0) & (f == 0))
    def _():
        o_ref[...] = x_ref[...]

    pos_row = posmt_ref[0, pl.ds(e, 1), :]
    load = loads_ref[pl.program_id(0), e]
    rem = load % MOE_CHUNK
    nfull = load // MOE_CHUNK + (rem > MOE_TAIL).astype(jnp.int32)
    has_tail = (rem > 0) & (rem <= MOE_TAIL)
    tail_off = nfull * MOE_CHUNK

    def rows_of(off, size):
        return pl.ds(pl.multiple_of(off, MOE_TAIL), size)

    def for_each_chunk(fn):
        def full(k, carry):
            fn(k * MOE_CHUNK, MOE_CHUNK)
            return carry

        lax.fori_loop(0, nfull, full, 0)

        @pl.when(has_tail)
        def _():
            fn(tail_off, MOE_TAIL)

    def hits(off, size):
        slot = (lax.broadcasted_iota(jnp.int32, (size, tm), 0) + off).astype(F32)
        return pos_row == slot

    def expert(off, size):
        return _dot(_gate_up(hc_ref[rows_of(off, size), :], wgu_ref[0]), wd_ref[0])

    def first(off, size):
        onehot = jnp.where(hits(off, size), 1.0, 0.0).astype(BF16)
        hc_ref[rows_of(off, size), :] = _dot(onehot, h_ref[...]).astype(BF16)
        y_ref[rows_of(off, size), :] = expert(off, size)

    def middle(off, size):
        y_ref[rows_of(off, size), :] += expert(off, size)

    def last(off, size):
        hit = hits(off, size)
        gate = jnp.sum(jnp.where(hit, gatet_ref[0, pl.ds(e, 1), :], 0.0), axis=-1, keepdims=True)
        ys = ((y_ref[rows_of(off, size), :] + expert(off, size)) * gate).astype(BF16)
        onehot = jnp.where(hit, 1.0, 0.0).astype(BF16)
        o_ref[...] += lax.dot_general(onehot, ys, (((0,), (0,)), ((), ())), preferred_element_type=F32)

    @pl.when(f == 0)
    def _():
        for_each_chunk(first)

    if D_FF // FF_BLOCK > 2:
        @pl.when((f > 0) & (f < last_f))
        def _():
            for_each_chunk(middle)

    @pl.when(f == last_f)
    def _():
        for_each_chunk(last)


def _moe_sparse(x, h, posmt, gatet, wgu, wd, tm=MOE_TILE):
    t = x.shape[0]
    tf = FF_BLOCK
    tm = min(tm, t)
    loads = (jnp.max(posmt[:, :N_EXPERTS, :], axis=-1) + 1.0).astype(jnp.int32)
    return pl.pallas_call(
        _moe_sparse_kernel,
        grid_spec=pltpu.PrefetchScalarGridSpec(
            num_scalar_prefetch=1,
            grid=(t // tm, N_EXPERTS, D_FF // tf),
            in_specs=[
                pl.BlockSpec((tm, D_MODEL), lambda i, e, f, loads: (i, 0)),
                pl.BlockSpec((tm, D_MODEL), lambda i, e, f, loads: (i, 0)),
                pl.BlockSpec((1, LANES, tm), lambda i, e, f, loads: (i, 0, 0)),
                pl.BlockSpec((1, LANES, tm), lambda i, e, f, loads: (i, 0, 0)),
                pl.BlockSpec((1, D_MODEL, 2 * tf), lambda i, e, f, loads: (e, 0, f)),
                pl.BlockSpec((1, tf, D_MODEL), lambda i, e, f, loads: (e, f, 0)),
            ],
            out_specs=pl.BlockSpec((tm, D_MODEL), lambda i, e, f, loads: (i, 0)),
            scratch_shapes=[pltpu.VMEM((tm, D_MODEL), BF16), pltpu.VMEM((tm, D_MODEL), F32)],
        ),
        out_shape=jax.ShapeDtypeStruct((t, D_MODEL), F32),
        compiler_params=_cparams(("parallel", "arbitrary", "arbitrary")),
        name="moe_sparse",
    )(loads, x, h, posmt, gatet, wgu, wd)


def _rope_tables(seq):
    inv = 1.0 / (ROPE_THETA ** (jnp.arange(0, DIFF_DH, 2, dtype=F32) / DIFF_DH))
    ang = jnp.arange(seq, dtype=F32)[:, None] * inv[None, :]
    ang = jnp.concatenate([ang, ang], axis=-1)
    cos, sin = jnp.cos(ang), jnp.sin(ang)
    sign = jnp.where(jnp.arange(DIFF_DH) < DIFF_DH // 2, -1.0, 1.0).astype(F32)
    return jnp.tile(cos, (1, 2)), jnp.tile(sin * sign, (1, 2))


def _block_diag_ones():
    idx = np.arange(MXU_TILE) // DIFF_DH
    return jnp.asarray(idx[:, None] == idx[None, :], dtype=BF16)


def _gate_up_blocks(w_gate, w_up):
    pieces = []
    for f in range(D_FF // FF_BLOCK):
        cols = slice(f * FF_BLOCK, (f + 1) * FF_BLOCK)
        pieces += [w_gate[..., cols], w_up[..., cols]]
    return jnp.concatenate(pieces, axis=-1).astype(BF16)


def _prepare_layer(l, p):
    lam_init = 0.8 - 0.6 * math.exp(-0.3 * l)
    lq = p["lambda_q"][l].astype(F32)
    lk = p["lambda_k"][l].astype(F32)
    lam = jnp.exp(jnp.sum(lq[0] * lk[0])) - jnp.exp(jnp.sum(lq[1] * lk[1])) + lam_init
    scale = DIFF_DH ** -0.5 * math.log2(math.e)
    tile8 = lambda g: jnp.tile(g.astype(F32), HALF_WIDTH // g.shape[0])
    zeros = jnp.zeros((HALF_WIDTH,), F32)
    dq_gain = p["diff_q_norm"][l].astype(F32) * scale
    dk_gain = p["diff_k_norm"][l].astype(F32)
    gains = jnp.stack([
        tile8(dq_gain), tile8(dk_gain), zeros,
        tile8(p["na_q_norm"][l]) * scale, tile8(p["na_k_norm"][l]), zeros, zeros, zeros])
    score_bound = SCORE_BOUND_MARGIN * DIFF_DH * jnp.max(jnp.abs(dq_gain)) * jnp.max(jnp.abs(dk_gain))
    lay = dict(
        scal=jnp.stack([lam, score_bound]).astype(F32),
        out_scale=1.0 - lam_init,
        attn_norm=p["attn_norm"][l].reshape(1, D_MODEL).astype(F32),
        w_in=p["w_in"][l].astype(BF16),
        gains=gains,
        g_sub=p["diff_sub_norm"][l].reshape(1, LANES).astype(F32),
        na_bias=_na_bias_table(p["na_rpb"][l]),
        g_na_out=jnp.tile(p["na_out_norm"][l].astype(F32), 2).reshape(1, LANES),
        w_out=p["w_out"][l].astype(BF16),
        ffn_norm=p["ffn_norm"][l].reshape(1, D_MODEL).astype(F32),
    )
    i = l // 2
    if l % 2 == 0:
        lay.update(wgu=_gate_up_blocks(p["dense_w_gate"][i], p["dense_w_up"][i]),
                   wd=p["dense_w_down"][i].astype(BF16))
    else:
        wr = p["moe_router"][i].astype(F32)
        wr_hi = wr.astype(BF16)
        wr_lo = (wr - wr_hi.astype(F32)).astype(BF16)
        wr2 = jnp.zeros((D_MODEL, LANES), BF16)
        wr2 = wr2.at[:, :N_EXPERTS].set(wr_hi).at[:, N_EXPERTS:2 * N_EXPERTS].set(wr_lo)
        lay.update(wr=wr2, wgu=_gate_up_blocks(p["moe_w_gate"][i], p["moe_w_up"][i]),
                   wd=p["moe_w_down"][i].astype(BF16))
    return lay


def _run_trunk(x, layers, bd):
    b, s, _ = x.shape
    t = b * s
    cos, sin = _rope_tables(s)
    x = x.reshape(t, D_MODEL)
    for l, lay in enumerate(layers):
        proj, vt = _in_proj(x, lay["attn_norm"], lay["w_in"], lay["gains"], cos, sin, bd, s)
        proj3 = proj.reshape(b, s, IN_WIDTH)
        a_out = _diff_attn(proj3, vt, lay["scal"], lay["g_sub"], lay["out_scale"])
        b_out = _na_attn(proj3, lay["na_bias"], lay["g_na_out"])
        a_out = a_out.reshape(t, HALF_WIDTH)
        b_out = b_out.reshape(t, HALF_WIDTH)
        if l % 2 == 0:
            x = _swiglu(x, a_out, b_out, lay["w_out"], lay["ffn_norm"], lay["wgu"], lay["wd"])
        else:
            x, h, posmt, gatet = _out_proj(x, a_out, b_out, lay["w_out"], lay["ffn_norm"], lay["wr"])
            x = _moe_sparse(x, h, posmt, gatet, lay["wgu"], lay["wd"])
    return x.reshape(b, s, D_MODEL)


def kernel(x_prompt, x_sample, attn_norm, w_in, w_out, diff_q_norm, diff_k_norm, lambda_q, lambda_k, diff_sub_norm, na_q_norm, na_k_norm, na_rpb, na_out_norm, ffn_norm, dense_w_gate, dense_w_up, dense_w_down, moe_router, moe_w_gate, moe_w_up, moe_w_down):
    p = dict(attn_norm=attn_norm, w_in=w_in, w_out=w_out, diff_q_norm=diff_q_norm, diff_k_norm=diff_k_norm,
             lambda_q=lambda_q, lambda_k=lambda_k, diff_sub_norm=diff_sub_norm, na_q_norm=na_q_norm,
             na_k_norm=na_k_norm, na_rpb=na_rpb, na_out_norm=na_out_norm, ffn_norm=ffn_norm,
             dense_w_gate=dense_w_gate, dense_w_up=dense_w_up, dense_w_down=dense_w_down,
             moe_router=moe_router, moe_w_gate=moe_w_gate, moe_w_up=moe_w_up, moe_w_down=moe_w_down)
    layers = [_prepare_layer(l, p) for l in range(DEPTH)]
    bd = _block_diag_ones()
    return (_run_trunk(x_prompt, layers, bd), _run_trunk(x_sample, layers, bd))
```
